```python
import math
import jax, jax.numpy as jnp
from jax import lax
import numpy as np

D_MODEL = 1024
BATCH = 16
SEQ = 256
DEPTH = 2
DEC_BATCH = 4
DEC_SEQ = 4096
PAST_LEN = 512

GRID_W = 64
RW_WIDTH = 512
RW_HEAD = 64
RW_HEADS = RW_WIDTH // RW_HEAD
LORA_W = 64
LORA_A = 64
LORA_G = 128
GN_EPS = 64e-5
HY_WIDTH = 512
HY_ORDER = 2
HY_BANDS = 16
HY_EMB = 2 * HY_BANDS + 1
HY_FFN = 64
HY_DECAY_TARGET = 1e-2
HY_DECAY_SHORT_PCT = 0.3
HY_DECAY_LONG_PCT = 1.5
RW_COLS = 3 * RW_WIDTH + LORA_W + LORA_A + LORA_G
HY_COLS = (HY_ORDER + 1) * HY_WIDTH
IN_COLS = RW_COLS + HY_COLS + 2 * D_MODEL
RW_SPLITS = [RW_WIDTH, 2 * RW_WIDTH, 3 * RW_WIDTH, 3 * RW_WIDTH + LORA_W, 3 * RW_WIDTH + LORA_W + LORA_A]
N_EXPERTS = 32
TOP_K = 4
D_FF = D_MODEL
SWIGLU_ALPHA = 1.702
SWIGLU_LIMIT = 7.0
MOE_BLOCK = 128
DN_ALPHA = (2 * DEPTH) ** 0.25
DN_BETA = (8 * DEPTH) ** -0.25
LN_EPS = 1e-5

kernel_name = 'bidir_rwkv7_hyena_moe_diffusion_step'


def _layer_norm(x, g, b):
    xf = x.astype(jnp.float32)
    mu = jnp.mean(xf, -1, keepdims=True)
    var = jnp.mean(jnp.square(xf - mu), -1, keepdims=True)
    return ((xf - mu) * lax.rsqrt(var + LN_EPS) * g + b).astype(x.dtype)


def _neighbours(u, grid_w):
    bsz, n, ch = u.shape
    seg = n if grid_w is None else grid_w
    rows = n // seg
    v = jnp.pad(u.reshape(bsz, rows, seg, ch), ((0, 0), (0, 0), (1, 1), (0, 0)))
    return v[:, :, :-2].reshape(bsz, n, ch), v[:, :, 2:].reshape(bsz, n, ch)


def _rwkv_scan(s0, r, w, kk, b, v, k, reverse):
    def tm(t):
        return jnp.moveaxis(t.astype(jnp.float32), 1, 0)

    def step(s, inp):
        r_t, w_t, kk_t, b_t, v_t, k_t = inp
        s_kk = jnp.einsum('bhvk,bhk->bhv', s, kk_t)
        s = s * w_t[:, :, None, :] - s_kk[..., None] * b_t[:, :, None, :] + v_t[..., None] * k_t[:, :, None, :]
        return s, jnp.einsum('bhvk,bhk->bhv', s, r_t)

    s_fin, ys = lax.scan(step, s0.astype(jnp.float32), (tm(r), tm(w), tm(kk), tm(b), tm(v), tm(k)), reverse=reverse)
    return s_fin, jnp.moveaxis(ys, 0, 1)


def _rwkv_branch(cols, s0, p, grid_w):
    bsz, n, _ = cols.shape
    prev, nxt = _neighbours(cols, grid_w)
    cols = cols + p['mu_shift'] * (0.5 * (prev + nxt) - cols)
    r, k, v, wd, ad, gd = jnp.split(cols, RW_SPLITS, axis=-1)

    def heads(t):
        return t.reshape(bsz, n, RW_HEADS, RW_HEAD)

    kk = heads((k * p['k_k']).astype(jnp.float32))
    kk = kk / jnp.maximum(jnp.sqrt(jnp.sum(kk * kk, -1, keepdims=True)), 1e-12)
    g = jax.nn.sigmoid(gd) @ p['g_up']
    tw = jnp.tanh(wd)
    ys, bonuses, finals = [], [], []
    for d in range(2):
        w_logit = (p['w0'][d] + tw @ p['w_lora_up'][d]).astype(jnp.float32)
        decay = jnp.exp(-jnp.exp(-jax.nn.softplus(-w_logit) - 0.5))
        a = jax.nn.sigmoid(p['a0'][d] + ad @ p['a_lora_up'][d])
        k_d = k * (1.0 + (a - 1.0) * p['k_a'])
        s_fin, y = _rwkv_scan(s0[:, d], heads(r), heads(decay), kk, kk * heads(a), heads(v), heads(k_d), d == 1)
        ys.append(y)
        finals.append(s_fin)
        bonuses.append(jnp.sum(heads(r * k_d) * p['r_k'][d], -1, keepdims=True) * heads(v))
    y = ys[0] + ys[1]
    mu = jnp.mean(y, -1, keepdims=True)
    var = jnp.mean(jnp.square(y - mu), -1, keepdims=True)
    yn = ((y - mu) * lax.rsqrt(var + GN_EPS)).reshape(bsz, n, RW_WIDTH) * p['gn_g'] + p['gn_b']
    out = (yn + (bonuses[0] + bonuses[1]).reshape(bsz, n, RW_WIDTH)) * g
    return out.astype(cols.dtype), jnp.stack(finals, axis=1)


def _hyena_filters(n, p):
    pos = jnp.arange(n, dtype=jnp.float32)
    t = (pos / n)[:, None]
    bands = jnp.linspace(1e-4, HY_BANDS - 1, HY_BANDS, dtype=jnp.float32)
    ang = (2.0 * math.pi / n) * pos[:, None] * bands[None, :]
    z = jnp.concatenate([t, jnp.cos(ang), jnp.sin(ang)], axis=-1)
    hdn = jnp.sin(p['hy_f_freq'] * (z @ p['hy_f_w1'] + p['hy_f_b1']))
    hdn = jnp.sin(p['hy_f_freq'] * (hdn @ p['hy_f_w2'] + p['hy_f_b2']))
    h = (hdn @ p['hy_f_w3']).astype(jnp.float32).reshape(n, HY_ORDER, 2, HY_WIDTH)
    max_decay = math.log(HY_DECAY_TARGET) / HY_DECAY_SHORT_PCT
    min_decay = math.log(HY_DECAY_TARGET) / HY_DECAY_LONG_PCT
    deltas = jnp.abs(jnp.linspace(min_decay, max_decay, HY_WIDTH, dtype=jnp.float32))
    window = jnp.exp(-t * deltas[None, :])
    return h * window[:, None, None, :]


def _long_conv(z, h_fwd, h_bwd, skip):
    n = z.shape[1]
    f = jnp.concatenate([h_fwd, jnp.zeros_like(h_fwd[:1]), h_bwd[:0:-1]], axis=0)
    zf = jnp.fft.rfft(z.astype(jnp.float32), n=2 * n, axis=1)
    ff = jnp.fft.rfft(f, n=2 * n, axis=0)
    y = jnp.fft.irfft(zf * ff[None], n=2 * n, axis=1)[:, :n]
    return (y + z * skip).astype(z.dtype)


def _hyena_branch(cols, p, grid_w):
    prev, nxt = _neighbours(cols, grid_w)
    w3 = p['hy_conv_w']
    cols = prev * w3[0] + cols * w3[1] + nxt * w3[2] + p['hy_conv_b']
    u, x1, x2 = jnp.split(cols, 3, axis=-1)
    h = _hyena_filters(cols.shape[1], p)
    z = x1 * _long_conv(u, h[:, 0, 0], h[:, 0, 1], p['hy_skip'][0])
    z = x2 * _long_conv(z, h[:, 1, 0], h[:, 1, 1], p['hy_skip'][1])
    return z


def _token_mixer(h, s0, p, grid_w):
    proj = h @ p['w_in']
    rw_cols, hy_cols, gate_cols = jnp.split(proj, [RW_COLS, RW_COLS + HY_COLS], axis=-1)
    y_a, s_fin = _rwkv_branch(rw_cols, s0, p, grid_w)
    y_b = _hyena_branch(hy_cols, p, grid_w)
    g_a, g_b = jnp.split(jax.nn.sigmoid(gate_cols), 2, axis=-1)
    merged = g_a * (y_a @ p['w_pa']) + g_b * (y_b @ p['w_pb'])
    return merged @ p['w_out'], s_fin


def _moe(h, p):
    bsz, n, dm = h.shape
    x = h.reshape(-1, dm)
    m = x.shape[0] * TOP_K
    logits = (x @ p['router_w'] + p['router_b']).astype(jnp.float32)
    top_v, top_i = lax.top_k(logits, TOP_K)
    gates = jax.nn.softmax(top_v, axis=-1)
    flat_e = top_i.reshape(-1)
    order = jnp.argsort(flat_e)
    e_sorted = flat_e[order]
    tok_sorted = order // TOP_K
    gate_sorted = gates.reshape(-1)[order]
    sizes = jnp.bincount(flat_e, length=N_EXPERTS)
    padded = (sizes + MOE_BLOCK - 1) // MOE_BLOCK * MOE_BLOCK
    pad_end = jnp.cumsum(padded)
    pad_start = pad_end - padded
    grp_start = jnp.cumsum(sizes) - sizes
    dest = pad_start[e_sorted] + jnp.arange(m) - grp_start[e_sorted]
    n_blocks = -(-(m + N_EXPERTS * (MOE_BLOCK - 1)) // MOE_BLOCK)
    x_pad = jnp.zeros((n_blocks * MOE_BLOCK, dm), x.dtype).at[dest].set(x[tok_sorted])
    block_e = jnp.minimum(jnp.searchsorted(pad_end, jnp.arange(n_blocks) * MOE_BLOCK, side='right'), N_EXPERTS - 1)

    def expert_block(args):
        xb, e = args
        up = xb @ p['ex_w_up'][e] + p['ex_b_up'][e]
        glu = jnp.minimum(up[:, ::2], SWIGLU_LIMIT)
        lin = jnp.clip(up[:, 1::2], -SWIGLU_LIMIT, SWIGLU_LIMIT)
        act = glu * jax.nn.sigmoid(SWIGLU_ALPHA * glu) * (lin + 1.0)
        return act @ p['ex_w_down'][e] + p['ex_b_down'][e]

    y_pad = lax.map(expert_block, (x_pad.reshape(n_blocks, MOE_BLOCK, dm), block_e))
    y_rows = y_pad.reshape(-1, dm)[dest] * gate_sorted[:, None].astype(x.dtype)
    out = jnp.zeros_like(x).at[tok_sorted].add(y_rows)
    return out.reshape(bsz, n, dm)


def _trunk_layer(x, cond, s0, p, grid_w):
    mod = jax.nn.silu(cond) @ p['w_mod'] + p['b_mod']
    sh1, sc1, g1, sh2, sc2, g2 = jnp.split(mod[:, None, :], 6, axis=-1)
    mix, s_fin = _token_mixer(x * (1.0 + sc1) + sh1, s0, p, grid_w)
    x = _layer_norm(DN_ALPHA * x + g1 * mix, p['ln1_g'], p['ln1_b'])
    x = _layer_norm(DN_ALPHA * x + g2 * _moe(x * (1.0 + sc2) + sh2, p), p['ln2_g'], p['ln2_b'])
    return x, s_fin


def setup_inputs(seed: int = 0) -> dict:
    key = jax.random.key(seed)
    ks = iter(jax.random.split(key, 64))

    def nrm(shape, scale):
        return jax.random.normal(next(ks), shape, jnp.float32) * scale

    def unif(shape, lo, hi):
        return jax.random.uniform(next(ks), shape, jnp.float32, lo, hi)

    L = DEPTH
    D = D_MODEL
    return {
        'x_prompt': nrm((BATCH, SEQ, D), 1.0),
        'x_sample': nrm((DEC_BATCH, DEC_SEQ, D), 1.0),
        'c': nrm((DEC_BATCH, D), 1.0),
        'state_rwkv': nrm((DEC_BATCH, L, 2, RW_HEADS, RW_HEAD, RW_HEAD), 1.0),
        'c_ctx': nrm((D,), 1.0),
        'w_mod': nrm((L, D, 6 * D), 0.5 * D ** -0.5),
        'b_mod': nrm((L, 6 * D), 0.02),
        'w_in': nrm((L, D, IN_COLS), D ** -0.5),
        'mu_shift': unif((L, RW_COLS), 0.0, 1.0),
        'w0': unif((L, 2, RW_WIDTH), -6.0, 1.0),
        'w_lora_up': nrm((L, 2, LORA_W, RW_WIDTH), 0.1 * LORA_W ** -0.5),
        'a0': nrm((L, 2, RW_WIDTH), 0.1),
        'a_lora_up': nrm((L, 2, LORA_A, RW_WIDTH), 0.1 * LORA_A ** -0.5),
        'g_up': nrm((L, LORA_G, RW_WIDTH), LORA_G ** -0.5),
        'k_k': 0.85 + nrm((L, RW_WIDTH), 0.02),
        'k_a': 1.0 + nrm((L, RW_WIDTH), 0.02),
        'r_k': nrm((L, 2, RW_HEADS, RW_HEAD), 0.1),
        'gn_g': 1.0 + nrm((L, RW_WIDTH), 0.02),
        'gn_b': nrm((L, RW_WIDTH), 0.02),
        'hy_conv_w': nrm((L, 3, HY_COLS), 3 ** -0.5),
        'hy_conv_b': nrm((L, HY_COLS), 0.02),
        'hy_f_w1': nrm((L, HY_EMB, HY_FFN), HY_EMB ** -0.5),
        'hy_f_b1': nrm((L, HY_FFN), 0.02),
        'hy_f_w2': nrm((L, HY_FFN, HY_FFN), HY_FFN ** -0.5),
        'hy_f_b2': nrm((L, HY_FFN), 0.02),
        'hy_f_freq': 1.0 + nrm((L, HY_FFN), 0.02),
        'hy_f_w3': nrm((L, HY_FFN, HY_ORDER * 2 * HY_WIDTH), 0.1 * HY_FFN ** -0.5),
        'hy_skip': nrm((L, HY_ORDER, HY_WIDTH), 1.0),
        'w_pa': nrm((L, RW_WIDTH, D), RW_WIDTH ** -0.5),
        'w_pb': nrm((L, HY_WIDTH, D), HY_WIDTH ** -0.5),
        'w_out': nrm((L, D, D), DN_BETA * D ** -0.5),
        'ln1_g': 1.0 + nrm((L, D), 0.02),
        'ln1_b': nrm((L, D), 0.02),
        'ln2_g': 1.0 + nrm((L, D), 0.02),
        'ln2_b': nrm((L, D), 0.02),
        'router_w': nrm((L, D, N_EXPERTS), D ** -0.5),
        'router_b': nrm((L, N_EXPERTS), 0.01),
        'ex_w_up': nrm((L, N_EXPERTS, D, 2 * D_FF), D ** -0.5),
        'ex_b_up': nrm((L, N_EXPERTS, 2 * D_FF), 0.01),
        'ex_w_down': nrm((L, N_EXPERTS, D_FF, D), DN_BETA * D_FF ** -0.5),
        'ex_b_down': nrm((L, N_EXPERTS, D), 0.01),
    }


def reference(x_prompt, x_sample, c, state_rwkv, c_ctx, w_mod, b_mod, w_in, mu_shift, w0, w_lora_up,
              a0, a_lora_up, g_up, k_k, k_a, r_k, gn_g, gn_b, hy_conv_w, hy_conv_b, hy_f_w1, hy_f_b1,
              hy_f_w2, hy_f_b2, hy_f_freq, hy_f_w3, hy_skip, w_pa, w_pb, w_out, ln1_g, ln1_b, ln2_g,
              ln2_b, router_w, router_b, ex_w_up, ex_b_up, ex_w_down, ex_b_down):
    y_p = x_prompt
    y_s = x_sample
    s_ctx0 = jnp.zeros((x_prompt.shape[0], 2, RW_HEADS, RW_HEAD, RW_HEAD), jnp.float32)
    ctx_cond = c_ctx[None, :]
    new_states = []
    for l in range(DEPTH):
        p = dict(w_mod=w_mod[l], b_mod=b_mod[l], w_in=w_in[l], mu_shift=mu_shift[l], w0=w0[l],
                 w_lora_up=w_lora_up[l], a0=a0[l], a_lora_up=a_lora_up[l], g_up=g_up[l], k_k=k_k[l],
                 k_a=k_a[l], r_k=r_k[l], gn_g=gn_g[l], gn_b=gn_b[l], hy_conv_w=hy_conv_w[l],
                 hy_conv_b=hy_conv_b[l], hy_f_w1=hy_f_w1[l], hy_f_b1=hy_f_b1[l], hy_f_w2=hy_f_w2[l],
                 hy_f_b2=hy_f_b2[l], hy_f_freq=hy_f_freq[l], hy_f_w3=hy_f_w3[l], hy_skip=hy_skip[l],
                 w_pa=w_pa[l], w_pb=w_pb[l], w_out=w_out[l], ln1_g=ln1_g[l], ln1_b=ln1_b[l],
                 ln2_g=ln2_g[l], ln2_b=ln2_b[l], router_w=router_w[l], router_b=router_b[l],
                 ex_w_up=ex_w_up[l], ex_b_up=ex_b_up[l], ex_w_down=ex_w_down[l], ex_b_down=ex_b_down[l])
        y_p, s_ctx = _trunk_layer(y_p, ctx_cond, s_ctx0, p, None)
        new_states.append(s_ctx.astype(x_prompt.dtype))
        y_s, _ = _trunk_layer(y_s, c, state_rwkv[:, l], p, GRID_W)
    new_state_rwkv = jnp.stack(new_states, axis=1)
    return (y_p, y_s, new_state_rwkv)
```

```python
import functools
import math

import numpy as np
import jax
import jax.numpy as jnp
from jax import lax
from jax.experimental import pallas as pl
from jax.experimental.pallas import tpu as pltpu

F32 = jnp.float32
BF16 = jnp.bfloat16

D_MODEL = 1024
GRID_W = 64
RW_WIDTH = 512
RW_HEAD = 64
RW_HEADS = RW_WIDTH // RW_HEAD
LORA_W = 64
LORA_A = 64
LORA_G = 128
GN_EPS = 64e-5
HY_WIDTH = 512
HY_ORDER = 2
HY_BANDS = 16
HY_EMB = 2 * HY_BANDS + 1
HY_FFN = 64
HY_DECAY_TARGET = 1e-2
HY_DECAY_SHORT_PCT = 0.3
HY_DECAY_LONG_PCT = 1.5
RW_COLS = 3 * RW_WIDTH + LORA_W + LORA_A + LORA_G
HY_COLS = (HY_ORDER + 1) * HY_WIDTH
IN_COLS = RW_COLS + HY_COLS + 2 * D_MODEL
N_EXPERTS = 32
TOP_K = 4
D_FF = D_MODEL
SWIGLU_ALPHA = 1.702
SWIGLU_LIMIT = 7.0
LN_EPS = 1e-5

TOKEN_BLOCK = 256
SCAN_CHUNK = 64
INV_BLOCK = 16
MOE_ROWS = 256
LANES = 128
VMEM_LIMIT = 56 * 1024 * 1024


def _cparams(sem, vmem=None):
    return pltpu.CompilerParams(dimension_semantics=sem, vmem_limit_bytes=vmem)


def _bdot(a, b):
    return jnp.dot(a.astype(BF16), b.astype(BF16), preferred_element_type=F32)


def _dot_nt(a, b):
    return lax.dot_general(a.astype(BF16), b.astype(BF16), (((1,), (1,)), ((), ())), preferred_element_type=F32)


def _dot_tn(a, b):
    return lax.dot_general(a.astype(BF16), b.astype(BF16), (((0,), (0,)), ((), ())), preferred_element_type=F32)


def _split2(x):
    hi = x.astype(BF16)
    lo = (x - hi.astype(F32)).astype(BF16)
    return hi, lo


def _split3(x):
    hi = x.astype(BF16)
    r1 = x - hi.astype(F32)
    mid = r1.astype(BF16)
    lo = (r1 - mid.astype(F32)).astype(BF16)
    return hi, mid, lo


def _dot_x3(x, g_bf16):
    hi, mid, lo = _split3(x)
    d = functools.partial(jnp.dot, preferred_element_type=F32)
    return d(hi, g_bf16) + (d(mid, g_bf16) + d(lo, g_bf16))


def _dot3(x, g_hi, g_lo):
    x_hi, x_lo = _split2(x)
    d = functools.partial(jnp.dot, preferred_element_type=F32)
    return d(x_hi, g_hi) + (d(x_hi, g_lo) + d(x_lo, g_hi))


def _sigmoid(x):
    return 1.0 / (1.0 + jnp.exp(-x))


def _layer_norm(x, g, b):
    mu = jnp.mean(x, axis=-1, keepdims=True)
    xc = x - mu
    var = jnp.mean(xc * xc, axis=-1, keepdims=True)
    return xc * lax.rsqrt(var + LN_EPS) * g + b


def _mod_body(c_ref, w_ref, b_ref, o_ref):
    c = c_ref[...]
    o_ref[...] = _bdot(c * _sigmoid(c), w_ref[...]) + b_ref[...]


def _mod_call(cond8, w_mod, b_mod):
    depth = w_mod.shape[0]
    tn = 1536
    return pl.pallas_call(
        _mod_body,
        out_shape=jax.ShapeDtypeStruct((depth, 8, 6 * D_MODEL), F32),
        grid=(depth, 6 * D_MODEL // tn),
        in_specs=[pl.BlockSpec((8, D_MODEL), lambda l, j: (0, 0)),
                  pl.BlockSpec((None, D_MODEL, tn), lambda l, j: (l, 0, j)),
                  pl.BlockSpec((None, 1, tn), lambda l, j: (l, 0, j))],
        out_specs=pl.BlockSpec((None, 8, tn), lambda l, j: (l, 0, j)),
        compiler_params=_cparams(("parallel", "parallel"), VMEM_LIMIT),
        name="adaln_mod",
    )(cond8, w_mod, b_mod.reshape(depth, 1, 6 * D_MODEL))


def _inproj_body(x_ref, sh_ref, sc_ref, w_ref, rw_ref, hy_ref, gt_ref):
    h = (x_ref[...] * (1.0 + sc_ref[...]) + sh_ref[...]).astype(BF16)
    d = functools.partial(jnp.dot, preferred_element_type=F32)
    rw_ref[...] = d(h, w_ref[:, :RW_COLS])
    hy_ref[...] = d(h, w_ref[:, RW_COLS:RW_COLS + HY_COLS])
    gt_ref[...] = _sigmoid(d(h, w_ref[:, RW_COLS + HY_COLS:]))


def _inproj_call(x, mod_l, w_in_bf16, cond_idx):
    t = x.shape[0]
    nb = t // TOKEN_BLOCK
    tok = lambda n: pl.BlockSpec((TOKEN_BLOCK, n), lambda i: (i, 0))
    modspec = lambda j: pl.BlockSpec((None, 1, D_MODEL), lambda i: (cond_idx(i), 0, j))
    return pl.pallas_call(
        _inproj_body,
        out_shape=(jax.ShapeDtypeStruct((t, RW_COLS), F32), jax.ShapeDtypeStruct((t, HY_COLS), F32),
                   jax.ShapeDtypeStruct((t, 2 * D_MODEL), F32)),
        grid=(nb,),
        in_specs=[tok(D_MODEL), modspec(0), modspec(1),
                  pl.BlockSpec((D_MODEL, IN_COLS), lambda i: (0, 0))],
        out_specs=(tok(RW_COLS), tok(HY_COLS), tok(2 * D_MODEL)),
        compiler_params=_cparams(("parallel",), VMEM_LIMIT),
        name="in_proj",
    )(x, mod_l, mod_l, w_in_bf16)


def _neighbours(x, n_ctx_blocks):
    rows = x.shape[0]
    row = lax.broadcasted_iota(jnp.int32, (rows, 1), 0)
    seg_mask = jnp.where(pl.program_id(0) < n_ctx_blocks, rows - 1, GRID_W - 1)
    pos = row & seg_mask
    prev = jnp.where(pos == 0, 0.0, pltpu.roll(x, 1, 0))
    nxt = jnp.where(pos == seg_mask, 0.0, pltpu.roll(x, rows - 1, 0))
    return prev, nxt


def _rwkv_pre_body(rw_ref, mu_ref, kk_s_ref, ka_ref, w0_ref, a0_ref, wl_ref, al_ref, gup_ref, rk_ref, seg_ref,
                   r_ref, kk_ref, v_ref, lw_ref, b_ref, kd_ref, g_ref, bonus_ref, *, n_ctx_blocks):
    x = rw_ref[...]
    prev, nxt = _neighbours(x, n_ctx_blocks)
    cols = x + mu_ref[...] * (0.5 * (prev + nxt) - x)
    w = RW_WIDTH
    r = cols[:, :w]
    k = cols[:, w:2 * w]
    v = cols[:, 2 * w:3 * w]
    wd = cols[:, 3 * w:3 * w + LORA_W]
    ad = cols[:, 3 * w + LORA_W:3 * w + LORA_W + LORA_A]
    gd = cols[:, 3 * w + LORA_W + LORA_A:]
    seg = seg_ref[...]
    kkr = k * kk_s_ref[...]
    ss = _dot_x3(kkr * kkr, seg)
    kk = kkr / jnp.maximum(jnp.sqrt(ss), 1e-12)
    r_ref[...] = r
    kk_ref[...] = kk
    v_ref[...] = v
    g_ref[...] = _bdot(_sigmoid(gd), gup_ref[...])
    tw = jnp.tanh(wd)
    bonus = jnp.zeros_like(r)
    for d in range(2):
        w_logit = w0_ref[d:d + 1, :] + _bdot(tw, wl_ref[d])
        lw_ref[d] = -math.exp(-0.5) * _sigmoid(w_logit)
        a = _sigmoid(a0_ref[d:d + 1, :] + _bdot(ad, al_ref[d]))
        kd = k * (1.0 + (a - 1.0) * ka_ref[...])
        kd_ref[d] = kd
        b_ref[d] = kk * a
        bonus = bonus + _dot_x3(r * kd * rk_ref[d:d + 1, :], seg) * v
    bonus_ref[...] = bonus


def _rwkv_pre_call(rw, p, seg, n_ctx_blocks):
    t = rw.shape[0]
    nb = t // TOKEN_BLOCK
    w = RW_WIDTH
    tok = lambda n: pl.BlockSpec((TOKEN_BLOCK, n), lambda i: (i, 0))
    tok2 = pl.BlockSpec((2, TOKEN_BLOCK, w), lambda i: (0, i, 0))
    full = lambda *s: pl.BlockSpec(s, lambda i: (0,) * len(s))
    o1 = jax.ShapeDtypeStruct((t, w), F32)
    o2 = jax.ShapeDtypeStruct((2, t, w), F32)
    return pl.pallas_call(
        functools.partial(_rwkv_pre_body, n_ctx_blocks=n_ctx_blocks),
        out_shape=(o1, o1, o1, o2, o2, o2, o1, o1),
        grid=(nb,),
        in_specs=[tok(RW_COLS), full(1, RW_COLS), full(1, w), full(1, w), full(2, w), full(2, w),
                  full(2, LORA_W, w), full(2, LORA_A, w), full(LORA_G, w), full(2, w), full(w, w)],
        out_specs=(tok(w), tok(w), tok(w), tok2, tok2, tok2, tok(w), tok(w)),
        compiler_params=_cparams(("parallel",), VMEM_LIMIT),
        name="rwkv_pre",
    )(rw, p['mu_shift'].reshape(1, RW_COLS), p['k_k'].reshape(1, w), p['k_a'].reshape(1, w), p['w0'], p['a0'],
      p['w_lora_up'], p['a_lora_up'], p['g_up'], p['r_k'].reshape(2, w), seg)


def _scan_body(tbl_ref, r_ref, kk_ref, v_ref, lw_ref, b_ref, kd_ref, s0_ref, y_ref, sfin_ref, s_ref, *, nsteps):
    d = pl.program_id(0)
    i = pl.program_id(1)
    base = (d * nsteps + i) * 4
    c = SCAN_CHUNK
    hd = RW_HEAD

    @pl.when(tbl_ref[base + 2] == 1)
    def _():
        s_ref[...] = s0_ref[...]

    rowi = lax.broadcasted_iota(jnp.int32, (c, c), 0)
    coli = lax.broadcasted_iota(jnp.int32, (c, c), 1)
    diff = jnp.where(d == 1, coli - rowi, rowi - coli)
    strict = diff > 0
    incl = diff >= 0
    same_blk = (rowi // INV_BLOCK) == (coli // INV_BLOCK)
    eye = (rowi == coli).astype(F32)

    lw = lw_ref[...]
    hi, mid, lo = _split3(lw)
    tri = incl.astype(BF16)
    dd = functools.partial(jnp.dot, preferred_element_type=F32)
    cum = dd(tri, hi) + (dd(tri, mid) + dd(tri, lo))
    tot = jnp.sum(lw, axis=0, keepdims=True)
    e_cum = jnp.exp(cum)
    e_neg = jnp.exp(-cum)
    e_rem = jnp.exp(tot - cum)
    g_tot = jnp.exp(tot)
    kk = kk_ref[...]
    bb = b_ref[...]
    kd = kd_ref[...]
    alpha = kk * jnp.exp(cum - lw)
    beta = bb * e_neg
    kappa = kd * e_neg
    rho = r_ref[...] * e_cum
    kap_g = kd * e_rem
    bet_g = bb * e_rem
    v = v_ref[...]

    for h in range(RW_HEADS):
        sl = slice(h * hd, (h + 1) * hd)
        a_h, b_h, k_h, r_h, v_h = alpha[:, sl], beta[:, sl], kappa[:, sl], rho[:, sl], v[:, sl]
        ar = jnp.concatenate([a_h, r_h], axis=0)
        gb = _dot_nt(ar, b_h)
        gk = _dot_nt(ar, k_h)
        low = jnp.where(strict, gb[:c], 0.0)
        a_ka = jnp.where(strict, gk[:c], 0.0)
        a_br = jnp.where(incl, gb[c:], 0.0)
        a_kr = jnp.where(incl, gk[c:], 0.0)
        nd = jnp.where(same_blk, -low, 0.0)
        loff = jnp.where(same_blk, 0.0, low)
        x = eye + nd
        n2 = _bdot(nd, nd)
        x = x + _bdot(x, n2)
        n4 = _bdot(n2, n2)
        x = x + _bdot(x, n4)
        n8 = _bdot(n4, n4)
        x = x + _bdot(x, n8)
        m = _bdot(x, loff)
        m2 = _bdot(m, m)
        y1 = x + _bdot(m2, x)
        tinv = y1 - _bdot(m, y1)
        w_h = _bdot(a_ka, v_h)
        rhs = jnp.concatenate([a_h, w_h], axis=1)
        x0 = _bdot(tinv, rhs)
        res = rhs - x0 - _bdot(low, x0)
        xs = x0 + _bdot(tinv, res)
        p_h = xs[:, :hd]
        q_h = xs[:, hd:]
        s_old = s_ref[h]
        uy = _dot_nt(jnp.concatenate([p_h, r_h], axis=0), s_old)
        u_h = uy[:c] + q_h
        y_h = uy[c:] + _bdot(a_kr, v_h) - _bdot(a_br, u_h)
        zv = jnp.concatenate([v_h, u_h], axis=0)
        zk = jnp.concatenate([kap_g[:, sl], -bet_g[:, sl]], axis=0)
        s_ref[h] = s_old * g_tot[:, sl] + _dot_tn(zv, zk)
        y_ref[:, sl] = y_h

    @pl.when(tbl_ref[base + 3] == 1)
    def _():
        sfin_ref[...] = s_ref[...]


def _scan_call(tbl, nsteps, nseq, r, kk, v, lw, b, kd, s0):
    t = r.shape[0]
    w = RW_WIDTH
    c = SCAN_CHUNK

    def rb(d, i, tb):
        return tb[(d * nsteps + i) * 4]

    def sq(d, i, tb):
        return tb[(d * nsteps + i) * 4 + 1]

    shared = pl.BlockSpec((c, w), lambda d, i, tb: (rb(d, i, tb), 0))
    perdir = pl.BlockSpec((None, c, w), lambda d, i, tb: (d, rb(d, i, tb), 0))
    st = pl.BlockSpec((None, None, RW_HEADS, RW_HEAD, RW_HEAD), lambda d, i, tb: (sq(d, i, tb), d, 0, 0, 0))
    gs = pltpu.PrefetchScalarGridSpec(
        num_scalar_prefetch=1,
        grid=(2, nsteps),
        in_specs=[shared, shared, shared, perdir, perdir, perdir, st],
        out_specs=(perdir, st),
        scratch_shapes=[pltpu.VMEM((RW_HEADS, RW_HEAD, RW_HEAD), F32)],
    )
    return pl.pallas_call(
        functools.partial(_scan_body, nsteps=nsteps),
        out_shape=(jax.ShapeDtypeStruct((2, t, w), F32),
                   jax.ShapeDtypeStruct((nseq, 2, RW_HEADS, RW_HEAD, RW_HEAD), F32)),
        grid_spec=gs,
        compiler_params=_cparams(("arbitrary", "arbitrary"), VMEM_LIMIT),
        name="rwkv_scan",
    )(tbl, r, kk, v, lw, b, kd, s0)


def _scan_table(seq_lens):
    c = SCAN_CHUNK
    rows = []
    for d in range(2):
        start = 0
        for s, n in enumerate(seq_lens):
            nc = n // c
            for j in range(nc):
                blk = start + (j if d == 0 else nc - 1 - j)
                rows.append((blk, s, int(j == 0), int(j == nc - 1)))
            start += nc
    return np.asarray(rows, np.int32).reshape(-1), len(rows) // 2


def _hy_pre_body(hy_ref, w_ref, b_ref, u_ref, x1_ref, x2_ref, *, n_ctx_blocks):
    x = hy_ref[...]
    prev, nxt = _neighbours(x, n_ctx_blocks)
    cols = prev * w_ref[0:1, :] + x * w_ref[1:2, :] + nxt * w_ref[2:3, :] + b_ref[...]
    w = HY_WIDTH
    u_ref[...] = cols[:, :w]
    x1_ref[...] = cols[:, w:2 * w]
    x2_ref[...] = cols[:, 2 * w:]


def _hy_pre_call(hy, conv_w, conv_b, n_ctx_blocks):
    t = hy.shape[0]
    tok = lambda n: pl.BlockSpec((TOKEN_BLOCK, n), lambda i: (i, 0))
    o = jax.ShapeDtypeStruct((t, HY_WIDTH), F32)
    return pl.pallas_call(
        functools.partial(_hy_pre_body, n_ctx_blocks=n_ctx_blocks),
        out_shape=(o, o, o),
        grid=(t // TOKEN_BLOCK,),
        in_specs=[tok(HY_COLS), pl.BlockSpec((3, HY_COLS), lambda i: (0, 0)),
                  pl.BlockSpec((1, HY_COLS), lambda i: (0, 0))],
        out_specs=(tok(HY_WIDTH),) * 3,
        compiler_params=_cparams(("parallel",)),
        name="hyena_pre",
    )(hy, conv_w, conv_b.reshape(1, HY_COLS))


def _dot_f32(a, b):
    a_hi, a_lo = _split2(a)
    b_hi, b_lo = _split2(b)
    d = functools.partial(jnp.dot, preferred_element_type=F32)
    return d(a_hi, b_hi) + (d(a_hi, b_lo) + d(a_lo, b_hi))


def _hy_filter_body(bands_ref, w1t_ref, w1c_ref, w1s_ref, b1_ref, w2_ref, b2_ref, fr_ref, w3_ref, dl_ref, h_ref,
                    *, n, tile):
    pos = (lax.broadcasted_iota(jnp.int32, (tile, LANES), 0) + pl.program_id(0) * tile).astype(F32)
    tcol = pos / n
    ang = (2.0 * math.pi / n) * pos * bands_ref[...]
    pre1 = (tcol[:, :HY_FFN] * w1t_ref[...] + _dot_f32(jnp.cos(ang), w1c_ref[...])
            + _dot_f32(jnp.sin(ang), w1s_ref[...]) + b1_ref[...])
    fr = fr_ref[...]
    h1 = jnp.sin(fr * pre1)
    h2 = jnp.sin(fr * (_dot_f32(h1, w2_ref[...]) + b2_ref[...]))
    h = _dot_f32(h2, w3_ref[...])
    for j in range(HY_ORDER * 2):
        for q in range(HY_WIDTH // LANES):
            lo = j * HY_WIDTH + q * LANES
            win = jnp.exp(-tcol * dl_ref[:, q * LANES:(q + 1) * LANES])
            h_ref[:, lo:lo + LANES] = h[:, lo:lo + LANES] * win


def _hy_filter_call(n, p):
    bands = jnp.zeros((1, LANES), F32).at[0, :HY_BANDS].set(
        jnp.linspace(1e-4, HY_BANDS - 1, HY_BANDS, dtype=F32))
    max_decay = math.log(HY_DECAY_TARGET) / HY_DECAY_SHORT_PCT
    min_decay = math.log(HY_DECAY_TARGET) / HY_DECAY_LONG_PCT
    deltas = jnp.abs(jnp.linspace(min_decay, max_decay, HY_WIDTH, dtype=F32)).reshape(1, HY_WIDTH)
    w1 = p['hy_f_w1']
    w1t = w1[0:1]
    w1c = jnp.zeros((LANES, HY_FFN), F32).at[:HY_BANDS].set(w1[1:1 + HY_BANDS])
    w1s = jnp.zeros((LANES, HY_FFN), F32).at[:HY_BANDS].set(w1[1 + HY_BANDS:HY_EMB])
    nout = HY_ORDER * 2 * HY_WIDTH
    tile = min(n, 512)
    full = lambda *s: pl.BlockSpec(s, lambda i: (0,) * len(s))
    return pl.pallas_call(
        functools.partial(_hy_filter_body, n=n, tile=tile),
        out_shape=jax.ShapeDtypeStruct((n, nout), F32),
        grid=(n // tile,),
        in_specs=[full(1, LANES), full(1, HY_FFN), full(LANES, HY_FFN), full(LANES, HY_FFN), full(1, HY_FFN),
                  full(HY_FFN, HY_FFN), full(1, HY_FFN), full(1, HY_FFN), full(HY_FFN, nout), full(1, HY_WIDTH)],
        out_specs=pl.BlockSpec((tile, nout), lambda i: (i, 0)),
        compiler_params=_cparams(("parallel",), VMEM_LIMIT),
        name="hyena_filter",
    )(bands, w1t, w1c, w1s, p['hy_f_b1'].reshape(1, HY_FFN), p['hy_f_w2'], p['hy_f_b2'].reshape(1, HY_FFN),
      p['hy_f_freq'].reshape(1, HY_FFN), p['hy_f_w3'], deltas)


def _dft_body(*refs, pre, post, two, half_in, half_mid):
    it = iter(refs)
    x = next(it)[...]
    if pre:
        m1 = next(it)[...]
        m2 = next(it)[...]
        x = x * m1 + pltpu.roll(x, half_in, 1) * m2
    g1h = next(it)[...]
    g1l = next(it)[...]
    y = _dot3(x, g1h, g1l)
    if post:
        m1 = next(it)[...]
        m2 = next(it)[...]
        y = y * m1 + pltpu.roll(y, half_mid, 1) * m2
    if two:
        g2h = next(it)[...]
        g2l = next(it)[...]
        y = _dot3(y, g2h, g2l)
    o_ref = next(it)
    o_ref[...] = y


def _dft_call(x, g1, tile, pre=None, post=None, g2=None, name="hyena_dft"):
    bsz, rows, k = x.shape
    args = [x]
    specs = [pl.BlockSpec((None, tile, k), lambda b, j: (b, j, 0))]

    def add_mul(mm):
        for a in mm:
            nblk = a.shape[0] // tile
            specs.append(pl.BlockSpec((tile, a.shape[1]), lambda b, j, nblk=nblk: (j % nblk, 0)))
            args.append(a)

    def add_mat(g):
        for a in g:
            specs.append(pl.BlockSpec(a.shape, lambda b, j: (0, 0)))
            args.append(a)

    if pre is not None:
        add_mul(pre)
    add_mat(g1)
    if post is not None:
        add_mul(post)
    if g2 is not None:
        add_mat(g2)
    mid = g1[0].shape[1]
    nout = g2[0].shape[1] if g2 is not None else mid
    body = functools.partial(_dft_body, pre=pre is not None, post=post is not None, two=g2 is not None,
                             half_in=k // 2, half_mid=mid // 2)
    return pl.pallas_call(
        body,
        out_shape=jax.ShapeDtypeStruct((bsz, rows, nout), F32),
        grid=(bsz, rows // tile),
        in_specs=specs,
        out_specs=pl.BlockSpec((None, tile, nout), lambda b, j: (b, j, 0)),
        compiler_params=_cparams(("parallel", "parallel"), VMEM_LIMIT),
        name=name,
    )(*args)


def _hilo(a):
    a = np.asarray(a, np.float64)
    hi = jnp.asarray(a, F32).astype(BF16)
    lo = (jnp.asarray(a, F32) - hi.astype(F32)).astype(BF16)
    return hi, lo


def _cplx_block(f):
    return np.block([[f.real, f.imag], [-f.imag, f.real]])


@functools.lru_cache(maxsize=None)
def _dft_plan(n):
    big = 2 * n
    if big <= 512:
        t = np.arange(n)[:, None]
        k = np.arange(big)[None, :]
        fwd = np.exp(-2j * np.pi * t * k / big)
        fwd_full = np.exp(-2j * np.pi * np.arange(big)[:, None] * k / big)
        inv = np.exp(2j * np.pi * np.arange(big)[:, None] * np.arange(n)[None, :] / big) / big
        return dict(stages=1,
                    fwd=np.concatenate([fwd.real, fwd.imag], axis=1),
                    fwd_full=np.concatenate([fwd_full.real, fwd_full.imag], axis=1),
                    inv=np.concatenate([inv.real, -inv.imag], axis=0))
    n1, n2 = 128, big // 128
    t1 = np.arange(n1)[:, None]
    k1 = np.arange(n1)[None, :]
    fa = np.exp(-2j * np.pi * t1 * k1 / n1)
    fa_ri = np.concatenate([fa.real, fa.imag], axis=1)
    t2 = np.arange(n2)[:, None]
    tw = np.exp(-2j * np.pi * t2 * k1 / big)
    k2 = np.arange(n2)[None, :]
    fb = np.exp(-2j * np.pi * t2 * k2 / n2)
    fb_inv = np.exp(2j * np.pi * np.arange(n2)[:, None] * np.arange(n2)[None, :] / n2) / big
    fa_inv = np.exp(2j * np.pi * np.arange(n1)[:, None] * np.arange(n1 // 2)[None, :] / n1)
    return dict(stages=2, n1=n1, n2=n2,
                fa_half=fa_ri[:n1 // 2], fa_full=fa_ri,
                tw_m1=np.concatenate([tw.real, tw.real], axis=1),
                tw_m2=np.concatenate([-tw.imag, tw.imag], axis=1),
                twc_m2=np.concatenate([tw.imag, -tw.imag], axis=1),
                fb=_cplx_block(fb), fb_inv=_cplx_block(fb_inv),
                fa_inv=np.concatenate([fa_inv.real, -fa_inv.imag], axis=0))


def _long_conv_setup(n, h):
    c = HY_WIDTH
    plan = _dft_plan(n)
    hh = h.reshape(n, HY_ORDER, 2, c)
    f = jnp.concatenate([hh[:, :, 0], jnp.zeros((1, HY_ORDER, c), F32), hh[:0:-1, :, 1]], axis=0)
    if plan['stages'] == 1:
        xf = jnp.transpose(f, (1, 2, 0))
        spec = _dft_call(xf, _hilo(plan['fwd_full']), tile=min(c, 256), name="hyena_filter_dft")
        fr, fi = spec[..., :2 * n], spec[..., 2 * n:]
        return [(jnp.concatenate([fr[o], fr[o]], axis=1), jnp.concatenate([-fi[o], fi[o]], axis=1))
                for o in range(HY_ORDER)]
    n1, n2 = plan['n1'], plan['n2']
    xf = jnp.transpose(f.reshape(n1, n2, HY_ORDER, c), (2, 3, 1, 0)).reshape(HY_ORDER, c * n2, n1)
    tile = 8 * n2
    tw = (jnp.asarray(np.tile(plan['tw_m1'], (8, 1)), F32), jnp.asarray(np.tile(plan['tw_m2'], (8, 1)), F32))
    a = _dft_call(xf, _hilo(plan['fa_full']), tile=tile, post=tw, name="hyena_filter_dft_a")
    a = jnp.transpose(a.reshape(HY_ORDER, c, n2, 2, n1), (0, 1, 4, 3, 2)).reshape(HY_ORDER, c * n1, 2 * n2)
    spec = _dft_call(a, _hilo(plan['fb']), tile=512, name="hyena_filter_dft_b")
    fr, fi = spec[..., :n2], spec[..., n2:]
    return [(jnp.concatenate([fr[o], fr[o]], axis=1), jnp.concatenate([-fi[o], fi[o]], axis=1))
            for o in range(HY_ORDER)]


def _long_conv(z, hmul):
    bsz, n, c = z.shape
    plan = _dft_plan(n)
    if plan['stages'] == 1:
        x = jnp.transpose(z, (0, 2, 1))
        y = _dft_call(x, _hilo(plan['fwd']), tile=min(c, 256), post=hmul, g2=_hilo(plan['inv']),
                      name="hyena_conv_short")
        return jnp.transpose(y, (0, 2, 1))
    n1, n2 = plan['n1'], plan['n2']
    h1 = n1 // 2
    x = jnp.transpose(z.reshape(bsz, h1, n2, c), (0, 3, 2, 1)).reshape(bsz, c * n2, h1)
    tile = 8 * n2
    tw = (jnp.asarray(np.tile(plan['tw_m1'], (8, 1)), F32), jnp.asarray(np.tile(plan['tw_m2'], (8, 1)), F32))
    a = _dft_call(x, _hilo(plan['fa_half']), tile=tile, post=tw, name="hyena_dft_a")
    a = jnp.transpose(a.reshape(bsz, c, n2, 2, n1), (0, 1, 4, 3, 2)).reshape(bsz, c * n1, 2 * n2)
    m = _dft_call(a, _hilo(plan['fb']), tile=512, post=hmul, g2=_hilo(plan['fb_inv']), name="hyena_dft_mid")
    m = jnp.transpose(m.reshape(bsz, c, n1, 2, n2), (0, 1, 4, 3, 2)).reshape(bsz, c * n2, 2 * n1)
    twc = (jnp.asarray(np.tile(plan['tw_m1'], (8, 1)), F32), jnp.asarray(np.tile(plan['twc_m2'], (8, 1)), F32))
    y = _dft_call(m, _hilo(plan['fa_inv']), tile=tile, pre=twc, name="hyena_dft_a_inv")
    return jnp.transpose(y.reshape(bsz, c, n2, h1), (0, 3, 2, 1)).reshape(bsz, n, c)


def _hy_gate_body(x_ref, y_ref, z_ref, s_ref, o_ref):
    o_ref[...] = x_ref[...] * (y_ref[...] + z_ref[...] * s_ref[...])


def _hy_gate_call(x, y, z, skip):
    t, c = x.shape
    tok = pl.BlockSpec((512, c), lambda i: (i, 0))
    return pl.pallas_call(
        _hy_gate_body,
        out_shape=jax.ShapeDtypeStruct((t, c), F32),
        grid=(t // 512,),
        in_specs=[tok, tok, tok, pl.BlockSpec((1, c), lambda i: (0, 0))],
        out_specs=tok,
        compiler_params=_cparams(("parallel",)),
        name="hyena_gate",
    )(x, y, z, skip.reshape(1, c))


def _hyena_branch(hy, p, groups, n_ctx_blocks):
    u, x1, x2 = _hy_pre_call(hy, p['hy_conv_w'], p['hy_conv_b'], n_ctx_blocks)
    outs = []
    for start, bsz, n in groups:
        rows = bsz * n
        hmul = _long_conv_setup(n, _hy_filter_call(n, p))
        ug, x1g, x2g = (a[start:start + rows] for a in (u, x1, x2))
        y = _long_conv(ug.reshape(bsz, n, HY_WIDTH), hmul[0]).reshape(rows, HY_WIDTH)
        z = _hy_gate_call(x1g, y, ug, p['hy_skip'][0])
        y = _long_conv(z.reshape(bsz, n, HY_WIDTH), hmul[1]).reshape(rows, HY_WIDTH)
        outs.append(_hy_gate_call(x2g, y, z, p['hy_skip'][1]))
    return jnp.concatenate(outs, axis=0)


def _merge_body(x_ref, y_ref, bonus_ref, g_ref, yb_ref, gt_ref, g1_ref, sh2_ref, sc2_ref, seg_ref, gng_ref,
                gnb_ref, wpa_ref, wpb_ref, wout_ref, lng_ref, lnb_ref, rwh_ref, rwl_ref, rb_ref,
                x1_ref, h2_ref, ti_ref, tg_ref, *, alpha):
    seg = seg_ref[...]
    y = y_ref[0] + y_ref[1]
    inv = 1.0 / RW_HEAD
    mu = _dot_x3(y, seg) * inv
    yc = y - mu
    var = _dot_x3(yc * yc, seg) * inv
    yn = yc * lax.rsqrt(var + GN_EPS) * gng_ref[...] + gnb_ref[...]
    y_a = (yn + bonus_ref[...]) * g_ref[...]
    gt = gt_ref[...]
    merged = gt[:, :D_MODEL] * _bdot(y_a, wpa_ref[...]) + gt[:, D_MODEL:] * _bdot(yb_ref[...], wpb_ref[...])
    mix = _bdot(merged, wout_ref[...])
    x1 = _layer_norm(alpha * x_ref[...] + g1_ref[...] * mix, lng_ref[...], lnb_ref[...])
    x1_ref[...] = x1
    h2 = x1 * (1.0 + sc2_ref[...]) + sh2_ref[...]
    h2_ref[...] = h2
    h_hi, h_lo = _split2(h2)
    d = functools.partial(jnp.dot, preferred_element_type=F32)
    logits = d(h_hi, rwh_ref[...]) + (d(h_hi, rwl_ref[...]) + d(h_lo, rwh_ref[...])) + rb_ref[...]
    lane = lax.broadcasted_iota(jnp.int32, logits.shape, 1)
    neg = jnp.float32(-jnp.inf)
    cur = jnp.where(lane < N_EXPERTS, logits, neg)
    top_i = jnp.zeros(logits.shape, jnp.int32)
    top_e = jnp.zeros(logits.shape, F32)
    den = jnp.zeros((logits.shape[0], 1), F32)
    v0 = None
    for j in range(TOP_K):
        mx = jnp.max(cur, axis=-1, keepdims=True)
        idx = jnp.min(jnp.where(cur == mx, lane, LANES), axis=-1, keepdims=True)
        if j == 0:
            v0 = mx
        e = jnp.exp(mx - v0)
        den = den + e
        top_i = jnp.where(lane == j, idx, top_i)
        top_e = jnp.where(lane == j, e, top_e)
        cur = jnp.where(lane == idx, neg, cur)
    ti_ref[...] = top_i
    tg_ref[...] = top_e / den


def _merge_call(x, y, bonus, g, yb, gates, mod_l, p, seg, cond_idx, alpha):
    t = x.shape[0]
    w = RW_WIDTH
    tok = lambda n: pl.BlockSpec((TOKEN_BLOCK, n), lambda i: (i, 0))
    modspec = lambda j: pl.BlockSpec((None, 1, D_MODEL), lambda i: (cond_idx(i), 0, j))
    full = lambda *s: pl.BlockSpec(s, lambda i: (0,) * len(s))
    rw_pad = jnp.zeros((D_MODEL, LANES), F32).at[:, :N_EXPERTS].set(p['router_w'])
    rw_hi = rw_pad.astype(BF16)
    rw_lo = (rw_pad - rw_hi.astype(F32)).astype(BF16)
    rb = jnp.zeros((1, LANES), F32).at[0, :N_EXPERTS].set(p['router_b'])
    o = jax.ShapeDtypeStruct((t, D_MODEL), F32)
    return pl.pallas_call(
        functools.partial(_merge_body, alpha=alpha),
        out_shape=(o, o, jax.ShapeDtypeStruct((t, LANES), jnp.int32), jax.ShapeDtypeStruct((t, LANES), F32)),
        grid=(t // TOKEN_BLOCK,),
        in_specs=[tok(D_MODEL), pl.BlockSpec((2, TOKEN_BLOCK, w), lambda i: (0, i, 0)), tok(w), tok(w), tok(w),
                  tok(2 * D_MODEL), modspec(2), modspec(3), modspec(4), full(w, w), full(1, w), full(1, w),
                  full(w, D_MODEL), full(w, D_MODEL), full(D_MODEL, D_MODEL), full(1, D_MODEL), full(1, D_MODEL),
                  full(D_MODEL, LANES), full(D_MODEL, LANES), full(1, LANES)],
        out_specs=(tok(D_MODEL), tok(D_MODEL), tok(LANES), tok(LANES)),
        compiler_params=_cparams(("parallel",), VMEM_LIMIT),
        name="merge_ln_router",
    )(x, y, bonus, g, yb, gates, mod_l, mod_l, mod_l, seg, p['gn_g'].reshape(1, w), p['gn_b'].reshape(1, w),
      p['w_pa'].astype(BF16), p['w_pb'].astype(BF16), p['w_out'].astype(BF16),
      p['ln1_g'].reshape(1, D_MODEL), p['ln1_b'].reshape(1, D_MODEL), rw_hi, rw_lo, rb)


def _moe_body(blk_e_ref, blk_on_ref, tok_ref, dst_ref, h_hbm, wup_ref, bup_ref, wdn_ref, bdn_ref, y_hbm,
              xbuf, ybuf, sem_in, sem_out):
    i = pl.program_id(0)

    def gather(r):
        return pltpu.make_async_copy(h_hbm.at[pl.ds(tok_ref[0, r], 1), :], xbuf.at[pl.ds(r, 1), :], sem_in)

    def scatter(r):
        return pltpu.make_async_copy(ybuf.at[pl.ds(r, 1), :], y_hbm.at[pl.ds(dst_ref[0, r], 1), :], sem_out)

    @pl.when(blk_on_ref[i] > 0)
    def _():
        def issue(r, c):
            gather(r).start()
            return c

        def wait(r, c):
            gather(r).wait()
            return c

        lax.fori_loop(0, MOE_ROWS, issue, 0)
        lax.fori_loop(0, MOE_ROWS, wait, 0)
        x = xbuf[...].astype(BF16)
        up = jnp.dot(x, wup_ref[...], preferred_element_type=F32) + bup_ref[...]
        glu = jnp.minimum(up[:, :D_FF], SWIGLU_LIMIT)
        lin = jnp.clip(up[:, D_FF:], -SWIGLU_LIMIT, SWIGLU_LIMIT)
        act = glu * _sigmoid(SWIGLU_ALPHA * glu) * (lin + 1.0)
        ybuf[...] = jnp.dot(act.astype(BF16), wdn_ref[...], preferred_element_type=F32) + bdn_ref[...]

        def out_issue(r, c):
            @pl.when(dst_ref[0, r] >= 0)
            def _():
                scatter(r).start()
            return c

        def out_wait(r, c):
            @pl.when(dst_ref[0, r] >= 0)
            def _():
                scatter(r).wait()
            return c

        lax.fori_loop(0, MOE_ROWS, out_issue, 0)
        lax.fori_loop(0, MOE_ROWS, out_wait, 0)


def _moe_call(h2, top_i, wup, bup, wdn, bdn):
    t = h2.shape[0]
    m = t * TOP_K
    e = N_EXPERTS
    blk = MOE_ROWS
    flat_e = top_i.reshape(-1)
    order = jnp.argsort(flat_e, stable=True).astype(jnp.int32)
    sizes = jnp.bincount(flat_e, length=e).astype(jnp.int32)
    padded = (sizes + blk - 1) // blk * blk
    pad_end = jnp.cumsum(padded)
    pad_start = pad_end - padded
    grp_start = jnp.cumsum(sizes) - sizes
    n_blocks = -(-(m + e * (blk - 1)) // blk)
    blk_e = jnp.minimum(jnp.searchsorted(pad_end, jnp.arange(n_blocks, dtype=jnp.int32) * blk, side='right'),
                        e - 1).astype(jnp.int32)
    pidx = jnp.arange(n_blocks * blk, dtype=jnp.int32)
    e_p = jnp.repeat(blk_e, blk)
    idx = pidx - pad_start[e_p]
    valid = idx < sizes[e_p]
    assign = order[jnp.clip(grp_start[e_p] + idx, 0, m - 1)]
    tok_row = jnp.where(valid, assign // TOP_K, 0).astype(jnp.int32)
    dst_row = jnp.where(valid, (assign % TOP_K) * t + assign // TOP_K, -1).astype(jnp.int32)
    blk_on = jnp.any(valid.reshape(n_blocks, blk), axis=1).astype(jnp.int32)

    smem_blk = pl.BlockSpec((None, 1, blk), lambda i, be, bo: (i, 0, 0), memory_space=pltpu.SMEM)
    gs = pltpu.PrefetchScalarGridSpec(
        num_scalar_prefetch=2,
        grid=(n_blocks,),
        in_specs=[smem_blk, smem_blk,
                  pl.BlockSpec(memory_space=pl.ANY),
                  pl.BlockSpec((None, D_MODEL, 2 * D_FF), lambda i, be, bo: (be[i], 0, 0)),
                  pl.BlockSpec((None, 1, 2 * D_FF), lambda i, be, bo: (be[i], 0, 0)),
                  pl.BlockSpec((None, D_FF, D_MODEL), lambda i, be, bo: (be[i], 0, 0)),
                  pl.BlockSpec((None, 1, D_MODEL), lambda i, be, bo: (be[i], 0, 0))],
        out_specs=pl.BlockSpec(memory_space=pl.ANY),
        scratch_shapes=[pltpu.VMEM((blk, D_MODEL), F32), pltpu.VMEM((blk, D_MODEL), F32),
                        pltpu.SemaphoreType.DMA(()), pltpu.SemaphoreType.DMA(())],
    )
    return pl.pallas_call(
        _moe_body,
        out_shape=jax.ShapeDtypeStruct((m, D_MODEL), F32),
        grid_spec=gs,
        compiler_params=_cparams(("arbitrary",), VMEM_LIMIT),
        name="moe_experts",
    )(blk_e, blk_on, tok_row.reshape(n_blocks, 1, blk), dst_row.reshape(n_blocks, 1, blk), h2, wup,
      bup.reshape(e, 1, 2 * D_FF), wdn, bdn.reshape(e, 1, D_MODEL))


def _combine_body(x1_ref, y_ref, tg_ref, g2_ref, lng_ref, lnb_ref, o_ref, *, alpha):
    tg = tg_ref[...]
    moe = tg[:, 0:1] * y_ref[0]
    for j in range(1, TOP_K):
        moe = moe + tg[:, j:j + 1] * y_ref[j]
    o_ref[...] = _layer_norm(alpha * x1_ref[...] + g2_ref[...] * moe, lng_ref[...], lnb_ref[...])


def _combine_call(x1, yexp, tg, mod_l, p, cond_idx, alpha):
    t = x1.shape[0]
    tok = lambda n: pl.BlockSpec((TOKEN_BLOCK, n), lambda i: (i, 0))
    full = lambda *s: pl.BlockSpec(s, lambda i: (0,) * len(s))
    return pl.pallas_call(
        functools.partial(_combine_body, alpha=alpha),
        out_shape=jax.ShapeDtypeStruct((t, D_MODEL), F32),
        grid=(t // TOKEN_BLOCK,),
        in_specs=[tok(D_MODEL), pl.BlockSpec((TOP_K, TOKEN_BLOCK, D_MODEL), lambda i: (0, i, 0)), tok(LANES),
                  pl.BlockSpec((None, 1, D_MODEL), lambda i: (cond_idx(i), 0, 5)),
                  full(1, D_MODEL), full(1, D_MODEL)],
        out_specs=tok(D_MODEL),
        compiler_params=_cparams(("parallel",), VMEM_LIMIT),
        name="moe_combine_ln",
    )(x1, yexp.reshape(TOP_K, t, D_MODEL), tg, mod_l, p['ln2_g'].reshape(1, D_MODEL), p['ln2_b'].reshape(1, D_MODEL))


def kernel(x_prompt, x_sample, c, state_rwkv, c_ctx, w_mod, b_mod, w_in, mu_shift, w0, w_lora_up, a0, a_lora_up, g_up, k_k, k_a, r_k, gn_g, gn_b, hy_conv_w, hy_conv_b, hy_f_w1, hy_f_b1, hy_f_w2, hy_f_b2, hy_f_freq, hy_f_w3, hy_skip, w_pa, w_pb, w_out, ln1_g, ln1_b, ln2_g, ln2_b, router_w, router_b, ex_w_up, ex_b_up, ex_w_down, ex_b_down):
    bsz, seq, dm = x_prompt.shape
    dbsz, dseq, _ = x_sample.shape
    depth = w_mod.shape[0]
    assert dm == D_MODEL and seq == TOKEN_BLOCK and dseq % TOKEN_BLOCK == 0 and TOKEN_BLOCK % GRID_W == 0
    assert 1 + dbsz <= 8
    alpha = (2 * depth) ** 0.25
    t_ctx = bsz * seq
    n_ctx_blocks = t_ctx // TOKEN_BLOCK
    lat_blocks = dseq // TOKEN_BLOCK

    def cond_idx(i):
        return jnp.where(i < n_ctx_blocks, 0, 1 + (i - n_ctx_blocks) // lat_blocks)

    x = jnp.concatenate([x_prompt.reshape(t_ctx, dm), x_sample.reshape(dbsz * dseq, dm)], axis=0)
    cond8 = jnp.zeros((8, dm), F32).at[0].set(c_ctx).at[1:1 + dbsz].set(c)
    mod = _mod_call(cond8, w_mod, b_mod)

    hd = RW_HEAD
    seg = (np.arange(RW_WIDTH)[:, None] // hd == np.arange(RW_WIDTH)[None, :] // hd)
    seg = jnp.asarray(seg, BF16)
    seq_lens = [seq] * bsz + [dseq] * dbsz
    tbl_np, nsteps = _scan_table(seq_lens)
    tbl = jnp.asarray(tbl_np)
    groups = [(0, bsz, seq), (t_ctx, dbsz, dseq)]

    new_states = []
    for l in range(depth):
        p = dict(mu_shift=mu_shift[l], w0=w0[l], w_lora_up=w_lora_up[l], a0=a0[l], a_lora_up=a_lora_up[l],
                 g_up=g_up[l], k_k=k_k[l], k_a=k_a[l], r_k=r_k[l], gn_g=gn_g[l], gn_b=gn_b[l],
                 hy_conv_w=hy_conv_w[l], hy_conv_b=hy_conv_b[l], hy_f_w1=hy_f_w1[l], hy_f_b1=hy_f_b1[l],
                 hy_f_w2=hy_f_w2[l], hy_f_b2=hy_f_b2[l], hy_f_freq=hy_f_freq[l], hy_f_w3=hy_f_w3[l],
                 hy_skip=hy_skip[l], w_pa=w_pa[l], w_pb=w_pb[l], w_out=w_out[l], ln1_g=ln1_g[l], ln1_b=ln1_b[l],
                 ln2_g=ln2_g[l], ln2_b=ln2_b[l], router_w=router_w[l], router_b=router_b[l])
        mod_l = mod[l].reshape(8, 1, 6 * dm)
        rw, hy, gates = _inproj_call(x, mod_l, w_in[l].astype(BF16), cond_idx)
        r, kk, v, lw, b, kd, g, bonus = _rwkv_pre_call(rw, p, seg, n_ctx_blocks)
        s0 = jnp.concatenate([jnp.zeros((bsz, 2, RW_HEADS, hd, hd), F32), state_rwkv[:, l].astype(F32)], axis=0)
        y, sfin = _scan_call(tbl, nsteps, bsz + dbsz, r, kk, v, lw, b, kd, s0)
        new_states.append(sfin[:bsz].astype(x_prompt.dtype))
        yb = _hyena_branch(hy, p, groups, n_ctx_blocks)
        x1, h2, top_i, top_g = _merge_call(x, y, bonus, g, yb, gates, mod_l, p, seg, cond_idx, alpha)
        wup = jnp.concatenate([ex_w_up[l][..., 0::2], ex_w_up[l][..., 1::2]], axis=-1).astype(BF16)
        bup = jnp.concatenate([ex_b_up[l][..., 0::2], ex_b_up[l][..., 1::2]], axis=-1)
        yexp = _moe_call(h2, top_i[:, :TOP_K], wup, bup, ex_w_down[l].astype(BF16), ex_b_down[l])
        x = _combine_call(x1, yexp, top_g, mod_l, p, cond_idx, alpha)

    y_p = x[:t_ctx].reshape(bsz, seq, dm)
    y_s = x[t_ctx:].reshape(dbsz, dseq, dm)
    return (y_p, y_s, jnp.stack(new_states, axis=1))
```

```python
import functools
import math

import numpy as np
import jax
import jax.numpy as jnp
from jax import lax
from jax.experimental import pallas as pl
from jax.experimental.pallas import tpu as pltpu

F32 = jnp.float32
BF16 = jnp.bfloat16

D_MODEL = 1024
GRID_W = 64
RW_WIDTH = 512
RW_HEAD = 64
RW_HEADS = RW_WIDTH // RW_HEAD
LORA_W = 64
LORA_A = 64
LORA_G = 128
GN_EPS = 64e-5
HY_WIDTH = 512
HY_ORDER = 2
HY_BANDS = 16
HY_EMB = 2 * HY_BANDS + 1
HY_FFN = 64
HY_DECAY_TARGET = 1e-2
HY_DECAY_SHORT_PCT = 0.3
HY_DECAY_LONG_PCT = 1.5
RW_COLS = 3 * RW_WIDTH + LORA_W + LORA_A + LORA_G
HY_COLS = (HY_ORDER + 1) * HY_WIDTH
IN_COLS = RW_COLS + HY_COLS + 2 * D_MODEL
N_EXPERTS = 32
TOP_K = 4
D_FF = D_MODEL
SWIGLU_ALPHA = 1.702
SWIGLU_LIMIT = 7.0
LN_EPS = 1e-5

TOKEN_BLOCK = 256
SCAN_CHUNK = 64
INV_BLOCK = 16
MOE_ROWS = 256
LANES = 128
VMEM_LIMIT = 56 * 1024 * 1024


def _cparams(sem, vmem=None):
    return pltpu.CompilerParams(dimension_semantics=sem, vmem_limit_bytes=vmem)


def _bdot(a, b):
    return jnp.dot(a.astype(BF16), b.astype(BF16), preferred_element_type=F32)


def _bdg(a, b, ca, cb):
    return lax.dot_general(a.astype(BF16), b.astype(BF16), (((ca,), (cb,)), ((0,), (0,))), preferred_element_type=F32)


def _bmm(a, b):
    return _bdg(a, b, 2, 1)


def _bmm_nt(a, b):
    return _bdg(a, b, 2, 2)


def _bmm_tn(a, b):
    return _bdg(a, b, 1, 1)


def _split2(x):
    hi = x.astype(BF16)
    lo = (x - hi.astype(F32)).astype(BF16)
    return hi, lo


def _split3(x):
    hi = x.astype(BF16)
    r1 = x - hi.astype(F32)
    mid = r1.astype(BF16)
    lo = (r1 - mid.astype(F32)).astype(BF16)
    return hi, mid, lo


def _dot_x3(x, g_bf16):
    hi, mid, lo = _split3(x)
    d = functools.partial(jnp.dot, preferred_element_type=F32)
    return d(hi, g_bf16) + (d(mid, g_bf16) + d(lo, g_bf16))


def _dot3(x, g_hi, g_lo):
    x_hi, x_lo = _split2(x)
    d = functools.partial(jnp.dot, preferred_element_type=F32)
    return d(x_hi, g_hi) + (d(x_hi, g_lo) + d(x_lo, g_hi))


def _sigmoid(x):
    return 1.0 / (1.0 + jnp.exp(-x))


def _layer_norm(x, g, b):
    mu = jnp.mean(x, axis=-1, keepdims=True)
    xc = x - mu
    var = jnp.mean(xc * xc, axis=-1, keepdims=True)
    return xc * lax.rsqrt(var + LN_EPS) * g + b


def _mod_body(c_ref, w_ref, b_ref, o_ref):
    c = c_ref[...]
    o_ref[...] = _bdot(c * _sigmoid(c), w_ref[...]) + b_ref[...]


def _mod_call(cond8, w_mod, b_mod):
    depth = w_mod.shape[0]
    tn = 1536
    return pl.pallas_call(
        _mod_body,
        out_shape=jax.ShapeDtypeStruct((depth, 8, 6 * D_MODEL), F32),
        grid=(depth, 6 * D_MODEL // tn),
        in_specs=[pl.BlockSpec((8, D_MODEL), lambda l, j: (0, 0)),
                  pl.BlockSpec((None, D_MODEL, tn), lambda l, j: (l, 0, j)),
                  pl.BlockSpec((None, 1, tn), lambda l, j: (l, 0, j))],
        out_specs=pl.BlockSpec((None, 8, tn), lambda l, j: (l, 0, j)),
        compiler_params=_cparams(("parallel", "parallel"), VMEM_LIMIT),
        name="adaln_mod",
    )(cond8, w_mod, b_mod.reshape(depth, 1, 6 * D_MODEL))


def _inproj_body(x_ref, sh_ref, sc_ref, w_ref, rw_ref, hy_ref, gt_ref):
    h = (x_ref[...] * (1.0 + sc_ref[...]) + sh_ref[...]).astype(BF16)
    d = functools.partial(jnp.dot, preferred_element_type=F32)
    rw_ref[...] = d(h, w_ref[:, :RW_COLS])
    hy_ref[...] = d(h, w_ref[:, RW_COLS:RW_COLS + HY_COLS])
    gt_ref[...] = _sigmoid(d(h, w_ref[:, RW_COLS + HY_COLS:]))


def _inproj_call(x, mod_l, w_in_bf16, cond_idx):
    t = x.shape[0]
    nb = t // TOKEN_BLOCK
    tok = lambda n: pl.BlockSpec((TOKEN_BLOCK, n), lambda i: (i, 0))
    modspec = lambda j: pl.BlockSpec((None, 1, D_MODEL), lambda i: (cond_idx(i), 0, j))
    return pl.pallas_call(
        _inproj_body,
        out_shape=(jax.ShapeDtypeStruct((t, RW_COLS), F32), jax.ShapeDtypeStruct((t, HY_COLS), F32),
                   jax.ShapeDtypeStruct((t, 2 * D_MODEL), F32)),
        grid=(nb,),
        in_specs=[tok(D_MODEL), modspec(0), modspec(1),
                  pl.BlockSpec((D_MODEL, IN_COLS), lambda i: (0, 0))],
        out_specs=(tok(RW_COLS), tok(HY_COLS), tok(2 * D_MODEL)),
        compiler_params=_cparams(("parallel",), VMEM_LIMIT),
        name="in_proj",
    )(x, mod_l, mod_l, w_in_bf16)


def _neighbours(x, n_ctx_blocks):
    rows = x.shape[0]
    row = lax.broadcasted_iota(jnp.int32, (rows, 1), 0)
    seg_mask = jnp.where(pl.program_id(0) < n_ctx_blocks, rows - 1, GRID_W - 1)
    pos = row & seg_mask
    prev = jnp.where(pos == 0, 0.0, pltpu.roll(x, 1, 0))
    nxt = jnp.where(pos == seg_mask, 0.0, pltpu.roll(x, rows - 1, 0))
    return prev, nxt


def _rwkv_pre_body(rw_ref, mu_ref, kk_s_ref, ka_ref, w0_ref, a0_ref, wl_ref, al_ref, gup_ref, rk_ref, seg_ref,
                   r_ref, kk_ref, v_ref, lw_ref, b_ref, kd_ref, g_ref, bonus_ref, *, n_ctx_blocks):
    x = rw_ref[...]
    prev, nxt = _neighbours(x, n_ctx_blocks)
    cols = x + mu_ref[...] * (0.5 * (prev + nxt) - x)
    w = RW_WIDTH
    r = cols[:, :w]
    k = cols[:, w:2 * w]
    v = cols[:, 2 * w:3 * w]
    wd = cols[:, 3 * w:3 * w + LORA_W]
    ad = cols[:, 3 * w + LORA_W:3 * w + LORA_W + LORA_A]
    gd = cols[:, 3 * w + LORA_W + LORA_A:]
    seg = seg_ref[...]
    kkr = k * kk_s_ref[...]
    ss = _dot_x3(kkr * kkr, seg)
    kk = kkr / jnp.maximum(jnp.sqrt(ss), 1e-12)
    r_ref[...] = r
    kk_ref[...] = kk
    v_ref[...] = v
    g_ref[...] = _bdot(_sigmoid(gd), gup_ref[...])
    tw = jnp.tanh(wd)
    bonus = jnp.zeros_like(r)
    for d in range(2):
        w_logit = w0_ref[d:d + 1, :] + _bdot(tw, wl_ref[d])
        lw_ref[d] = -math.exp(-0.5) * _sigmoid(w_logit)
        a = _sigmoid(a0_ref[d:d + 1, :] + _bdot(ad, al_ref[d]))
        kd = k * (1.0 + (a - 1.0) * ka_ref[...])
        kd_ref[d] = kd
        b_ref[d] = kk * a
        bonus = bonus + _dot_x3(r * kd * rk_ref[d:d + 1, :], seg) * v
    bonus_ref[...] = bonus


def _rwkv_pre_call(rw, p, seg, n_ctx_blocks):
    t = rw.shape[0]
    nb = t // TOKEN_BLOCK
    w = RW_WIDTH
    tok = lambda n: pl.BlockSpec((TOKEN_BLOCK, n), lambda i: (i, 0))
    tok2 = pl.BlockSpec((2, TOKEN_BLOCK, w), lambda i: (0, i, 0))
    full = lambda *s: pl.BlockSpec(s, lambda i: (0,) * len(s))
    o1 = jax.ShapeDtypeStruct((t, w), F32)
    o2 = jax.ShapeDtypeStruct((2, t, w), F32)
    return pl.pallas_call(
        functools.partial(_rwkv_pre_body, n_ctx_blocks=n_ctx_blocks),
        out_shape=(o1, o1, o1, o2, o2, o2, o1, o1),
        grid=(nb,),
        in_specs=[tok(RW_COLS), full(1, RW_COLS), full(1, w), full(1, w), full(2, w), full(2, w),
                  full(2, LORA_W, w), full(2, LORA_A, w), full(LORA_G, w), full(2, w), full(w, w)],
        out_specs=(tok(w), tok(w), tok(w), tok2, tok2, tok2, tok(w), tok(w)),
        compiler_params=_cparams(("parallel",), VMEM_LIMIT),
        name="rwkv_pre",
    )(rw, p['mu_shift'].reshape(1, RW_COLS), p['k_k'].reshape(1, w), p['k_a'].reshape(1, w), p['w0'], p['a0'],
      p['w_lora_up'], p['a_lora_up'], p['g_up'], p['r_k'].reshape(2, w), seg)


def _scan_body(tbl_ref, r_ref, kk_ref, v_ref, lw_ref, b_ref, kd_ref, s0_ref, y_ref, sfin_ref, s_ref, *, nsteps):
    d = pl.program_id(0)
    i = pl.program_id(1)
    base = (d * nsteps + i) * 4
    c = SCAN_CHUNK
    hd = RW_HEAD

    @pl.when(tbl_ref[base + 2] == 1)
    def _():
        s_ref[...] = s0_ref[...]

    rowi = lax.broadcasted_iota(jnp.int32, (c, c), 0)
    coli = lax.broadcasted_iota(jnp.int32, (c, c), 1)
    diff = jnp.where(d == 1, coli - rowi, rowi - coli)
    strict = diff > 0
    incl = diff >= 0
    same_blk = (rowi // INV_BLOCK) == (coli // INV_BLOCK)
    eye = (rowi == coli).astype(F32)

    lw = lw_ref[...]
    hi, mid, lo = _split3(lw)
    tri = incl.astype(BF16)
    dd = functools.partial(jnp.dot, preferred_element_type=F32)
    cum = dd(tri, hi) + (dd(tri, mid) + dd(tri, lo))
    tot = jnp.sum(lw, axis=0, keepdims=True)
    e_cum = jnp.exp(cum)
    e_neg = jnp.exp(-cum)
    e_rem = jnp.exp(tot - cum)
    g_tot = jnp.exp(tot)
    kk = kk_ref[...]
    bb = b_ref[...]
    kd = kd_ref[...]
    alpha = kk * jnp.exp(cum - lw)
    beta = bb * e_neg
    kappa = kd * e_neg
    rho = r_ref[...] * e_cum
    kap_g = kd * e_rem
    bet_g = bb * e_rem
    v = v_ref[...]

    def heads(a):
        return jnp.stack([a[:, h * hd:(h + 1) * hd] for h in range(RW_HEADS)], axis=0)

    a_h, b_h, k_h, r_h, v_h = heads(alpha), heads(beta), heads(kappa), heads(rho), heads(v)
    ar = jnp.concatenate([a_h, r_h], axis=1)
    gb = _bmm_nt(ar, b_h)
    gk = _bmm_nt(ar, k_h)
    low = jnp.where(strict, gb[:, :c], 0.0)
    a_ka = jnp.where(strict, gk[:, :c], 0.0)
    a_br = jnp.where(incl, gb[:, c:], 0.0)
    a_kr = jnp.where(incl, gk[:, c:], 0.0)
    nd = jnp.where(same_blk, -low, 0.0)
    loff = jnp.where(same_blk, 0.0, low)
    x = eye + nd
    n2 = _bmm(nd, nd)
    x = x + _bmm(x, n2)
    n4 = _bmm(n2, n2)
    x = x + _bmm(x, n4)
    n8 = _bmm(n4, n4)
    x = x + _bmm(x, n8)
    m = _bmm(x, loff)
    m2 = _bmm(m, m)
    y1 = x + _bmm(m2, x)
    tinv = y1 - _bmm(m, y1)
    w_h = _bmm(a_ka, v_h)
    rhs = jnp.concatenate([a_h, w_h], axis=2)
    x0 = _bmm(tinv, rhs)
    res = rhs - x0 - _bmm(low, x0)
    xs = x0 + _bmm(tinv, res)
    p_h = xs[:, :, :hd]
    q_h = xs[:, :, hd:]
    s_old = s_ref[...]
    uy = _bmm_nt(jnp.concatenate([p_h, r_h], axis=1), s_old)
    u_h = uy[:, :c] + q_h
    y_h = uy[:, c:] + _bmm(a_kr, v_h) - _bmm(a_br, u_h)
    zv = jnp.concatenate([v_h, u_h], axis=1)
    zk = jnp.concatenate([heads(kap_g), -heads(bet_g)], axis=1)
    s_ref[...] = s_old * heads(g_tot) + _bmm_tn(zv, zk)
    for h in range(RW_HEADS):
        y_ref[:, h * hd:(h + 1) * hd] = y_h[h]

    @pl.when(tbl_ref[base + 3] == 1)
    def _():
        sfin_ref[...] = s_ref[...]


def _scan_call(tbl, nsteps, nseq, r, kk, v, lw, b, kd, s0):
    t = r.shape[0]
    w = RW_WIDTH
    c = SCAN_CHUNK

    def rb(d, i, tb):
        return tb[(d * nsteps + i) * 4]

    def sq(d, i, tb):
        return tb[(d * nsteps + i) * 4 + 1]

    shared = pl.BlockSpec((c, w), lambda d, i, tb: (rb(d, i, tb), 0))
    perdir = pl.BlockSpec((None, c, w), lambda d, i, tb: (d, rb(d, i, tb), 0))
    st = pl.BlockSpec((None, None, RW_HEADS, RW_HEAD, RW_HEAD), lambda d, i, tb: (sq(d, i, tb), d, 0, 0, 0))
    gs = pltpu.PrefetchScalarGridSpec(
        num_scalar_prefetch=1,
        grid=(2, nsteps),
        in_specs=[shared, shared, shared, perdir, perdir, perdir, st],
        out_specs=(perdir, st),
        scratch_shapes=[pltpu.VMEM((RW_HEADS, RW_HEAD, RW_HEAD), F32)],
    )
    return pl.pallas_call(
        functools.partial(_scan_body, nsteps=nsteps),
        out_shape=(jax.ShapeDtypeStruct((2, t, w), F32),
                   jax.ShapeDtypeStruct((nseq, 2, RW_HEADS, RW_HEAD, RW_HEAD), F32)),
        grid_spec=gs,
        compiler_params=_cparams(("arbitrary", "arbitrary"), VMEM_LIMIT),
        name="rwkv_scan",
    )(tbl, r, kk, v, lw, b, kd, s0)


def _scan_table(seq_lens):
    c = SCAN_CHUNK
    rows = []
    for d in range(2):
        start = 0
        for s, n in enumerate(seq_lens):
            nc = n // c
            for j in range(nc):
                blk = start + (j if d == 0 else nc - 1 - j)
                rows.append((blk, s, int(j == 0), int(j == nc - 1)))
            start += nc
    return np.asarray(rows, np.int32).reshape(-1), len(rows) // 2


def _hy_pre_body(hy_ref, w_ref, b_ref, u_ref, x1_ref, x2_ref, *, n_ctx_blocks):
    x = hy_ref[...]
    prev, nxt = _neighbours(x, n_ctx_blocks)
    cols = prev * w_ref[0:1, :] + x * w_ref[1:2, :] + nxt * w_ref[2:3, :] + b_ref[...]
    w = HY_WIDTH
    u_ref[...] = cols[:, :w]
    x1_ref[...] = cols[:, w:2 * w]
    x2_ref[...] = cols[:, 2 * w:]


def _hy_pre_call(hy, conv_w, conv_b, n_ctx_blocks):
    t = hy.shape[0]
    tok = lambda n: pl.BlockSpec((TOKEN_BLOCK, n), lambda i: (i, 0))
    o = jax.ShapeDtypeStruct((t, HY_WIDTH), F32)
    return pl.pallas_call(
        functools.partial(_hy_pre_body, n_ctx_blocks=n_ctx_blocks),
        out_shape=(o, o, o),
        grid=(t // TOKEN_BLOCK,),
        in_specs=[tok(HY_COLS), pl.BlockSpec((3, HY_COLS), lambda i: (0, 0)),
                  pl.BlockSpec((1, HY_COLS), lambda i: (0, 0))],
        out_specs=(tok(HY_WIDTH),) * 3,
        compiler_params=_cparams(("parallel",)),
        name="hyena_pre",
    )(hy, conv_w, conv_b.reshape(1, HY_COLS))


def _dot_f32(a, b):
    a_hi, a_lo = _split2(a)
    b_hi, b_lo = _split2(b)
    d = functools.partial(jnp.dot, preferred_element_type=F32)
    return d(a_hi, b_hi) + (d(a_hi, b_lo) + d(a_lo, b_hi))


def _hy_filter_body(bands_ref, w1t_ref, w1c_ref, w1s_ref, b1_ref, w2_ref, b2_ref, fr_ref, w3_ref, dl_ref, h_ref,
                    *, n, tile):
    pos = (lax.broadcasted_iota(jnp.int32, (tile, LANES), 0) + pl.program_id(0) * tile).astype(F32)
    tcol = pos / n
    ang = (2.0 * math.pi / n) * pos * bands_ref[...]
    pre1 = (tcol[:, :HY_FFN] * w1t_ref[...] + _dot_f32(jnp.cos(ang), w1c_ref[...])
            + _dot_f32(jnp.sin(ang), w1s_ref[...]) + b1_ref[...])
    fr = fr_ref[...]
    h1 = jnp.sin(fr * pre1)
    h2 = jnp.sin(fr * (_dot_f32(h1, w2_ref[...]) + b2_ref[...]))
    h = _dot_f32(h2, w3_ref[...])
    for j in range(HY_ORDER * 2):
        for q in range(HY_WIDTH // LANES):
            lo = j * HY_WIDTH + q * LANES
            win = jnp.exp(-tcol * dl_ref[:, q * LANES:(q + 1) * LANES])
            h_ref[:, lo:lo + LANES] = h[:, lo:lo + LANES] * win


def _hy_filter_call(n, p):
    bands = jnp.zeros((1, LANES), F32).at[0, :HY_BANDS].set(
        jnp.linspace(1e-4, HY_BANDS - 1, HY_BANDS, dtype=F32))
    max_decay = math.log(HY_DECAY_TARGET) / HY_DECAY_SHORT_PCT
    min_decay = math.log(HY_DECAY_TARGET) / HY_DECAY_LONG_PCT
    deltas = jnp.abs(jnp.linspace(min_decay, max_decay, HY_WIDTH, dtype=F32)).reshape(1, HY_WIDTH)
    w1 = p['hy_f_w1']
    w1t = w1[0:1]
    w1c = jnp.zeros((LANES, HY_FFN), F32).at[:HY_BANDS].set(w1[1:1 + HY_BANDS])
    w1s = jnp.zeros((LANES, HY_FFN), F32).at[:HY_BANDS].set(w1[1 + HY_BANDS:HY_EMB])
    nout = HY_ORDER * 2 * HY_WIDTH
    tile = min(n, 512)
    full = lambda *s: pl.BlockSpec(s, lambda i: (0,) * len(s))
    return pl.pallas_call(
        functools.partial(_hy_filter_body, n=n, tile=tile),
        out_shape=jax.ShapeDtypeStruct((n, nout), F32),
        grid=(n // tile,),
        in_specs=[full(1, LANES), full(1, HY_FFN), full(LANES, HY_FFN), full(LANES, HY_FFN), full(1, HY_FFN),
                  full(HY_FFN, HY_FFN), full(1, HY_FFN), full(1, HY_FFN), full(HY_FFN, nout), full(1, HY_WIDTH)],
        out_specs=pl.BlockSpec((tile, nout), lambda i: (i, 0)),
        compiler_params=_cparams(("parallel",), VMEM_LIMIT),
        name="hyena_filter",
    )(bands, w1t, w1c, w1s, p['hy_f_b1'].reshape(1, HY_FFN), p['hy_f_w2'], p['hy_f_b2'].reshape(1, HY_FFN),
      p['hy_f_freq'].reshape(1, HY_FFN), p['hy_f_w3'], deltas)


def _dft_body(*refs, pre, post, two, half_in, half_mid):
    it = iter(refs)
    x = next(it)[...]
    if pre:
        m1 = next(it)[...]
        m2 = next(it)[...]
        x = x * m1 + pltpu.roll(x, half_in, 1) * m2
    g1h = next(it)[...]
    g1l = next(it)[...]
    y = _dot3(x, g1h, g1l)
    if post:
        m1 = next(it)[...]
        m2 = next(it)[...]
        y = y * m1 + pltpu.roll(y, half_mid, 1) * m2
    if two:
        g2h = next(it)[...]
        g2l = next(it)[...]
        y = _dot3(y, g2h, g2l)
    o_ref = next(it)
    o_ref[...] = y


def _dft_call(x, g1, tile, pre=None, post=None, g2=None, name="hyena_dft"):
    bsz, rows, k = x.shape
    args = [x]
    specs = [pl.BlockSpec((None, tile, k), lambda b, j: (b, j, 0))]

    def add_mul(mm):
        for a in mm:
            nblk = a.shape[0] // tile
            specs.append(pl.BlockSpec((tile, a.shape[1]), lambda b, j, nblk=nblk: (j % nblk, 0)))
            args.append(a)

    def add_mat(g):
        for a in g:
            specs.append(pl.BlockSpec(a.shape, lambda b, j: (0, 0)))
            args.append(a)

    if pre is not None:
        add_mul(pre)
    add_mat(g1)
    if post is not None:
        add_mul(post)
    if g2 is not None:
        add_mat(g2)
    mid = g1[0].shape[1]
    nout = g2[0].shape[1] if g2 is not None else mid
    body = functools.partial(_dft_body, pre=pre is not None, post=post is not None, two=g2 is not None,
                             half_in=k // 2, half_mid=mid // 2)
    return pl.pallas_call(
        body,
        out_shape=jax.ShapeDtypeStruct((bsz, rows, nout), F32),
        grid=(bsz, rows // tile),
        in_specs=specs,
        out_specs=pl.BlockSpec((None, tile, nout), lambda b, j: (b, j, 0)),
        compiler_params=_cparams(("parallel", "parallel"), VMEM_LIMIT),
        name=name,
    )(*args)


def _hilo(a):
    a = np.asarray(a, np.float64)
    hi = jnp.asarray(a, F32).astype(BF16)
    lo = (jnp.asarray(a, F32) - hi.astype(F32)).astype(BF16)
    return hi, lo


def _cplx_block(f):
    return np.block([[f.real, f.imag], [-f.imag, f.real]])


@functools.lru_cache(maxsize=None)
def _dft_plan(n):
    big = 2 * n
    if big <= 512:
        t = np.arange(n)[:, None]
        k = np.arange(big)[None, :]
        fwd = np.exp(-2j * np.pi * t * k / big)
        fwd_full = np.exp(-2j * np.pi * np.arange(big)[:, None] * k / big)
        inv = np.exp(2j * np.pi * np.arange(big)[:, None] * np.arange(n)[None, :] / big) / big
        return dict(stages=1,
                    fwd=np.concatenate([fwd.real, fwd.imag], axis=1),
                    fwd_full=np.concatenate([fwd_full.real, fwd_full.imag], axis=1),
                    inv=np.concatenate([inv.real, -inv.imag], axis=0))
    n1, n2 = 128, big // 128
    t1 = np.arange(n1)[:, None]
    k1 = np.arange(n1)[None, :]
    fa = np.exp(-2j * np.pi * t1 * k1 / n1)
    fa_ri = np.concatenate([fa.real, fa.imag], axis=1)
    t2 = np.arange(n2)[:, None]
    tw = np.exp(-2j * np.pi * t2 * k1 / big)
    k2 = np.arange(n2)[None, :]
    fb = np.exp(-2j * np.pi * t2 * k2 / n2)
    fb_inv = np.exp(2j * np.pi * np.arange(n2)[:, None] * np.arange(n2)[None, :] / n2) / big
    fa_inv = np.exp(2j * np.pi * np.arange(n1)[:, None] * np.arange(n1 // 2)[None, :] / n1)
    return dict(stages=2, n1=n1, n2=n2,
                fa_half=fa_ri[:n1 // 2], fa_full=fa_ri,
                tw_m1=np.concatenate([tw.real, tw.real], axis=1),
                tw_m2=np.concatenate([-tw.imag, tw.imag], axis=1),
                twc_m2=np.concatenate([tw.imag, -tw.imag], axis=1),
                fb=_cplx_block(fb), fb_inv=_cplx_block(fb_inv),
                fa_inv=np.concatenate([fa_inv.real, -fa_inv.imag], axis=0))


def _long_conv_setup(n, h):
    c = HY_WIDTH
    plan = _dft_plan(n)
    hh = h.reshape(n, HY_ORDER, 2, c)
    f = jnp.concatenate([hh[:, :, 0], jnp.zeros((1, HY_ORDER, c), F32), hh[:0:-1, :, 1]], axis=0)
    if plan['stages'] == 1:
        xf = jnp.transpose(f, (1, 2, 0))
        spec = _dft_call(xf, _hilo(plan['fwd_full']), tile=min(c, 256), name="hyena_filter_dft")
        fr, fi = spec[..., :2 * n], spec[..., 2 * n:]
        return [(jnp.concatenate([fr[o], fr[o]], axis=1), jnp.concatenate([-fi[o], fi[o]], axis=1))
                for o in range(HY_ORDER)]
    n1, n2 = plan['n1'], plan['n2']
    xf = jnp.transpose(f.reshape(n1, n2, HY_ORDER, c), (2, 3, 1, 0)).reshape(HY_ORDER, c * n2, n1)
    tile = 8 * n2
    tw = (jnp.asarray(np.tile(plan['tw_m1'], (8, 1)), F32), jnp.asarray(np.tile(plan['tw_m2'], (8, 1)), F32))
    a = _dft_call(xf, _hilo(plan['fa_full']), tile=tile, post=tw, name="hyena_filter_dft_a")
    a = jnp.transpose(a.reshape(HY_ORDER, c, n2, 2, n1), (0, 1, 4, 3, 2)).reshape(HY_ORDER, c * n1, 2 * n2)
    spec = _dft_call(a, _hilo(plan['fb']), tile=512, name="hyena_filter_dft_b")
    fr, fi = spec[..., :n2], spec[..., n2:]
    return [(jnp.concatenate([fr[o], fr[o]], axis=1), jnp.concatenate([-fi[o], fi[o]], axis=1))
            for o in range(HY_ORDER)]


def _long_conv(z, hmul):
    bsz, n, c = z.shape
    plan = _dft_plan(n)
    if plan['stages'] == 1:
        x = jnp.transpose(z, (0, 2, 1))
        y = _dft_call(x, _hilo(plan['fwd']), tile=min(c, 256), post=hmul, g2=_hilo(plan['inv']),
                      name="hyena_conv_short")
        return jnp.transpose(y, (0, 2, 1))
    n1, n2 = plan['n1'], plan['n2']
    h1 = n1 // 2
    x = jnp.transpose(z.reshape(bsz, h1, n2, c), (0, 3, 2, 1)).reshape(bsz, c * n2, h1)
    tile = 8 * n2
    tw = (jnp.asarray(np.tile(plan['tw_m1'], (8, 1)), F32), jnp.asarray(np.tile(plan['tw_m2'], (8, 1)), F32))
    a = _dft_call(x, _hilo(plan['fa_half']), tile=tile, post=tw, name="hyena_dft_a")
    a = jnp.transpose(a.reshape(bsz, c, n2, 2, n1), (0, 1, 4, 3, 2)).reshape(bsz, c * n1, 2 * n2)
    m = _dft_call(a, _hilo(plan['fb']), tile=512, post=hmul, g2=_hilo(plan['fb_inv']), name="hyena_dft_mid")
    m = jnp.transpose(m.reshape(bsz, c, n1, 2, n2), (0, 1, 4, 3, 2)).reshape(bsz, c * n2, 2 * n1)
    twc = (jnp.asarray(np.tile(plan['tw_m1'], (8, 1)), F32), jnp.asarray(np.tile(plan['twc_m2'], (8, 1)), F32))
    y = _dft_call(m, _hilo(plan['fa_inv']), tile=tile, pre=twc, name="hyena_dft_a_inv")
    return jnp.transpose(y.reshape(bsz, c, n2, h1), (0, 3, 2, 1)).reshape(bsz, n, c)


def _hy_gate_body(x_ref, y_ref, z_ref, s_ref, o_ref):
    o_ref[...] = x_ref[...] * (y_ref[...] + z_ref[...] * s_ref[...])


def _hy_gate_call(x, y, z, skip):
    t, c = x.shape
    tok = pl.BlockSpec((512, c), lambda i: (i, 0))
    return pl.pallas_call(
        _hy_gate_body,
        out_shape=jax.ShapeDtypeStruct((t, c), F32),
        grid=(t // 512,),
        in_specs=[tok, tok, tok, pl.BlockSpec((1, c), lambda i: (0, 0))],
        out_specs=tok,
        compiler_params=_cparams(("parallel",)),
        name="hyena_gate",
    )(x, y, z, skip.reshape(1, c))


def _hyena_branch(hy, p, groups, n_ctx_blocks):
    u, x1, x2 = _hy_pre_call(hy, p['hy_conv_w'], p['hy_conv_b'], n_ctx_blocks)
    outs = []
    for start, bsz, n in groups:
        rows = bsz * n
        hmul = _long_conv_setup(n, _hy_filter_call(n, p))
        ug, x1g, x2g = (a[start:start + rows] for a in (u, x1, x2))
        y = _long_conv(ug.reshape(bsz, n, HY_WIDTH), hmul[0]).reshape(rows, HY_WIDTH)
        z = _hy_gate_call(x1g, y, ug, p['hy_skip'][0])
        y = _long_conv(z.reshape(bsz, n, HY_WIDTH), hmul[1]).reshape(rows, HY_WIDTH)
        outs.append(_hy_gate_call(x2g, y, z, p['hy_skip'][1]))
    return jnp.concatenate(outs, axis=0)


def _merge_body(x_ref, y_ref, bonus_ref, g_ref, yb_ref, gt_ref, g1_ref, sh2_ref, sc2_ref, seg_ref, gng_ref,
                gnb_ref, wpa_ref, wpb_ref, wout_ref, lng_ref, lnb_ref, rwh_ref, rwl_ref, rb_ref,
                x1_ref, h2_ref, ti_ref, tg_ref, *, alpha):
    seg = seg_ref[...]
    y = y_ref[0] + y_ref[1]
    inv = 1.0 / RW_HEAD
    mu = _dot_x3(y, seg) * inv
    yc = y - mu
    var = _dot_x3(yc * yc, seg) * inv
    yn = yc * lax.rsqrt(var + GN_EPS) * gng_ref[...] + gnb_ref[...]
    y_a = (yn + bonus_ref[...]) * g_ref[...]
    gt = gt_ref[...]
    merged = gt[:, :D_MODEL] * _bdot(y_a, wpa_ref[...]) + gt[:, D_MODEL:] * _bdot(yb_ref[...], wpb_ref[...])
    mix = _bdot(merged, wout_ref[...])
    x1 = _layer_norm(alpha * x_ref[...] + g1_ref[...] * mix, lng_ref[...], lnb_ref[...])
    x1_ref[...] = x1
    h2 = x1 * (1.0 + sc2_ref[...]) + sh2_ref[...]
    h2_ref[...] = h2
    h_hi, h_lo = _split2(h2)
    d = functools.partial(jnp.dot, preferred_element_type=F32)
    logits = d(h_hi, rwh_ref[...]) + (d(h_hi, rwl_ref[...]) + d(h_lo, rwh_ref[...])) + rb_ref[...]
    lane = lax.broadcasted_iota(jnp.int32, logits.shape, 1)
    neg = jnp.float32(-jnp.inf)
    cur = jnp.where(lane < N_EXPERTS, logits, neg)
    top_i = jnp.zeros(logits.shape, jnp.int32)
    top_e = jnp.zeros(logits.shape, F32)
    den = jnp.zeros((logits.shape[0], 1), F32)
    v0 = None
    for j in range(TOP_K):
        mx = jnp.max(cur, axis=-1, keepdims=True)
        idx = jnp.min(jnp.where(cur == mx, lane, LANES), axis=-1, keepdims=True)
        if j == 0:
            v0 = mx
        e = jnp.exp(mx - v0)
        den = den + e
        top_i = jnp.where(lane == j, idx, top_i)
        top_e = jnp.where(lane == j, e, top_e)
        cur = jnp.where(lane == idx, neg, cur)
    ti_ref[...] = top_i
    tg_ref[...] = top_e / den


def _merge_call(x, y, bonus, g, yb, gates, mod_l, p, seg, cond_idx, alpha):
    t = x.shape[0]
    w = RW_WIDTH
    tok = lambda n: pl.BlockSpec((TOKEN_BLOCK, n), lambda i: (i, 0))
    modspec = lambda j: pl.BlockSpec((None, 1, D_MODEL), lambda i: (cond_idx(i), 0, j))
    full = lambda *s: pl.BlockSpec(s, lambda i: (0,) * len(s))
    rw_pad = jnp.zeros((D_MODEL, LANES), F32).at[:, :N_EXPERTS].set(p['router_w'])
    rw_hi = rw_pad.astype(BF16)
    rw_lo = (rw_pad - rw_hi.astype(F32)).astype(BF16)
    rb = jnp.zeros((1, LANES), F32).at[0, :N_EXPERTS].set(p['router_b'])
    o = jax.ShapeDtypeStruct((t, D_MODEL), F32)
    return pl.pallas_call(
        functools.partial(_merge_body, alpha=alpha),
        out_shape=(o, o, jax.ShapeDtypeStruct((t, LANES), jnp.int32), jax.ShapeDtypeStruct((t, LANES), F32)),
        grid=(t // TOKEN_BLOCK,),
        in_specs=[tok(D_MODEL), pl.BlockSpec((2, TOKEN_BLOCK, w), lambda i: (0, i, 0)), tok(w), tok(w), tok(w),
                  tok(2 * D_MODEL), modspec(2), modspec(3), modspec(4), full(w, w), full(1, w), full(1, w),
                  full(w, D_MODEL), full(w, D_MODEL), full(D_MODEL, D_MODEL), full(1, D_MODEL), full(1, D_MODEL),
                  full(D_MODEL, LANES), full(D_MODEL, LANES), full(1, LANES)],
        out_specs=(tok(D_MODEL), tok(D_MODEL), tok(LANES), tok(LANES)),
        compiler_params=_cparams(("parallel",), VMEM_LIMIT),
        name="merge_ln_router",
    )(x, y, bonus, g, yb, gates, mod_l, mod_l, mod_l, seg, p['gn_g'].reshape(1, w), p['gn_b'].reshape(1, w),
      p['w_pa'].astype(BF16), p['w_pb'].astype(BF16), p['w_out'].astype(BF16),
      p['ln1_g'].reshape(1, D_MODEL), p['ln1_b'].reshape(1, D_MODEL), rw_hi, rw_lo, rb)


DEINT_COLS = 256


def _deint_body(w_ref, p_ref, g_ref, l_ref):
    y = jnp.dot(w_ref[...].astype(BF16), p_ref[...], preferred_element_type=F32)
    half = DEINT_COLS // 2
    g_ref[...] = y[:, :half].astype(BF16)
    l_ref[...] = y[:, half:].astype(BF16)


def _deint_call(w):
    e, k, n2 = w.shape
    half = DEINT_COLS // 2
    sel = np.zeros((DEINT_COLS, DEINT_COLS), np.float32)
    sel[2 * np.arange(half), np.arange(half)] = 1.0
    sel[2 * np.arange(half) + 1, half + np.arange(half)] = 1.0
    o = jax.ShapeDtypeStruct((e, k, n2 // 2), BF16)
    return pl.pallas_call(
        _deint_body,
        out_shape=(o, o),
        grid=(e, n2 // DEINT_COLS),
        in_specs=[pl.BlockSpec((None, k, DEINT_COLS), lambda i, j: (i, 0, j)),
                  pl.BlockSpec((DEINT_COLS, DEINT_COLS), lambda i, j: (0, 0))],
        out_specs=(pl.BlockSpec((None, k, half), lambda i, j: (i, 0, j)),) * 2,
        compiler_params=_cparams(("parallel", "parallel")),
        name="expert_w_split",
    )(w, jnp.asarray(sel, BF16))


def _moe_body(blk_e_ref, n_on_ref, tok_ref, tokn_ref, dst_ref, h_hbm, wg_ref, wl_ref, bg_ref, bl_ref, wdn_ref,
              bdn_ref, y_hbm, xbuf, ybuf, sem_in, sem_out, *, n_real):
    i = pl.program_id(0)
    n_on = n_on_ref[0]
    slot = i % 2

    def gather_start(tref, s):
        def body(r, c):
            pltpu.make_async_copy(h_hbm.at[pl.ds(tref[0, r], 1), :], xbuf.at[s, pl.ds(r, 1), :],
                                  sem_in.at[s]).start()
            return c
        lax.fori_loop(0, MOE_ROWS, body, 0, unroll=8)

    def gather_wait(s):
        pltpu.make_async_copy(h_hbm.at[pl.ds(0, MOE_ROWS), :], xbuf.at[s], sem_in.at[s]).wait()

    def scatter_start(s):
        def body(r, c):
            pltpu.make_async_copy(ybuf.at[s, pl.ds(r, 1), :], y_hbm.at[pl.ds(dst_ref[0, r], 1), :],
                                  sem_out.at[s]).start()
            return c
        lax.fori_loop(0, MOE_ROWS, body, 0, unroll=8)

    def scatter_wait(s):
        pltpu.make_async_copy(ybuf.at[s], y_hbm.at[pl.ds(0, MOE_ROWS), :], sem_out.at[s]).wait()

    @pl.when(i == 0)
    def _():
        ybuf[...] = jnp.zeros_like(ybuf)
        fills = [pltpu.make_async_copy(ybuf.at[s], y_hbm.at[pl.ds(n_real + s * MOE_ROWS, MOE_ROWS), :],
                                       sem_out.at[s]) for s in range(2)]
        for cp in fills:
            cp.start()
        for cp in fills:
            cp.wait()

    @pl.when(jnp.logical_and(i == 0, n_on > 0))
    def _():
        gather_start(tok_ref, 0)

    @pl.when(i < n_on)
    def _():
        @pl.when(i + 1 < n_on)
        def _():
            gather_start(tokn_ref, 1 - slot)

        gather_wait(slot)
        x = xbuf[slot].astype(BF16)
        d = functools.partial(jnp.dot, preferred_element_type=F32)
        glu = jnp.minimum(d(x, wg_ref[...]) + bg_ref[...], SWIGLU_LIMIT)
        lin = jnp.clip(d(x, wl_ref[...]) + bl_ref[...], -SWIGLU_LIMIT, SWIGLU_LIMIT)
        act = glu * _sigmoid(SWIGLU_ALPHA * glu) * (lin + 1.0)
        ybuf[slot] = d(act.astype(BF16), wdn_ref[...]) + bdn_ref[...]
        scatter_start(slot)

        @pl.when(i >= 1)
        def _():
            scatter_wait(1 - slot)

        @pl.when(i == n_on - 1)
        def _():
            scatter_wait(slot)


def _moe_call(h2, top_i, wg, wl, bg, bl, wdn, bdn):
    t = h2.shape[0]
    m = t * TOP_K
    e = N_EXPERTS
    blk = MOE_ROWS
    flat_e = top_i.reshape(-1)
    order = jnp.argsort(flat_e, stable=True).astype(jnp.int32)
    sizes = jnp.bincount(flat_e, length=e).astype(jnp.int32)
    padded = (sizes + blk - 1) // blk * blk
    pad_end = jnp.cumsum(padded)
    pad_start = pad_end - padded
    grp_start = jnp.cumsum(sizes) - sizes
    n_blocks = -(-(m + e * (blk - 1)) // blk)
    blk_e = jnp.minimum(jnp.searchsorted(pad_end, jnp.arange(n_blocks, dtype=jnp.int32) * blk, side='right'),
                        e - 1).astype(jnp.int32)
    pidx = jnp.arange(n_blocks * blk, dtype=jnp.int32)
    e_p = jnp.repeat(blk_e, blk)
    idx = pidx - pad_start[e_p]
    valid = idx < sizes[e_p]
    assign = order[jnp.clip(grp_start[e_p] + idx, 0, m - 1)]
    tok_row = jnp.where(valid, assign // TOP_K, 0).astype(jnp.int32)
    spare = m + ((pidx // blk) % 2) * blk + pidx % blk
    dst_row = jnp.where(valid, (assign % TOP_K) * t + assign // TOP_K, spare).astype(jnp.int32)
    n_on = (pad_end[-1] // blk).astype(jnp.int32).reshape(1)
    tok3 = tok_row.reshape(n_blocks, 1, blk)

    smem = lambda f: pl.BlockSpec((None, 1, blk), f, memory_space=pltpu.SMEM)
    wspec = lambda a, b: pl.BlockSpec((None, a, b), lambda i, be, no: (be[i], 0, 0))
    gs = pltpu.PrefetchScalarGridSpec(
        num_scalar_prefetch=2,
        grid=(n_blocks,),
        in_specs=[smem(lambda i, be, no: (i, 0, 0)),
                  smem(lambda i, be, no: (jnp.minimum(i + 1, n_blocks - 1), 0, 0)),
                  smem(lambda i, be, no: (i, 0, 0)),
                  pl.BlockSpec(memory_space=pl.ANY),
                  wspec(D_MODEL, D_FF), wspec(D_MODEL, D_FF), wspec(1, D_FF), wspec(1, D_FF),
                  wspec(D_FF, D_MODEL), wspec(1, D_MODEL)],
        out_specs=pl.BlockSpec(memory_space=pl.ANY),
        scratch_shapes=[pltpu.VMEM((2, blk, D_MODEL), F32), pltpu.VMEM((2, blk, D_MODEL), F32),
                        pltpu.SemaphoreType.DMA((2,)), pltpu.SemaphoreType.DMA((2,))],
    )
    return pl.pallas_call(
        functools.partial(_moe_body, n_real=m),
        out_shape=jax.ShapeDtypeStruct((m + 2 * blk, D_MODEL), F32),
        grid_spec=gs,
        compiler_params=_cparams(("arbitrary",), VMEM_LIMIT),
        name="moe_experts",
    )(blk_e, n_on, tok3, tok3, dst_row.reshape(n_blocks, 1, blk), h2, wg, wl, bg.reshape(e, 1, D_FF),
      bl.reshape(e, 1, D_FF), wdn, bdn.reshape(e, 1, D_MODEL))


def _combine_body(x1_ref, y0_ref, y1_ref, y2_ref, y3_ref, tg_ref, g2_ref, lng_ref, lnb_ref, o_ref, *, alpha):
    tg = tg_ref[...]
    moe = tg[:, 0:1] * y0_ref[...]
    for j, y_ref in enumerate((y1_ref, y2_ref, y3_ref), start=1):
        moe = moe + tg[:, j:j + 1] * y_ref[...]
    o_ref[...] = _layer_norm(alpha * x1_ref[...] + g2_ref[...] * moe, lng_ref[...], lnb_ref[...])


def _combine_call(x1, yexp, tg, mod_l, p, cond_idx, alpha):
    t = x1.shape[0]
    nb = t // TOKEN_BLOCK
    tok = lambda n: pl.BlockSpec((TOKEN_BLOCK, n), lambda i: (i, 0))
    full = lambda *s: pl.BlockSpec(s, lambda i: (0,) * len(s))
    yspec = [pl.BlockSpec((TOKEN_BLOCK, D_MODEL), lambda i, j=j: (j * nb + i, 0)) for j in range(TOP_K)]
    return pl.pallas_call(
        functools.partial(_combine_body, alpha=alpha),
        out_shape=jax.ShapeDtypeStruct((t, D_MODEL), F32),
        grid=(nb,),
        in_specs=[tok(D_MODEL)] + yspec + [tok(LANES),
                  pl.BlockSpec((None, 1, D_MODEL), lambda i: (cond_idx(i), 0, 5)),
                  full(1, D_MODEL), full(1, D_MODEL)],
        out_specs=tok(D_MODEL),
        compiler_params=_cparams(("parallel",), VMEM_LIMIT),
        name="moe_combine_ln",
    )(x1, yexp, yexp, yexp, yexp, tg, mod_l, p['ln2_g'].reshape(1, D_MODEL), p['ln2_b'].reshape(1, D_MODEL))


def kernel(x_prompt, x_sample, c, state_rwkv, c_ctx, w_mod, b_mod, w_in, mu_shift, w0, w_lora_up, a0, a_lora_up, g_up, k_k, k_a, r_k, gn_g, gn_b, hy_conv_w, hy_conv_b, hy_f_w1, hy_f_b1, hy_f_w2, hy_f_b2, hy_f_freq, hy_f_w3, hy_skip, w_pa, w_pb, w_out, ln1_g, ln1_b, ln2_g, ln2_b, router_w, router_b, ex_w_up, ex_b_up, ex_w_down, ex_b_down):
    bsz, seq, dm = x_prompt.shape
    dbsz, dseq, _ = x_sample.shape
    depth = w_mod.shape[0]
    assert dm == D_MODEL and seq == TOKEN_BLOCK and dseq % TOKEN_BLOCK == 0 and TOKEN_BLOCK % GRID_W == 0
    assert 1 + dbsz <= 8
    alpha = (2 * depth) ** 0.25
    t_ctx = bsz * seq
    n_ctx_blocks = t_ctx // TOKEN_BLOCK
    lat_blocks = dseq // TOKEN_BLOCK

    def cond_idx(i):
        return jnp.where(i < n_ctx_blocks, 0, 1 + (i - n_ctx_blocks) // lat_blocks)

    x = jnp.concatenate([x_prompt.reshape(t_ctx, dm), x_sample.reshape(dbsz * dseq, dm)], axis=0)
    cond8 = jnp.zeros((8, dm), F32).at[0].set(c_ctx).at[1:1 + dbsz].set(c)
    mod = _mod_call(cond8, w_mod, b_mod)

    hd = RW_HEAD
    seg = (np.arange(RW_WIDTH)[:, None] // hd == np.arange(RW_WIDTH)[None, :] // hd)
    seg = jnp.asarray(seg, BF16)
    seq_lens = [seq] * bsz + [dseq] * dbsz
    tbl_np, nsteps = _scan_table(seq_lens)
    tbl = jnp.asarray(tbl_np)
    groups = [(0, bsz, seq), (t_ctx, dbsz, dseq)]

    new_states = []
    for l in range(depth):
        p = dict(mu_shift=mu_shift[l], w0=w0[l], w_lora_up=w_lora_up[l], a0=a0[l], a_lora_up=a_lora_up[l],
                 g_up=g_up[l], k_k=k_k[l], k_a=k_a[l], r_k=r_k[l], gn_g=gn_g[l], gn_b=gn_b[l],
                 hy_conv_w=hy_conv_w[l], hy_conv_b=hy_conv_b[l], hy_f_w1=hy_f_w1[l], hy_f_b1=hy_f_b1[l],
                 hy_f_w2=hy_f_w2[l], hy_f_b2=hy_f_b2[l], hy_f_freq=hy_f_freq[l], hy_f_w3=hy_f_w3[l],
                 hy_skip=hy_skip[l], w_pa=w_pa[l], w_pb=w_pb[l], w_out=w_out[l], ln1_g=ln1_g[l], ln1_b=ln1_b[l],
                 ln2_g=ln2_g[l], ln2_b=ln2_b[l], router_w=router_w[l], router_b=router_b[l])
        mod_l = mod[l].reshape(8, 1, 6 * dm)
        rw, hy, gates = _inproj_call(x, mod_l, w_in[l].astype(BF16), cond_idx)
        r, kk, v, lw, b, kd, g, bonus = _rwkv_pre_call(rw, p, seg, n_ctx_blocks)
        s0 = jnp.concatenate([jnp.zeros((bsz, 2, RW_HEADS, hd, hd), F32), state_rwkv[:, l].astype(F32)], axis=0)
        y, sfin = _scan_call(tbl, nsteps, bsz + dbsz, r, kk, v, lw, b, kd, s0)
        new_states.append(sfin[:bsz].astype(x_prompt.dtype))
        yb = _hyena_branch(hy, p, groups, n_ctx_blocks)
        x1, h2, top_i, top_g = _merge_call(x, y, bonus, g, yb, gates, mod_l, p, seg, cond_idx, alpha)
        wg, wl = _deint_call(ex_w_up[l])
        yexp = _moe_call(h2, top_i[:, :TOP_K], wg, wl, ex_b_up[l][:, 0::2], ex_b_up[l][:, 1::2],
                         ex_w_down[l].astype(BF16), ex_b_down[l])
        x = _combine_call(x1, yexp, top_g, mod_l, p, cond_idx, alpha)

    y_p = x[:t_ctx].reshape(bsz, seq, dm)
    y_s = x[t_ctx:].reshape(dbsz, dseq, dm)
    return (y_p, y_s, jnp.stack(new_states, axis=1))
```

```python
import functools
import math

import numpy as np
import jax
import jax.numpy as jnp
from jax import lax
from jax.experimental import pallas as pl
from jax.experimental.pallas import tpu as pltpu

F32 = jnp.float32
BF16 = jnp.bfloat16

D_MODEL = 1024
GRID_W = 64
RW_WIDTH = 512
RW_HEAD = 64
RW_HEADS = RW_WIDTH // RW_HEAD
LORA_W = 64
LORA_A = 64
LORA_G = 128
GN_EPS = 64e-5
HY_WIDTH = 512
HY_ORDER = 2
HY_BANDS = 16
HY_EMB = 2 * HY_BANDS + 1
HY_FFN = 64
HY_DECAY_TARGET = 1e-2
HY_DECAY_SHORT_PCT = 0.3
HY_DECAY_LONG_PCT = 1.5
RW_COLS = 3 * RW_WIDTH + LORA_W + LORA_A + LORA_G
HY_COLS = (HY_ORDER + 1) * HY_WIDTH
IN_COLS = RW_COLS + HY_COLS + 2 * D_MODEL
N_EXPERTS = 32
TOP_K = 4
D_FF = D_MODEL
SWIGLU_ALPHA = 1.702
SWIGLU_LIMIT = 7.0
LN_EPS = 1e-5

TOKEN_BLOCK = 256
SCAN_CHUNK = 64
INV_BLOCK = 16
DFT_N1 = 128
MOE_ROWS = 256
LANES = 128
VMEM_LIMIT = 56 * 1024 * 1024


def _cparams(sem, vmem=None):
    return pltpu.CompilerParams(dimension_semantics=sem, vmem_limit_bytes=vmem)


def _bdot(a, b):
    return jnp.dot(a.astype(BF16), b.astype(BF16), preferred_element_type=F32)


def _bdg(a, b, ca, cb):
    return lax.dot_general(a.astype(BF16), b.astype(BF16), (((ca,), (cb,)), ((0,), (0,))), preferred_element_type=F32)


def _bmm(a, b):
    return _bdg(a, b, 2, 1)


def _bmm_nt(a, b):
    return _bdg(a, b, 2, 2)


def _bmm_tn(a, b):
    return _bdg(a, b, 1, 1)


def _split2(x):
    hi = x.astype(BF16)
    lo = (x - hi.astype(F32)).astype(BF16)
    return hi, lo


def _split3(x):
    hi = x.astype(BF16)
    r1 = x - hi.astype(F32)
    mid = r1.astype(BF16)
    lo = (r1 - mid.astype(F32)).astype(BF16)
    return hi, mid, lo


def _dot_x3(x, g_bf16):
    hi, mid, lo = _split3(x)
    d = functools.partial(jnp.dot, preferred_element_type=F32)
    return d(hi, g_bf16) + (d(mid, g_bf16) + d(lo, g_bf16))


def _dot3(x, g_hi, g_lo):
    x_hi, x_lo = _split2(x)
    d = functools.partial(jnp.dot, preferred_element_type=F32)
    return d(x_hi, g_hi) + (d(x_hi, g_lo) + d(x_lo, g_hi))


def _sigmoid(x):
    return 1.0 / (1.0 + jnp.exp(-x))


def _layer_norm(x, g, b):
    mu = jnp.mean(x, axis=-1, keepdims=True)
    xc = x - mu
    var = jnp.mean(xc * xc, axis=-1, keepdims=True)
    return xc * lax.rsqrt(var + LN_EPS) * g + b


def _mod_body(c_ref, w_ref, b_ref, o_ref):
    c = c_ref[...]
    o_ref[...] = _bdot(c * _sigmoid(c), w_ref[...]) + b_ref[...]


def _mod_call(cond8, w_mod, b_mod):
    depth = w_mod.shape[0]
    tn = 1536
    return pl.pallas_call(
        _mod_body,
        out_shape=jax.ShapeDtypeStruct((depth, 8, 6 * D_MODEL), F32),
        grid=(depth, 6 * D_MODEL // tn),
        in_specs=[pl.BlockSpec((8, D_MODEL), lambda l, j: (0, 0)),
                  pl.BlockSpec((None, D_MODEL, tn), lambda l, j: (l, 0, j)),
                  pl.BlockSpec((None, 1, tn), lambda l, j: (l, 0, j))],
        out_specs=pl.BlockSpec((None, 8, tn), lambda l, j: (l, 0, j)),
        compiler_params=_cparams(("parallel", "parallel"), VMEM_LIMIT),
        name="adaln_mod",
    )(cond8, w_mod, b_mod.reshape(depth, 1, 6 * D_MODEL))


def _inproj_body(x_ref, sh_ref, sc_ref, w_ref, rw_ref, hy_ref, gt_ref):
    h = (x_ref[...] * (1.0 + sc_ref[...]) + sh_ref[...]).astype(BF16)
    d = functools.partial(jnp.dot, preferred_element_type=F32)
    rw_ref[...] = d(h, w_ref[:, :RW_COLS])
    hy_ref[...] = d(h, w_ref[:, RW_COLS:RW_COLS + HY_COLS])
    gt_ref[...] = _sigmoid(d(h, w_ref[:, RW_COLS + HY_COLS:]))


def _inproj_call(x, mod_l, w_in_bf16, cond_idx):
    t = x.shape[0]
    nb = t // TOKEN_BLOCK
    tok = lambda n: pl.BlockSpec((TOKEN_BLOCK, n), lambda i: (i, 0))
    modspec = lambda j: pl.BlockSpec((None, 1, D_MODEL), lambda i: (cond_idx(i), 0, j))
    return pl.pallas_call(
        _inproj_body,
        out_shape=(jax.ShapeDtypeStruct((t, RW_COLS), F32), jax.ShapeDtypeStruct((t, HY_COLS), F32),
                   jax.ShapeDtypeStruct((t, 2 * D_MODEL), F32)),
        grid=(nb,),
        in_specs=[tok(D_MODEL), modspec(0), modspec(1),
                  pl.BlockSpec((D_MODEL, IN_COLS), lambda i: (0, 0))],
        out_specs=(tok(RW_COLS), tok(HY_COLS), tok(2 * D_MODEL)),
        compiler_params=_cparams(("parallel",), VMEM_LIMIT),
        name="in_proj",
    )(x, mod_l, mod_l, w_in_bf16)


def _neighbours(x, n_ctx_blocks):
    rows = x.shape[0]
    row = lax.broadcasted_iota(jnp.int32, (rows, 1), 0)
    seg_mask = jnp.where(pl.program_id(0) < n_ctx_blocks, rows - 1, GRID_W - 1)
    pos = row & seg_mask
    prev = jnp.where(pos == 0, 0.0, pltpu.roll(x, 1, 0))
    nxt = jnp.where(pos == seg_mask, 0.0, pltpu.roll(x, rows - 1, 0))
    return prev, nxt


def _rwkv_pre_body(rw_ref, mu_ref, kk_s_ref, ka_ref, w0_ref, a0_ref, wl_ref, al_ref, gup_ref, rk_ref, seg_ref,
                   r_ref, kk_ref, v_ref, lw_ref, b_ref, kd_ref, g_ref, bonus_ref, *, n_ctx_blocks):
    x = rw_ref[...]
    prev, nxt = _neighbours(x, n_ctx_blocks)
    cols = x + mu_ref[...] * (0.5 * (prev + nxt) - x)
    w = RW_WIDTH
    r = cols[:, :w]
    k = cols[:, w:2 * w]
    v = cols[:, 2 * w:3 * w]
    wd = cols[:, 3 * w:3 * w + LORA_W]
    ad = cols[:, 3 * w + LORA_W:3 * w + LORA_W + LORA_A]
    gd = cols[:, 3 * w + LORA_W + LORA_A:]
    seg = seg_ref[...]
    kkr = k * kk_s_ref[...]
    ss = _dot_x3(kkr * kkr, seg)
    kk = kkr / jnp.maximum(jnp.sqrt(ss), 1e-12)
    r_ref[...] = r
    kk_ref[...] = kk
    v_ref[...] = v
    g_ref[...] = _bdot(_sigmoid(gd), gup_ref[...])
    tw = jnp.tanh(wd)
    bonus = jnp.zeros_like(r)
    for d in range(2):
        w_logit = w0_ref[d:d + 1, :] + _bdot(tw, wl_ref[d])
        lw_ref[d] = -math.exp(-0.5) * _sigmoid(w_logit)
        a = _sigmoid(a0_ref[d:d + 1, :] + _bdot(ad, al_ref[d]))
        kd = k * (1.0 + (a - 1.0) * ka_ref[...])
        kd_ref[d] = kd
        b_ref[d] = kk * a
        bonus = bonus + _dot_x3(r * kd * rk_ref[d:d + 1, :], seg) * v
    bonus_ref[...] = bonus


def _rwkv_pre_call(rw, p, seg, n_ctx_blocks):
    t = rw.shape[0]
    nb = t // TOKEN_BLOCK
    w = RW_WIDTH
    tok = lambda n: pl.BlockSpec((TOKEN_BLOCK, n), lambda i: (i, 0))
    tok2 = pl.BlockSpec((2, TOKEN_BLOCK, w), lambda i: (0, i, 0))
    full = lambda *s: pl.BlockSpec(s, lambda i: (0,) * len(s))
    o1 = jax.ShapeDtypeStruct((t, w), F32)
    o2 = jax.ShapeDtypeStruct((2, t, w), F32)
    return pl.pallas_call(
        functools.partial(_rwkv_pre_body, n_ctx_blocks=n_ctx_blocks),
        out_shape=(o1, o1, o1, o2, o2, o2, o1, o1),
        grid=(nb,),
        in_specs=[tok(RW_COLS), full(1, RW_COLS), full(1, w), full(1, w), full(2, w), full(2, w),
                  full(2, LORA_W, w), full(2, LORA_A, w), full(LORA_G, w), full(2, w), full(w, w)],
        out_specs=(tok(w), tok(w), tok(w), tok2, tok2, tok2, tok(w), tok(w)),
        compiler_params=_cparams(("parallel",), VMEM_LIMIT),
        name="rwkv_pre",
    )(rw, p['mu_shift'].reshape(1, RW_COLS), p['k_k'].reshape(1, w), p['k_a'].reshape(1, w), p['w0'], p['a0'],
      p['w_lora_up'], p['a_lora_up'], p['g_up'], p['r_k'].reshape(2, w), seg)


def _scan_body(tbl_ref, r_ref, kk_ref, v_ref, lw_ref, b_ref, kd_ref, s0_ref, y_ref, sfin_ref, s_ref, *, nsteps):
    d = pl.program_id(0)
    i = pl.program_id(1)
    base = (d * nsteps + i) * 4
    c = SCAN_CHUNK
    hd = RW_HEAD

    @pl.when(tbl_ref[base + 2] == 1)
    def _():
        s_ref[...] = s0_ref[...]

    rowi = lax.broadcasted_iota(jnp.int32, (c, c), 0)
    coli = lax.broadcasted_iota(jnp.int32, (c, c), 1)
    diff = jnp.where(d == 1, coli - rowi, rowi - coli)
    strict = diff > 0
    incl = diff >= 0
    same_blk = (rowi // INV_BLOCK) == (coli // INV_BLOCK)
    eye = (rowi == coli).astype(F32)

    lw = lw_ref[...]
    hi, mid, lo = _split3(lw)
    tri = incl.astype(BF16)
    dd = functools.partial(jnp.dot, preferred_element_type=F32)
    cum = dd(tri, hi) + (dd(tri, mid) + dd(tri, lo))
    tot = jnp.sum(lw, axis=0, keepdims=True)
    e_cum = jnp.exp(cum)
    e_neg = jnp.exp(-cum)
    e_rem = jnp.exp(tot - cum)
    g_tot = jnp.exp(tot)
    kk = kk_ref[...]
    bb = b_ref[...]
    kd = kd_ref[...]
    alpha = kk * jnp.exp(cum - lw)
    beta = bb * e_neg
    kappa = kd * e_neg
    rho = r_ref[...] * e_cum
    kap_g = kd * e_rem
    bet_g = bb * e_rem
    v = v_ref[...]

    def heads(a):
        return jnp.stack([a[:, h * hd:(h + 1) * hd] for h in range(RW_HEADS)], axis=0)

    a_h, b_h, k_h, r_h, v_h = heads(alpha), heads(beta), heads(kappa), heads(rho), heads(v)
    ar = jnp.concatenate([a_h, r_h], axis=1)
    gb = _bmm_nt(ar, b_h)
    gk = _bmm_nt(ar, k_h)
    low = jnp.where(strict, gb[:, :c], 0.0)
    a_ka = jnp.where(strict, gk[:, :c], 0.0)
    a_br = jnp.where(incl, gb[:, c:], 0.0)
    a_kr = jnp.where(incl, gk[:, c:], 0.0)
    nd = jnp.where(same_blk, -low, 0.0)
    loff = jnp.where(same_blk, 0.0, low)
    x = eye + nd
    n2 = _bmm(nd, nd)
    x = x + _bmm(x, n2)
    n4 = _bmm(n2, n2)
    x = x + _bmm(x, n4)
    n8 = _bmm(n4, n4)
    x = x + _bmm(x, n8)
    m = _bmm(x, loff)
    m2 = _bmm(m, m)
    y1 = x + _bmm(m2, x)
    tinv = y1 - _bmm(m, y1)
    w_h = _bmm(a_ka, v_h)
    rhs = jnp.concatenate([a_h, w_h], axis=2)
    x0 = _bmm(tinv, rhs)
    res = rhs - x0 - _bmm(low, x0)
    xs = x0 + _bmm(tinv, res)
    p_h = xs[:, :, :hd]
    q_h = xs[:, :, hd:]
    s_old = s_ref[...]
    uy = _bmm_nt(jnp.concatenate([p_h, r_h], axis=1), s_old)
    u_h = uy[:, :c] + q_h
    y_h = uy[:, c:] + _bmm(a_kr, v_h) - _bmm(a_br, u_h)
    zv = jnp.concatenate([v_h, u_h], axis=1)
    zk = jnp.concatenate([heads(kap_g), -heads(bet_g)], axis=1)
    s_ref[...] = s_old * heads(g_tot) + _bmm_tn(zv, zk)
    for h in range(RW_HEADS):
        y_ref[:, h * hd:(h + 1) * hd] = y_h[h]

    @pl.when(tbl_ref[base + 3] == 1)
    def _():
        sfin_ref[...] = s_ref[...]


def _scan_call(tbl, nsteps, nseq, r, kk, v, lw, b, kd, s0):
    t = r.shape[0]
    w = RW_WIDTH
    c = SCAN_CHUNK

    def rb(d, i, tb):
        return tb[(d * nsteps + i) * 4]

    def sq(d, i, tb):
        return tb[(d * nsteps + i) * 4 + 1]

    shared = pl.BlockSpec((c, w), lambda d, i, tb: (rb(d, i, tb), 0))
    perdir = pl.BlockSpec((None, c, w), lambda d, i, tb: (d, rb(d, i, tb), 0))
    st = pl.BlockSpec((None, None, RW_HEADS, RW_HEAD, RW_HEAD), lambda d, i, tb: (sq(d, i, tb), d, 0, 0, 0))
    gs = pltpu.PrefetchScalarGridSpec(
        num_scalar_prefetch=1,
        grid=(2, nsteps),
        in_specs=[shared, shared, shared, perdir, perdir, perdir, st],
        out_specs=(perdir, st),
        scratch_shapes=[pltpu.VMEM((RW_HEADS, RW_HEAD, RW_HEAD), F32)],
    )
    return pl.pallas_call(
        functools.partial(_scan_body, nsteps=nsteps),
        out_shape=(jax.ShapeDtypeStruct((2, t, w), F32),
                   jax.ShapeDtypeStruct((nseq, 2, RW_HEADS, RW_HEAD, RW_HEAD), F32)),
        grid_spec=gs,
        compiler_params=_cparams(("arbitrary", "arbitrary"), VMEM_LIMIT),
        name="rwkv_scan",
    )(tbl, r, kk, v, lw, b, kd, s0)


def _scan_table(seq_lens):
    c = SCAN_CHUNK
    rows = []
    for d in range(2):
        start = 0
        for s, n in enumerate(seq_lens):
            nc = n // c
            for j in range(nc):
                blk = start + (j if d == 0 else nc - 1 - j)
                rows.append((blk, s, int(j == 0), int(j == nc - 1)))
            start += nc
    return np.asarray(rows, np.int32).reshape(-1), len(rows) // 2


def _hy_pre_body(hy_ref, w_ref, b_ref, u_ref, x1_ref, x2_ref, *, n_ctx_blocks):
    x = hy_ref[...]
    prev, nxt = _neighbours(x, n_ctx_blocks)
    cols = prev * w_ref[0:1, :] + x * w_ref[1:2, :] + nxt * w_ref[2:3, :] + b_ref[...]
    w = HY_WIDTH
    u_ref[...] = cols[:, :w]
    x1_ref[...] = cols[:, w:2 * w]
    x2_ref[...] = cols[:, 2 * w:]


def _hy_pre_call(hy, conv_w, conv_b, n_ctx_blocks):
    t = hy.shape[0]
    tok = lambda n: pl.BlockSpec((TOKEN_BLOCK, n), lambda i: (i, 0))
    o = jax.ShapeDtypeStruct((t, HY_WIDTH), F32)
    return pl.pallas_call(
        functools.partial(_hy_pre_body, n_ctx_blocks=n_ctx_blocks),
        out_shape=(o, o, o),
        grid=(t // TOKEN_BLOCK,),
        in_specs=[tok(HY_COLS), pl.BlockSpec((3, HY_COLS), lambda i: (0, 0)),
                  pl.BlockSpec((1, HY_COLS), lambda i: (0, 0))],
        out_specs=(tok(HY_WIDTH),) * 3,
        compiler_params=_cparams(("parallel",)),
        name="hyena_pre",
    )(hy, conv_w, conv_b.reshape(1, HY_COLS))


def _dot_f32(a, b):
    a_hi, a_lo = _split2(a)
    b_hi, b_lo = _split2(b)
    d = functools.partial(jnp.dot, preferred_element_type=F32)
    return d(a_hi, b_hi) + (d(a_hi, b_lo) + d(a_lo, b_hi))


def _hy_filter_body(bands_ref, w1t_ref, w1c_ref, w1s_ref, b1_ref, w2_ref, b2_ref, fr_ref, w3_ref, dl_ref, h_ref,
                    *, n, tile):
    pos = (lax.broadcasted_iota(jnp.int32, (tile, LANES), 0) + pl.program_id(0) * tile).astype(F32)
    tcol = pos / n
    ang = (2.0 * math.pi / n) * pos * bands_ref[...]
    pre1 = (tcol[:, :HY_FFN] * w1t_ref[...] + _dot_f32(jnp.cos(ang), w1c_ref[...])
            + _dot_f32(jnp.sin(ang), w1s_ref[...]) + b1_ref[...])
    fr = fr_ref[...]
    h1 = jnp.sin(fr * pre1)
    h2 = jnp.sin(fr * (_dot_f32(h1, w2_ref[...]) + b2_ref[...]))
    h = _dot_f32(h2, w3_ref[...])
    for j in range(HY_ORDER * 2):
        for q in range(HY_WIDTH // LANES):
            lo = j * HY_WIDTH + q * LANES
            win = jnp.exp(-tcol * dl_ref[:, q * LANES:(q + 1) * LANES])
            h_ref[:, lo:lo + LANES] = h[:, lo:lo + LANES] * win


def _hy_filter_call(n, p):
    bands = jnp.zeros((1, LANES), F32).at[0, :HY_BANDS].set(
        jnp.linspace(1e-4, HY_BANDS - 1, HY_BANDS, dtype=F32))
    max_decay = math.log(HY_DECAY_TARGET) / HY_DECAY_SHORT_PCT
    min_decay = math.log(HY_DECAY_TARGET) / HY_DECAY_LONG_PCT
    deltas = jnp.abs(jnp.linspace(min_decay, max_decay, HY_WIDTH, dtype=F32)).reshape(1, HY_WIDTH)
    w1 = p['hy_f_w1']
    w1t = w1[0:1]
    w1c = jnp.zeros((LANES, HY_FFN), F32).at[:HY_BANDS].set(w1[1:1 + HY_BANDS])
    w1s = jnp.zeros((LANES, HY_FFN), F32).at[:HY_BANDS].set(w1[1 + HY_BANDS:HY_EMB])
    nout = HY_ORDER * 2 * HY_WIDTH
    tile = min(n, 512)
    full = lambda *s: pl.BlockSpec(s, lambda i: (0,) * len(s))
    return pl.pallas_call(
        functools.partial(_hy_filter_body, n=n, tile=tile),
        out_shape=jax.ShapeDtypeStruct((n, nout), F32),
        grid=(n // tile,),
        in_specs=[full(1, LANES), full(1, HY_FFN), full(LANES, HY_FFN), full(LANES, HY_FFN), full(1, HY_FFN),
                  full(HY_FFN, HY_FFN), full(1, HY_FFN), full(1, HY_FFN), full(HY_FFN, nout), full(1, HY_WIDTH)],
        out_specs=pl.BlockSpec((tile, nout), lambda i: (i, 0)),
        compiler_params=_cparams(("parallel",), VMEM_LIMIT),
        name="hyena_filter",
    )(bands, w1t, w1c, w1s, p['hy_f_b1'].reshape(1, HY_FFN), p['hy_f_w2'], p['hy_f_b2'].reshape(1, HY_FFN),
      p['hy_f_freq'].reshape(1, HY_FFN), p['hy_f_w3'], deltas)


def _dft_body(*refs, pre, post, two, half_in, half_mid):
    it = iter(refs)
    x = next(it)[...]
    if pre:
        m1 = next(it)[...]
        m2 = next(it)[...]
        x = x * m1 + pltpu.roll(x, half_in, 1) * m2
    g1h = next(it)[...]
    g1l = next(it)[...]
    y = _dot3(x, g1h, g1l)
    if post:
        m1 = next(it)[...]
        m2 = next(it)[...]
        y = y * m1 + pltpu.roll(y, half_mid, 1) * m2
    if two:
        g2h = next(it)[...]
        g2l = next(it)[...]
        y = _dot3(y, g2h, g2l)
    o_ref = next(it)
    o_ref[...] = y


def _dft_call(x, g1, tile, pre=None, post=None, g2=None, name="hyena_dft"):
    bsz, rows, k = x.shape
    args = [x]
    specs = [pl.BlockSpec((None, tile, k), lambda b, j: (b, j, 0))]

    def add_mul(mm):
        for a in mm:
            nblk = a.shape[0] // tile
            specs.append(pl.BlockSpec((tile, a.shape[1]), lambda b, j, nblk=nblk: (j % nblk, 0)))
            args.append(a)

    def add_mat(g):
        for a in g:
            specs.append(pl.BlockSpec(a.shape, lambda b, j: (0, 0)))
            args.append(a)

    if pre is not None:
        add_mul(pre)
    add_mat(g1)
    if post is not None:
        add_mul(post)
    if g2 is not None:
        add_mat(g2)
    mid = g1[0].shape[1]
    nout = g2[0].shape[1] if g2 is not None else mid
    body = functools.partial(_dft_body, pre=pre is not None, post=post is not None, two=g2 is not None,
                             half_in=k // 2, half_mid=mid // 2)
    return pl.pallas_call(
        body,
        out_shape=jax.ShapeDtypeStruct((bsz, rows, nout), F32),
        grid=(bsz, rows // tile),
        in_specs=specs,
        out_specs=pl.BlockSpec((None, tile, nout), lambda b, j: (b, j, 0)),
        compiler_params=_cparams(("parallel", "parallel"), VMEM_LIMIT),
        name=name,
    )(*args)


def _hilo(a):
    a = np.asarray(a, np.float64)
    hi = jnp.asarray(a, F32).astype(BF16)
    lo = (jnp.asarray(a, F32) - hi.astype(F32)).astype(BF16)
    return hi, lo


def _cplx_block(f):
    return np.block([[f.real, f.imag], [-f.imag, f.real]])


@functools.lru_cache(maxsize=None)
def _dft_plan(n):
    big = 2 * n
    if big <= 512:
        t = np.arange(n)[:, None]
        k = np.arange(big)[None, :]
        fwd = np.exp(-2j * np.pi * t * k / big)
        fwd_full = np.exp(-2j * np.pi * np.arange(big)[:, None] * k / big)
        inv = np.exp(2j * np.pi * np.arange(big)[:, None] * np.arange(n)[None, :] / big) / big
        return dict(stages=1,
                    fwd=np.concatenate([fwd.real, fwd.imag], axis=1),
                    fwd_full=np.concatenate([fwd_full.real, fwd_full.imag], axis=1),
                    inv=np.concatenate([inv.real, -inv.imag], axis=0))
    n1, n2 = DFT_N1, big // DFT_N1
    k1 = np.arange(n1)[None, :, None]
    t = n2 * np.arange(n1)[None, None, :] + np.arange(n2)[:, None, None]
    ga = np.exp(-2j * np.pi * k1 * t / big)
    ga = np.concatenate([ga.real, ga.imag], axis=1)
    fb = np.exp(-2j * np.pi * np.arange(n2)[:, None] * np.arange(n2)[None, :] / n2)
    fbc = np.block([[fb.real, -fb.imag], [fb.imag, fb.real]])
    fbi = np.conj(fb) / big
    fbic = np.block([[fbi.real, -fbi.imag], [fbi.imag, fbi.real]])
    return dict(stages=2, n1=n1, n2=n2, ga=ga, fb=fbc, fb_inv=fbic)


def _long_conv_setup(n, h):
    c = HY_WIDTH
    plan = _dft_plan(n)
    hh = h.reshape(n, HY_ORDER, 2, c)
    f = jnp.concatenate([hh[:, :, 0], jnp.zeros((1, HY_ORDER, c), F32), hh[:0:-1, :, 1]], axis=0)
    if plan['stages'] == 1:
        xf = jnp.transpose(f, (1, 2, 0))
        spec = _dft_call(xf, _hilo(plan['fwd_full']), tile=min(c, 256), name="hyena_filter_dft")
        fr, fi = spec[..., :2 * n], spec[..., 2 * n:]
        return [(jnp.concatenate([fr[o], fr[o]], axis=1), jnp.concatenate([-fi[o], fi[o]], axis=1))
                for o in range(HY_ORDER)]
    return _fused_filter_call(jnp.transpose(f, (1, 0, 2)), plan)


def _long_conv(z, hmul):
    bsz, n, c = z.shape
    plan = _dft_plan(n)
    x = jnp.transpose(z, (0, 2, 1))
    y = _dft_call(x, _hilo(plan['fwd']), tile=min(c, 256), post=hmul, g2=_hilo(plan['inv']),
                  name="hyena_conv_short")
    return jnp.transpose(y, (0, 2, 1))


def _plane_pitch(n2):
    return n2 if (n2 // 8) % 2 == 1 else n2 + 8


def _dot3c(g_hi, g_lo, x):
    x_hi, x_lo = _split2(x)
    d = functools.partial(jnp.dot, preferred_element_type=F32)
    return d(g_hi, x_hi) + (d(g_hi, x_lo) + d(g_lo, x_hi))


def _dot3c_tn(g_hi, g_lo, x):
    x_hi, x_lo = _split2(x)
    d = lambda a, b: lax.dot_general(a, b, (((0,), (0,)), ((), ())), preferred_element_type=F32)
    return d(g_hi, x_hi) + (d(g_hi, x_lo) + d(g_lo, x_hi))


def _stage_one(x_ref, g_hi_ref, g_lo_ref, asc_ref, *, n2, nt1, n1, pitch):
    def body(t2, c):
        xs = x_ref[pl.ds(t2, nt1, stride=n2), :]
        asc_ref[pl.ds(t2, 2 * n1, stride=pitch), :] = _dot3c(g_hi_ref[t2], g_lo_ref[t2], xs)
        return c
    lax.fori_loop(0, n2, body, 0, unroll=2)


def _plane(asc_ref, k1, n1, n2, pitch):
    o_re = pl.multiple_of(k1 * pitch, 8)
    o_im = pl.multiple_of((n1 + k1) * pitch, 8)
    return o_re, o_im, jnp.concatenate([asc_ref[pl.ds(o_re, n2), :], asc_ref[pl.ds(o_im, n2), :]], axis=0)


def _fused_filter_body(f_ref, g_hi_ref, g_lo_ref, fb_hi_ref, fb_lo_ref, h_ref, asc_ref, *, n1, n2, pitch):
    _stage_one(f_ref, g_hi_ref, g_lo_ref, asc_ref, n2=n2, nt1=n1, n1=n1, pitch=pitch)

    def body(k1, c):
        _, _, ain = _plane(asc_ref, k1, n1, n2, pitch)
        h_ref[k1] = _dot3c(fb_hi_ref[...], fb_lo_ref[...], ain)
        return c
    lax.fori_loop(0, n1, body, 0, unroll=2)


def _fused_filter_call(f, plan):
    order, big, c = f.shape
    n1, n2 = plan['n1'], plan['n2']
    pitch = _plane_pitch(n2)
    g_hi, g_lo = _hilo(plan['ga'])
    fb_hi, fb_lo = _hilo(plan['fb'])
    const = lambda a: pl.BlockSpec(a.shape, lambda o, j: (0,) * a.ndim)
    return pl.pallas_call(
        functools.partial(_fused_filter_body, n1=n1, n2=n2, pitch=pitch),
        out_shape=jax.ShapeDtypeStruct((order, c // LANES, n1, 2 * n2, LANES), F32),
        grid=(order, c // LANES),
        in_specs=[pl.BlockSpec((None, big, LANES), lambda o, j: (o, 0, j)),
                  const(g_hi), const(g_lo), const(fb_hi), const(fb_lo)],
        out_specs=pl.BlockSpec((None, None, n1, 2 * n2, LANES), lambda o, j: (o, j, 0, 0, 0)),
        scratch_shapes=[pltpu.VMEM((2 * n1 * pitch, LANES), F32)],
        compiler_params=_cparams(("parallel", "parallel"), VMEM_LIMIT),
        name="hyena_filter_spectrum",
    )(f, g_hi, g_lo, fb_hi, fb_lo)


def _fused_conv_body(x_ref, xg_ref, skip_ref, h_ref, g_hi_ref, g_lo_ref, fb_hi_ref, fb_lo_ref, fbi_hi_ref,
                     fbi_lo_ref, o_ref, asc_ref, y_ref, *, n1, n2, pitch):
    h1 = n1 // 2
    _stage_one(x_ref, g_hi_ref, g_lo_ref, asc_ref, n2=n2, nt1=h1, n1=n1, pitch=pitch)

    def mid(k1, c):
        o_re, o_im, ain = _plane(asc_ref, k1, n1, n2, pitch)
        b = _dot3c(fb_hi_ref[...], fb_lo_ref[...], ain)
        h = h_ref[k1]
        br, bi, hr, hi = b[:n2], b[n2:], h[:n2], h[n2:]
        z = jnp.concatenate([br * hr - bi * hi, br * hi + bi * hr], axis=0)
        cc = _dot3c(fbi_hi_ref[...], fbi_lo_ref[...], z)
        asc_ref[pl.ds(o_re, n2), :] = cc[:n2]
        asc_ref[pl.ds(o_im, n2), :] = cc[n2:]
        return c
    lax.fori_loop(0, n1, mid, 0, unroll=2)

    def last(t2, c):
        zin = asc_ref[pl.ds(t2, 2 * n1, stride=pitch), :]
        y_ref[pl.ds(t2, h1, stride=n2), :] = _dot3c_tn(g_hi_ref[t2], g_lo_ref[t2], zin)
        return c
    lax.fori_loop(0, n2, last, 0, unroll=2)
    x = x_ref[...]
    o_ref[...] = xg_ref[...] * (y_ref[...] + x * skip_ref[...])


def _fused_conv_call(x, xg, skip, hspec, plan, row0_x, row0_g, bsz, n):
    assert row0_x % n == 0 and row0_g % n == 0
    c = x.shape[1]
    n1, n2 = plan['n1'], plan['n2']
    h1 = n1 // 2
    pitch = _plane_pitch(n2)
    g_hi, g_lo = _hilo(plan['ga'][:, :, :h1])
    fb_hi, fb_lo = _hilo(plan['fb'])
    fbi_hi, fbi_lo = _hilo(plan['fb_inv'])
    const = lambda a: pl.BlockSpec(a.shape, lambda j, b: (0,) * a.ndim)
    seq = lambda row0: pl.BlockSpec((n, LANES), lambda j, b: (row0 // n + b, j))
    return pl.pallas_call(
        functools.partial(_fused_conv_body, n1=n1, n2=n2, pitch=pitch),
        out_shape=jax.ShapeDtypeStruct((bsz * n, c), F32),
        grid=(c // LANES, bsz),
        in_specs=[seq(row0_x), seq(row0_g), pl.BlockSpec((1, LANES), lambda j, b: (0, j)),
                  pl.BlockSpec((None, n1, 2 * n2, LANES), lambda j, b: (j, 0, 0, 0)),
                  const(g_hi), const(g_lo), const(fb_hi), const(fb_lo), const(fbi_hi), const(fbi_lo)],
        out_specs=pl.BlockSpec((n, LANES), lambda j, b: (b, j)),
        scratch_shapes=[pltpu.VMEM((2 * n1 * pitch, LANES), F32), pltpu.VMEM((n, LANES), F32)],
        compiler_params=_cparams(("parallel", "parallel"), VMEM_LIMIT),
        name="hyena_conv_long",
    )(x, xg, skip.reshape(1, c), hspec, g_hi, g_lo, fb_hi, fb_lo, fbi_hi, fbi_lo)


def _hy_gate_body(x_ref, y_ref, z_ref, s_ref, o_ref):
    o_ref[...] = x_ref[...] * (y_ref[...] + z_ref[...] * s_ref[...])


def _hy_gate_call(x, y, z, skip):
    t, c = x.shape
    tok = pl.BlockSpec((512, c), lambda i: (i, 0))
    return pl.pallas_call(
        _hy_gate_body,
        out_shape=jax.ShapeDtypeStruct((t, c), F32),
        grid=(t // 512,),
        in_specs=[tok, tok, tok, pl.BlockSpec((1, c), lambda i: (0, 0))],
        out_specs=tok,
        compiler_params=_cparams(("parallel",)),
        name="hyena_gate",
    )(x, y, z, skip.reshape(1, c))


def _hyena_branch(hy, p, groups, n_ctx_blocks):
    u, x1, x2 = _hy_pre_call(hy, p['hy_conv_w'], p['hy_conv_b'], n_ctx_blocks)
    outs = []
    for start, bsz, n in groups:
        rows = bsz * n
        plan = _dft_plan(n)
        hmul = _long_conv_setup(n, _hy_filter_call(n, p))
        if plan['stages'] == 1:
            ug, x1g, x2g = (a[start:start + rows] for a in (u, x1, x2))
            y = _long_conv(ug.reshape(bsz, n, HY_WIDTH), hmul[0]).reshape(rows, HY_WIDTH)
            z = _hy_gate_call(x1g, y, ug, p['hy_skip'][0])
            y = _long_conv(z.reshape(bsz, n, HY_WIDTH), hmul[1]).reshape(rows, HY_WIDTH)
            outs.append(_hy_gate_call(x2g, y, z, p['hy_skip'][1]))
        else:
            z = _fused_conv_call(u, x1, p['hy_skip'][0], hmul[0], plan, start, start, bsz, n)
            outs.append(_fused_conv_call(z, x2, p['hy_skip'][1], hmul[1], plan, 0, start, bsz, n))
    return jnp.concatenate(outs, axis=0)


def _merge_body(x_ref, y_ref, bonus_ref, g_ref, yb_ref, gt_ref, g1_ref, sh2_ref, sc2_ref, seg_ref, gng_ref,
                gnb_ref, wpa_ref, wpb_ref, wout_ref, lng_ref, lnb_ref, rwh_ref, rwl_ref, rb_ref,
                x1_ref, h2_ref, ti_ref, tg_ref, *, alpha):
    seg = seg_ref[...]
    y = y_ref[0] + y_ref[1]
    inv = 1.0 / RW_HEAD
    mu = _dot_x3(y, seg) * inv
    yc = y - mu
    var = _dot_x3(yc * yc, seg) * inv
    yn = yc * lax.rsqrt(var + GN_EPS) * gng_ref[...] + gnb_ref[...]
    y_a = (yn + bonus_ref[...]) * g_ref[...]
    gt = gt_ref[...]
    merged = gt[:, :D_MODEL] * _bdot(y_a, wpa_ref[...]) + gt[:, D_MODEL:] * _bdot(yb_ref[...], wpb_ref[...])
    mix = _bdot(merged, wout_ref[...])
    x1 = _layer_norm(alpha * x_ref[...] + g1_ref[...] * mix, lng_ref[...], lnb_ref[...])
    x1_ref[...] = x1
    h2 = x1 * (1.0 + sc2_ref[...]) + sh2_ref[...]
    h2_ref[...] = h2
    h_hi, h_lo = _split2(h2)
    d = functools.partial(jnp.dot, preferred_element_type=F32)
    logits = d(h_hi, rwh_ref[...]) + (d(h_hi, rwl_ref[...]) + d(h_lo, rwh_ref[...])) + rb_ref[...]
    lane = lax.broadcasted_iota(jnp.int32, logits.shape, 1)
    neg = jnp.float32(-jnp.inf)
    cur = jnp.where(lane < N_EXPERTS, logits, neg)
    top_i = jnp.zeros(logits.shape, jnp.int32)
    top_e = jnp.zeros(logits.shape, F32)
    den = jnp.zeros((logits.shape[0], 1), F32)
    v0 = None
    for j in range(TOP_K):
        mx = jnp.max(cur, axis=-1, keepdims=True)
        idx = jnp.min(jnp.where(cur == mx, lane, LANES), axis=-1, keepdims=True)
        if j == 0:
            v0 = mx
        e = jnp.exp(mx - v0)
        den = den + e
        top_i = jnp.where(lane == j, idx, top_i)
        top_e = jnp.where(lane == j, e, top_e)
        cur = jnp.where(lane == idx, neg, cur)
    ti_ref[...] = top_i
    tg_ref[...] = top_e / den


def _merge_call(x, y, bonus, g, yb, gates, mod_l, p, seg, cond_idx, alpha):
    t = x.shape[0]
    w = RW_WIDTH
    tok = lambda n: pl.BlockSpec((TOKEN_BLOCK, n), lambda i: (i, 0))
    modspec = lambda j: pl.BlockSpec((None, 1, D_MODEL), lambda i: (cond_idx(i), 0, j))
    full = lambda *s: pl.BlockSpec(s, lambda i: (0,) * len(s))
    rw_pad = jnp.zeros((D_MODEL, LANES), F32).at[:, :N_EXPERTS].set(p['router_w'])
    rw_hi = rw_pad.astype(BF16)
    rw_lo = (rw_pad - rw_hi.astype(F32)).astype(BF16)
    rb = jnp.zeros((1, LANES), F32).at[0, :N_EXPERTS].set(p['router_b'])
    o = jax.ShapeDtypeStruct((t, D_MODEL), F32)
    return pl.pallas_call(
        functools.partial(_merge_body, alpha=alpha),
        out_shape=(o, o, jax.ShapeDtypeStruct((t, LANES), jnp.int32), jax.ShapeDtypeStruct((t, LANES), F32)),
        grid=(t // TOKEN_BLOCK,),
        in_specs=[tok(D_MODEL), pl.BlockSpec((2, TOKEN_BLOCK, w), lambda i: (0, i, 0)), tok(w), tok(w), tok(w),
                  tok(2 * D_MODEL), modspec(2), modspec(3), modspec(4), full(w, w), full(1, w), full(1, w),
                  full(w, D_MODEL), full(w, D_MODEL), full(D_MODEL, D_MODEL), full(1, D_MODEL), full(1, D_MODEL),
                  full(D_MODEL, LANES), full(D_MODEL, LANES), full(1, LANES)],
        out_specs=(tok(D_MODEL), tok(D_MODEL), tok(LANES), tok(LANES)),
        compiler_params=_cparams(("parallel",), VMEM_LIMIT),
        name="merge_ln_router",
    )(x, y, bonus, g, yb, gates, mod_l, mod_l, mod_l, seg, p['gn_g'].reshape(1, w), p['gn_b'].reshape(1, w),
      p['w_pa'].astype(BF16), p['w_pb'].astype(BF16), p['w_out'].astype(BF16),
      p['ln1_g'].reshape(1, D_MODEL), p['ln1_b'].reshape(1, D_MODEL), rw_hi, rw_lo, rb)


DEINT_COLS = 256


def _deint_body(w_ref, p_ref, g_ref, l_ref):
    y = jnp.dot(w_ref[...].astype(BF16), p_ref[...], preferred_element_type=F32)
    half = DEINT_COLS // 2
    g_ref[...] = y[:, :half].astype(BF16)
    l_ref[...] = y[:, half:].astype(BF16)


def _deint_call(w):
    e, k, n2 = w.shape
    half = DEINT_COLS // 2
    sel = np.zeros((DEINT_COLS, DEINT_COLS), np.float32)
    sel[2 * np.arange(half), np.arange(half)] = 1.0
    sel[2 * np.arange(half) + 1, half + np.arange(half)] = 1.0
    o = jax.ShapeDtypeStruct((e, k, n2 // 2), BF16)
    return pl.pallas_call(
        _deint_body,
        out_shape=(o, o),
        grid=(e, n2 // DEINT_COLS),
        in_specs=[pl.BlockSpec((None, k, DEINT_COLS), lambda i, j: (i, 0, j)),
                  pl.BlockSpec((DEINT_COLS, DEINT_COLS), lambda i, j: (0, 0))],
        out_specs=(pl.BlockSpec((None, k, half), lambda i, j: (i, 0, j)),) * 2,
        compiler_params=_cparams(("parallel", "parallel")),
        name="expert_w_split",
    )(w, jnp.asarray(sel, BF16))


def _moe_body(blk_e_ref, n_on_ref, tok_ref, tokn_ref, dst_ref, h_hbm, wg_ref, wl_ref, bg_ref, bl_ref, wdn_ref,
              bdn_ref, y_hbm, xbuf, ybuf, sem_in, sem_out, *, n_real):
    i = pl.program_id(0)
    n_on = n_on_ref[0]
    slot = i % 2

    def gather_start(tref, s):
        def body(r, c):
            pltpu.make_async_copy(h_hbm.at[pl.ds(tref[0, r], 1), :], xbuf.at[s, pl.ds(r, 1), :],
                                  sem_in.at[s]).start()
            return c
        lax.fori_loop(0, MOE_ROWS, body, 0, unroll=8)

    def gather_wait(s):
        pltpu.make_async_copy(h_hbm.at[pl.ds(0, MOE_ROWS), :], xbuf.at[s], sem_in.at[s]).wait()

    def scatter_start(s):
        def body(r, c):
            pltpu.make_async_copy(ybuf.at[s, pl.ds(r, 1), :], y_hbm.at[pl.ds(dst_ref[0, r], 1), :],
                                  sem_out.at[s]).start()
            return c
        lax.fori_loop(0, MOE_ROWS, body, 0, unroll=8)

    def scatter_wait(s):
        pltpu.make_async_copy(ybuf.at[s], y_hbm.at[pl.ds(0, MOE_ROWS), :], sem_out.at[s]).wait()

    @pl.when(i == 0)
    def _():
        ybuf[...] = jnp.zeros_like(ybuf)
        fills = [pltpu.make_async_copy(ybuf.at[s], y_hbm.at[pl.ds(n_real + s * MOE_ROWS, MOE_ROWS), :],
                                       sem_out.at[s]) for s in range(2)]
        for cp in fills:
            cp.start()
        for cp in fills:
            cp.wait()

    @pl.when(jnp.logical_and(i == 0, n_on > 0))
    def _():
        gather_start(tok_ref, 0)

    @pl.when(i < n_on)
    def _():
        @pl.when(i + 1 < n_on)
        def _():
            gather_start(tokn_ref, 1 - slot)

        gather_wait(slot)
        x = xbuf[slot].astype(BF16)
        d = functools.partial(jnp.dot, preferred_element_type=F32)
        glu = jnp.minimum(d(x, wg_ref[...]) + bg_ref[...], SWIGLU_LIMIT)
        lin = jnp.clip(d(x, wl_ref[...]) + bl_ref[...], -SWIGLU_LIMIT, SWIGLU_LIMIT)
        act = glu * _sigmoid(SWIGLU_ALPHA * glu) * (lin + 1.0)
        ybuf[slot] = d(act.astype(BF16), wdn_ref[...]) + bdn_ref[...]
        scatter_start(slot)

        @pl.when(i >= 1)
        def _():
            scatter_wait(1 - slot)

        @pl.when(i == n_on - 1)
        def _():
            scatter_wait(slot)


def _moe_call(h2, top_i, wg, wl, bg, bl, wdn, bdn):
    t = h2.shape[0]
    m = t * TOP_K
    e = N_EXPERTS
    blk = MOE_ROWS
    flat_e = top_i.reshape(-1)
    order = jnp.argsort(flat_e, stable=True).astype(jnp.int32)
    sizes = jnp.bincount(flat_e, length=e).astype(jnp.int32)
    padded = (sizes + blk - 1) // blk * blk
    pad_end = jnp.cumsum(padded)
    pad_start = pad_end - padded
    grp_start = jnp.cumsum(sizes) - sizes
    n_blocks = -(-(m + e * (blk - 1)) // blk)
    blk_first = jnp.arange(n_blocks, dtype=jnp.int32) * blk
    blk_e = jnp.minimum(jnp.sum((pad_end[None, :] <= blk_first[:, None]).astype(jnp.int32), axis=1), e - 1)
    pidx = jnp.arange(n_blocks * blk, dtype=jnp.int32)
    e_p = jnp.repeat(blk_e, blk)
    idx = pidx - pad_start[e_p]
    valid = idx < sizes[e_p]
    assign = order[jnp.clip(grp_start[e_p] + idx, 0, m - 1)]
    tok_row = jnp.where(valid, assign // TOP_K, 0).astype(jnp.int32)
    spare = m + ((pidx // blk) % 2) * blk + pidx % blk
    dst_row = jnp.where(valid, (assign % TOP_K) * t + assign // TOP_K, spare).astype(jnp.int32)
    n_on = (pad_end[-1] // blk).astype(jnp.int32).reshape(1)
    tok3 = tok_row.reshape(n_blocks, 1, blk)

    smem = lambda f: pl.BlockSpec((None, 1, blk), f, memory_space=pltpu.SMEM)
    wspec = lambda a, b: pl.BlockSpec((None, a, b), lambda i, be, no: (be[i], 0, 0))
    gs = pltpu.PrefetchScalarGridSpec(
        num_scalar_prefetch=2,
        grid=(n_blocks,),
        in_specs=[smem(lambda i, be, no: (i, 0, 0)),
                  smem(lambda i, be, no: (jnp.minimum(i + 1, n_blocks - 1), 0, 0)),
                  smem(lambda i, be, no: (i, 0, 0)),
                  pl.BlockSpec(memory_space=pl.ANY),
                  wspec(D_MODEL, D_FF), wspec(D_MODEL, D_FF), wspec(1, D_FF), wspec(1, D_FF),
                  wspec(D_FF, D_MODEL), wspec(1, D_MODEL)],
        out_specs=pl.BlockSpec(memory_space=pl.ANY),
        scratch_shapes=[pltpu.VMEM((2, blk, D_MODEL), F32), pltpu.VMEM((2, blk, D_MODEL), F32),
                        pltpu.SemaphoreType.DMA((2,)), pltpu.SemaphoreType.DMA((2,))],
    )
    return pl.pallas_call(
        functools.partial(_moe_body, n_real=m),
        out_shape=jax.ShapeDtypeStruct((m + 2 * blk, D_MODEL), F32),
        grid_spec=gs,
        compiler_params=_cparams(("arbitrary",), VMEM_LIMIT),
        name="moe_experts",
    )(blk_e, n_on, tok3, tok3, dst_row.reshape(n_blocks, 1, blk), h2, wg, wl, bg.reshape(e, 1, D_FF),
      bl.reshape(e, 1, D_FF), wdn, bdn.reshape(e, 1, D_MODEL))


def _combine_body(x1_ref, y0_ref, y1_ref, y2_ref, y3_ref, tg_ref, g2_ref, lng_ref, lnb_ref, o_ref, *, alpha):
    tg = tg_ref[...]
    moe = tg[:, 0:1] * y0_ref[...]
    for j, y_ref in enumerate((y1_ref, y2_ref, y3_ref), start=1):
        moe = moe + tg[:, j:j + 1] * y_ref[...]
    o_ref[...] = _layer_norm(alpha * x1_ref[...] + g2_ref[...] * moe, lng_ref[...], lnb_ref[...])


def _combine_call(x1, yexp, tg, mod_l, p, cond_idx, alpha):
    t = x1.shape[0]
    nb = t // TOKEN_BLOCK
    tok = lambda n: pl.BlockSpec((TOKEN_BLOCK, n), lambda i: (i, 0))
    full = lambda *s: pl.BlockSpec(s, lambda i: (0,) * len(s))
    yspec = [pl.BlockSpec((TOKEN_BLOCK, D_MODEL), lambda i, j=j: (j * nb + i, 0)) for j in range(TOP_K)]
    return pl.pallas_call(
        functools.partial(_combine_body, alpha=alpha),
        out_shape=jax.ShapeDtypeStruct((t, D_MODEL), F32),
        grid=(nb,),
        in_specs=[tok(D_MODEL)] + yspec + [tok(LANES),
                  pl.BlockSpec((None, 1, D_MODEL), lambda i: (cond_idx(i), 0, 5)),
                  full(1, D_MODEL), full(1, D_MODEL)],
        out_specs=tok(D_MODEL),
        compiler_params=_cparams(("parallel",), VMEM_LIMIT),
        name="moe_combine_ln",
    )(x1, yexp, yexp, yexp, yexp, tg, mod_l, p['ln2_g'].reshape(1, D_MODEL), p['ln2_b'].reshape(1, D_MODEL))


def kernel(x_prompt, x_sample, c, state_rwkv, c_ctx, w_mod, b_mod, w_in, mu_shift, w0, w_lora_up, a0, a_lora_up, g_up, k_k, k_a, r_k, gn_g, gn_b, hy_conv_w, hy_conv_b, hy_f_w1, hy_f_b1, hy_f_w2, hy_f_b2, hy_f_freq, hy_f_w3, hy_skip, w_pa, w_pb, w_out, ln1_g, ln1_b, ln2_g, ln2_b, router_w, router_b, ex_w_up, ex_b_up, ex_w_down, ex_b_down):
    bsz, seq, dm = x_prompt.shape
    dbsz, dseq, _ = x_sample.shape
    depth = w_mod.shape[0]
    assert dm == D_MODEL and seq == TOKEN_BLOCK and dseq % TOKEN_BLOCK == 0 and TOKEN_BLOCK % GRID_W == 0
    assert 1 + dbsz <= 8
    alpha = (2 * depth) ** 0.25
    t_ctx = bsz * seq
    n_ctx_blocks = t_ctx // TOKEN_BLOCK
    lat_blocks = dseq // TOKEN_BLOCK

    def cond_idx(i):
        return jnp.where(i < n_ctx_blocks, 0, 1 + (i - n_ctx_blocks) // lat_blocks)

    x = jnp.concatenate([x_prompt.reshape(t_ctx, dm), x_sample.reshape(dbsz * dseq, dm)], axis=0)
    cond8 = jnp.zeros((8, dm), F32).at[0].set(c_ctx).at[1:1 + dbsz].set(c)
    mod = _mod_call(cond8, w_mod, b_mod)

    hd = RW_HEAD
    seg = (np.arange(RW_WIDTH)[:, None] // hd == np.arange(RW_WIDTH)[None, :] // hd)
    seg = jnp.asarray(seg, BF16)
    seq_lens = [seq] * bsz + [dseq] * dbsz
    tbl_np, nsteps = _scan_table(seq_lens)
    tbl = jnp.asarray(tbl_np)
    groups = [(0, bsz, seq), (t_ctx, dbsz, dseq)]

    new_states = []
    for l in range(depth):
        p = dict(mu_shift=mu_shift[l], w0=w0[l], w_lora_up=w_lora_up[l], a0=a0[l], a_lora_up=a_lora_up[l],
                 g_up=g_up[l], k_k=k_k[l], k_a=k_a[l], r_k=r_k[l], gn_g=gn_g[l], gn_b=gn_b[l],
                 hy_conv_w=hy_conv_w[l], hy_conv_b=hy_conv_b[l], hy_f_w1=hy_f_w1[l], hy_f_b1=hy_f_b1[l],
                 hy_f_w2=hy_f_w2[l], hy_f_b2=hy_f_b2[l], hy_f_freq=hy_f_freq[l], hy_f_w3=hy_f_w3[l],
                 hy_skip=hy_skip[l], w_pa=w_pa[l], w_pb=w_pb[l], w_out=w_out[l], ln1_g=ln1_g[l], ln1_b=ln1_b[l],
                 ln2_g=ln2_g[l], ln2_b=ln2_b[l], router_w=router_w[l], router_b=router_b[l])
        mod_l = mod[l].reshape(8, 1, 6 * dm)
        rw, hy, gates = _inproj_call(x, mod_l, w_in[l].astype(BF16), cond_idx)
        r, kk, v, lw, b, kd, g, bonus = _rwkv_pre_call(rw, p, seg, n_ctx_blocks)
        s0 = jnp.concatenate([jnp.zeros((bsz, 2, RW_HEADS, hd, hd), F32), state_rwkv[:, l].astype(F32)], axis=0)
        y, sfin = _scan_call(tbl, nsteps, bsz + dbsz, r, kk, v, lw, b, kd, s0)
        new_states.append(sfin[:bsz].astype(x_prompt.dtype))
        yb = _hyena_branch(hy, p, groups, n_ctx_blocks)
        x1, h2, top_i, top_g = _merge_call(x, y, bonus, g, yb, gates, mod_l, p, seg, cond_idx, alpha)
        wg, wl = _deint_call(ex_w_up[l])
        yexp = _moe_call(h2, top_i[:, :TOP_K], wg, wl, ex_b_up[l][:, 0::2], ex_b_up[l][:, 1::2],
                         ex_w_down[l].astype(BF16), ex_b_down[l])
        x = _combine_call(x1, yexp, top_g, mod_l, p, cond_idx, alpha)

    y_p = x[:t_ctx].reshape(bsz, seq, dm)
    y_s = x[t_ctx:].reshape(dbsz, dseq, dm)
    return (y_p, y_s, jnp.stack(new_states, axis=1))
```

```python
import functools
import math

import numpy as np
import jax
import jax.numpy as jnp
from jax import lax
from jax.experimental import pallas as pl
from jax.experimental.pallas import tpu as pltpu

F32 = jnp.float32
BF16 = jnp.bfloat16

D_MODEL = 1024
GRID_W = 64
RW_WIDTH = 512
RW_HEAD = 64
RW_HEADS = RW_WIDTH // RW_HEAD
LORA_W = 64
LORA_A = 64
LORA_G = 128
GN_EPS = 64e-5
HY_WIDTH = 512
HY_ORDER = 2
HY_BANDS = 16
HY_EMB = 2 * HY_BANDS + 1
HY_FFN = 64
HY_DECAY_TARGET = 1e-2
HY_DECAY_SHORT_PCT = 0.3
HY_DECAY_LONG_PCT = 1.5
RW_COLS = 3 * RW_WIDTH + LORA_W + LORA_A + LORA_G
HY_COLS = (HY_ORDER + 1) * HY_WIDTH
IN_COLS = RW_COLS + HY_COLS + 2 * D_MODEL
N_EXPERTS = 32
TOP_K = 4
D_FF = D_MODEL
SWIGLU_ALPHA = 1.702
SWIGLU_LIMIT = 7.0
LN_EPS = 1e-5

TOKEN_BLOCK = 256
SCAN_CHUNK = 64
INV_BLOCK = 16
DFT_N1 = 128
MOE_ROWS = 256
LANES = 128
VMEM_LIMIT = 56 * 1024 * 1024


def _cparams(sem, vmem=None):
    return pltpu.CompilerParams(dimension_semantics=sem, vmem_limit_bytes=vmem)


def _bdot(a, b):
    return jnp.dot(a.astype(BF16), b.astype(BF16), preferred_element_type=F32)


def _bdg(a, b, ca, cb):
    return lax.dot_general(a.astype(BF16), b.astype(BF16), (((ca,), (cb,)), ((0,), (0,))), preferred_element_type=F32)


def _bmm(a, b):
    return _bdg(a, b, 2, 1)


def _bmm_nt(a, b):
    return _bdg(a, b, 2, 2)


def _bmm_tn(a, b):
    return _bdg(a, b, 1, 1)


def _split2(x):
    hi = x.astype(BF16)
    lo = (x - hi.astype(F32)).astype(BF16)
    return hi, lo


def _split3(x):
    hi = x.astype(BF16)
    r1 = x - hi.astype(F32)
    mid = r1.astype(BF16)
    lo = (r1 - mid.astype(F32)).astype(BF16)
    return hi, mid, lo


def _dot_x3(x, g_bf16):
    hi, mid, lo = _split3(x)
    d = functools.partial(jnp.dot, preferred_element_type=F32)
    return d(hi, g_bf16) + (d(mid, g_bf16) + d(lo, g_bf16))


def _dot3(x, g_hi, g_lo):
    x_hi, x_lo = _split2(x)
    d = functools.partial(jnp.dot, preferred_element_type=F32)
    return d(x_hi, g_hi) + (d(x_hi, g_lo) + d(x_lo, g_hi))


def _sigmoid(x):
    return 1.0 / (1.0 + jnp.exp(-x))


def _layer_norm(x, g, b):
    mu = jnp.mean(x, axis=-1, keepdims=True)
    xc = x - mu
    var = jnp.mean(xc * xc, axis=-1, keepdims=True)
    return xc * lax.rsqrt(var + LN_EPS) * g + b


def _mod_body(c_ref, w_ref, b_ref, o_ref):
    c = c_ref[...]
    o_ref[...] = _bdot(c * _sigmoid(c), w_ref[...]) + b_ref[...]


def _mod_call(cond8, w_mod, b_mod):
    depth = w_mod.shape[0]
    tn = 1536
    return pl.pallas_call(
        _mod_body,
        out_shape=jax.ShapeDtypeStruct((depth, 8, 6 * D_MODEL), F32),
        grid=(depth, 6 * D_MODEL // tn),
        in_specs=[pl.BlockSpec((8, D_MODEL), lambda l, j: (0, 0)),
                  pl.BlockSpec((None, D_MODEL, tn), lambda l, j: (l, 0, j)),
                  pl.BlockSpec((None, 1, tn), lambda l, j: (l, 0, j))],
        out_specs=pl.BlockSpec((None, 8, tn), lambda l, j: (l, 0, j)),
        compiler_params=_cparams(("parallel", "parallel"), VMEM_LIMIT),
        name="adaln_mod",
    )(cond8, w_mod, b_mod.reshape(depth, 1, 6 * D_MODEL))


def _inproj_body(x_ref, sh_ref, sc_ref, w_ref, rw_ref, hy_ref, gt_ref):
    h = (x_ref[...] * (1.0 + sc_ref[...]) + sh_ref[...]).astype(BF16)
    d = functools.partial(jnp.dot, preferred_element_type=F32)
    rw_ref[...] = d(h, w_ref[:, :RW_COLS])
    hy_ref[...] = d(h, w_ref[:, RW_COLS:RW_COLS + HY_COLS])
    gt_ref[...] = _sigmoid(d(h, w_ref[:, RW_COLS + HY_COLS:]))


def _inproj_call(x, mod_l, w_in_bf16, cond_idx):
    t = x.shape[0]
    nb = t // TOKEN_BLOCK
    tok = lambda n: pl.BlockSpec((TOKEN_BLOCK, n), lambda i: (i, 0))
    modspec = lambda j: pl.BlockSpec((None, 1, D_MODEL), lambda i: (cond_idx(i), 0, j))
    return pl.pallas_call(
        _inproj_body,
        out_shape=(jax.ShapeDtypeStruct((t, RW_COLS), F32), jax.ShapeDtypeStruct((t, HY_COLS), F32),
                   jax.ShapeDtypeStruct((t, 2 * D_MODEL), F32)),
        grid=(nb,),
        in_specs=[tok(D_MODEL), modspec(0), modspec(1),
                  pl.BlockSpec((D_MODEL, IN_COLS), lambda i: (0, 0))],
        out_specs=(tok(RW_COLS), tok(HY_COLS), tok(2 * D_MODEL)),
        compiler_params=_cparams(("parallel",), VMEM_LIMIT),
        name="in_proj",
    )(x, mod_l, mod_l, w_in_bf16)


def _neighbours(x, n_ctx_blocks):
    rows = x.shape[0]
    row = lax.broadcasted_iota(jnp.int32, (rows, 1), 0)
    seg_mask = jnp.where(pl.program_id(0) < n_ctx_blocks, rows - 1, GRID_W - 1)
    pos = row & seg_mask
    prev = jnp.where(pos == 0, 0.0, pltpu.roll(x, 1, 0))
    nxt = jnp.where(pos == seg_mask, 0.0, pltpu.roll(x, rows - 1, 0))
    return prev, nxt


def _rwkv_pre_body(rw_ref, mu_ref, kk_s_ref, ka_ref, w0_ref, a0_ref, wl_ref, al_ref, gup_ref, rk_ref, seg_ref,
                   r_ref, kk_ref, v_ref, lw_ref, b_ref, kd_ref, g_ref, bonus_ref, *, n_ctx_blocks):
    x = rw_ref[...]
    prev, nxt = _neighbours(x, n_ctx_blocks)
    cols = x + mu_ref[...] * (0.5 * (prev + nxt) - x)
    w = RW_WIDTH
    r = cols[:, :w]
    k = cols[:, w:2 * w]
    v = cols[:, 2 * w:3 * w]
    wd = cols[:, 3 * w:3 * w + LORA_W]
    ad = cols[:, 3 * w + LORA_W:3 * w + LORA_W + LORA_A]
    gd = cols[:, 3 * w + LORA_W + LORA_A:]
    seg = seg_ref[...]
    kkr = k * kk_s_ref[...]
    ss = _dot_x3(kkr * kkr, seg)
    kk = kkr / jnp.maximum(jnp.sqrt(ss), 1e-12)
    r_ref[...] = r
    kk_ref[...] = kk
    v_ref[...] = v
    g_ref[...] = _bdot(_sigmoid(gd), gup_ref[...])
    tw = jnp.tanh(wd)
    bonus = jnp.zeros_like(r)
    for d in range(2):
        w_logit = w0_ref[d:d + 1, :] + _bdot(tw, wl_ref[d])
        lw_ref[d] = -math.exp(-0.5) * _sigmoid(w_logit)
        a = _sigmoid(a0_ref[d:d + 1, :] + _bdot(ad, al_ref[d]))
        kd = k * (1.0 + (a - 1.0) * ka_ref[...])
        kd_ref[d] = kd
        b_ref[d] = kk * a
        bonus = bonus + _dot_x3(r * kd * rk_ref[d:d + 1, :], seg) * v
    bonus_ref[...] = bonus


def _rwkv_pre_call(rw, p, seg, n_ctx_blocks):
    t = rw.shape[0]
    nb = t // TOKEN_BLOCK
    w = RW_WIDTH
    tok = lambda n: pl.BlockSpec((TOKEN_BLOCK, n), lambda i: (i, 0))
    tok2 = pl.BlockSpec((2, TOKEN_BLOCK, w), lambda i: (0, i, 0))
    full = lambda *s: pl.BlockSpec(s, lambda i: (0,) * len(s))
    o1 = jax.ShapeDtypeStruct((t, w), F32)
    o2 = jax.ShapeDtypeStruct((2, t, w), F32)
    return pl.pallas_call(
        functools.partial(_rwkv_pre_body, n_ctx_blocks=n_ctx_blocks),
        out_shape=(o1, o1, o1, o2, o2, o2, o1, o1),
        grid=(nb,),
        in_specs=[tok(RW_COLS), full(1, RW_COLS), full(1, w), full(1, w), full(2, w), full(2, w),
                  full(2, LORA_W, w), full(2, LORA_A, w), full(LORA_G, w), full(2, w), full(w, w)],
        out_specs=(tok(w), tok(w), tok(w), tok2, tok2, tok2, tok(w), tok(w)),
        compiler_params=_cparams(("parallel",), VMEM_LIMIT),
        name="rwkv_pre",
    )(rw, p['mu_shift'].reshape(1, RW_COLS), p['k_k'].reshape(1, w), p['k_a'].reshape(1, w), p['w0'], p['a0'],
      p['w_lora_up'], p['a_lora_up'], p['g_up'], p['r_k'].reshape(2, w), seg)


def _scan_body(tbl_ref, rf_ref, kkf_ref, vf_ref, rb_ref, kkb_ref, vb_ref, lwf_ref, bf_ref, kdf_ref, lwb_ref,
               bb_ref, kdb_ref, s0_ref, yf_ref, yb_ref, sfin_ref, s_ref):
    base = pl.program_id(0) * 5
    c = SCAN_CHUNK
    hd = RW_HEAD
    nh = 2 * RW_HEADS

    @pl.when(tbl_ref[base + 3] == 1)
    def _():
        s_ref[...] = s0_ref[...].reshape(nh, hd, hd)

    rowi = lax.broadcasted_iota(jnp.int32, (c, c), 0)
    coli = lax.broadcasted_iota(jnp.int32, (c, c), 1)
    same_blk = (rowi // INV_BLOCK) == (coli // INV_BLOCK)
    eye = (rowi == coli).astype(F32)

    def both(fwd, bwd):
        return jnp.concatenate([jnp.broadcast_to(fwd, (RW_HEADS, c, c)), jnp.broadcast_to(bwd, (RW_HEADS, c, c))],
                               axis=0)

    diff = both(rowi - coli, coli - rowi)
    strict = diff > 0
    incl = diff >= 0

    def heads(a):
        return jnp.stack([a[:, h * hd:(h + 1) * hd] for h in range(RW_HEADS)], axis=0)

    def prep(r_ref, kk_ref, v_ref, lw_ref, b_ref, kd_ref, tri):
        lw = lw_ref[...]
        hi, mid, lo = _split3(lw)
        dd = functools.partial(jnp.dot, preferred_element_type=F32)
        cum = dd(tri, hi) + (dd(tri, mid) + dd(tri, lo))
        tot = jnp.sum(lw, axis=0, keepdims=True)
        e_neg = jnp.exp(-cum)
        e_rem = jnp.exp(tot - cum)
        kk = kk_ref[...]
        bb = b_ref[...]
        kd = kd_ref[...]
        return [heads(a) for a in (kk * jnp.exp(cum - lw), bb * e_neg, kd * e_neg, r_ref[...] * jnp.exp(cum),
                                   v_ref[...], kd * e_rem, bb * e_rem, jnp.exp(tot))]

    fw = prep(rf_ref, kkf_ref, vf_ref, lwf_ref, bf_ref, kdf_ref, (rowi >= coli).astype(BF16))
    bw = prep(rb_ref, kkb_ref, vb_ref, lwb_ref, bb_ref, kdb_ref, (rowi <= coli).astype(BF16))
    a_h, b_h, k_h, r_h, v_h, kapg_h, betg_h, gtot_h = [jnp.concatenate([f, b], axis=0) for f, b in zip(fw, bw)]
    ar = jnp.concatenate([a_h, r_h], axis=1)
    gb = _bmm_nt(ar, b_h)
    gk = _bmm_nt(ar, k_h)
    low = jnp.where(strict, gb[:, :c], 0.0)
    a_ka = jnp.where(strict, gk[:, :c], 0.0)
    a_br = jnp.where(incl, gb[:, c:], 0.0)
    a_kr = jnp.where(incl, gk[:, c:], 0.0)
    nd = jnp.where(same_blk, -low, 0.0)
    loff = jnp.where(same_blk, 0.0, low)
    x = eye + nd
    n2 = _bmm(nd, nd)
    x = x + _bmm(x, n2)
    n4 = _bmm(n2, n2)
    x = x + _bmm(x, n4)
    n8 = _bmm(n4, n4)
    x = x + _bmm(x, n8)
    m = _bmm(x, loff)
    m2 = _bmm(m, m)
    y1 = x + _bmm(m2, x)
    tinv = y1 - _bmm(m, y1)
    w_h = _bmm(a_ka, v_h)
    rhs = jnp.concatenate([a_h, w_h], axis=2)
    x0 = _bmm(tinv, rhs)
    res = rhs - x0 - _bmm(low, x0)
    xs = x0 + _bmm(tinv, res)
    p_h = xs[:, :, :hd]
    q_h = xs[:, :, hd:]
    s_old = s_ref[...]
    uy = _bmm_nt(jnp.concatenate([p_h, r_h], axis=1), s_old)
    u_h = uy[:, :c] + q_h
    y_h = uy[:, c:] + _bmm(a_kr, v_h) - _bmm(a_br, u_h)
    zv = jnp.concatenate([v_h, u_h], axis=1)
    zk = jnp.concatenate([kapg_h, -betg_h], axis=1)
    s_ref[...] = s_old * gtot_h + _bmm_tn(zv, zk)
    for h in range(RW_HEADS):
        yf_ref[:, h * hd:(h + 1) * hd] = y_h[h]
        yb_ref[:, h * hd:(h + 1) * hd] = y_h[RW_HEADS + h]

    @pl.when(tbl_ref[base + 4] == 1)
    def _():
        sfin_ref[...] = s_ref[...].reshape(2, RW_HEADS, hd, hd)


def _scan_call(tbl, nsteps, nseq, r, kk, v, lw, b, kd, s0):
    t = r.shape[0]
    w = RW_WIDTH
    c = SCAN_CHUNK
    blk = lambda d: pl.BlockSpec((c, w), lambda i, tb: (tb[i * 5 + d], 0))
    blk_dir = lambda d: pl.BlockSpec((None, c, w), lambda i, tb: (d, tb[i * 5 + d], 0))
    st = pl.BlockSpec((None, 2, RW_HEADS, RW_HEAD, RW_HEAD), lambda i, tb: (tb[i * 5 + 2], 0, 0, 0, 0))
    gs = pltpu.PrefetchScalarGridSpec(
        num_scalar_prefetch=1,
        grid=(nsteps,),
        in_specs=[blk(0), blk(0), blk(0), blk(1), blk(1), blk(1), blk_dir(0), blk_dir(0), blk_dir(0),
                  blk_dir(1), blk_dir(1), blk_dir(1), st],
        out_specs=(blk(0), blk(1), st),
        scratch_shapes=[pltpu.VMEM((2 * RW_HEADS, RW_HEAD, RW_HEAD), F32)],
    )
    o = jax.ShapeDtypeStruct((t, w), F32)
    return pl.pallas_call(
        _scan_body,
        out_shape=(o, o, jax.ShapeDtypeStruct((nseq, 2, RW_HEADS, RW_HEAD, RW_HEAD), F32)),
        grid_spec=gs,
        compiler_params=_cparams(("arbitrary",), VMEM_LIMIT),
        name="rwkv_scan",
    )(tbl, r, kk, v, r, kk, v, lw, b, kd, lw, b, kd, s0)


def _scan_table(seq_lens):
    c = SCAN_CHUNK
    rows = []
    start = 0
    for s, n in enumerate(seq_lens):
        nc = n // c
        for j in range(nc):
            rows.append((start + j, start + nc - 1 - j, s, int(j == 0), int(j == nc - 1)))
        start += nc
    return np.asarray(rows, np.int32).reshape(-1), len(rows)


def _hy_pre_body(hy_ref, w_ref, b_ref, u_ref, x1_ref, x2_ref, *, n_ctx_blocks):
    x = hy_ref[...]
    prev, nxt = _neighbours(x, n_ctx_blocks)
    cols = prev * w_ref[0:1, :] + x * w_ref[1:2, :] + nxt * w_ref[2:3, :] + b_ref[...]
    w = HY_WIDTH
    u_ref[...] = cols[:, :w]
    x1_ref[...] = cols[:, w:2 * w]
    x2_ref[...] = cols[:, 2 * w:]


def _hy_pre_call(hy, conv_w, conv_b, n_ctx_blocks):
    t = hy.shape[0]
    tok = lambda n: pl.BlockSpec((TOKEN_BLOCK, n), lambda i: (i, 0))
    o = jax.ShapeDtypeStruct((t, HY_WIDTH), F32)
    return pl.pallas_call(
        functools.partial(_hy_pre_body, n_ctx_blocks=n_ctx_blocks),
        out_shape=(o, o, o),
        grid=(t // TOKEN_BLOCK,),
        in_specs=[tok(HY_COLS), pl.BlockSpec((3, HY_COLS), lambda i: (0, 0)),
                  pl.BlockSpec((1, HY_COLS), lambda i: (0, 0))],
        out_specs=(tok(HY_WIDTH),) * 3,
        compiler_params=_cparams(("parallel",)),
        name="hyena_pre",
    )(hy, conv_w, conv_b.reshape(1, HY_COLS))


def _dot_f32(a, b):
    a_hi, a_lo = _split2(a)
    b_hi, b_lo = _split2(b)
    d = functools.partial(jnp.dot, preferred_element_type=F32)
    return d(a_hi, b_hi) + (d(a_hi, b_lo) + d(a_lo, b_hi))


def _hy_filter_body(bands_ref, w1t_ref, w1c_ref, w1s_ref, b1_ref, w2_ref, b2_ref, fr_ref, w3_ref, dl_ref, f_ref,
                    *, n, tile):
    row = lax.broadcasted_iota(jnp.int32, (tile, LANES), 0) + pl.program_id(0) * tile
    fwd = row < n
    pos = jnp.where(fwd, row, 2 * n - row).astype(F32)
    tcol = pos / n
    ang = (2.0 * math.pi / n) * pos * bands_ref[...]
    pre1 = (tcol[:, :HY_FFN] * w1t_ref[...] + _dot_f32(jnp.cos(ang), w1c_ref[...])
            + _dot_f32(jnp.sin(ang), w1s_ref[...]) + b1_ref[...])
    fr = fr_ref[...]
    h1 = jnp.sin(fr * pre1)
    h2 = jnp.sin(fr * (_dot_f32(h1, w2_ref[...]) + b2_ref[...]))
    h = _dot_f32(h2, w3_ref[...])
    keep = row != n
    for q in range(HY_WIDTH // LANES):
        win = jnp.where(keep, jnp.exp(-tcol * dl_ref[:, q * LANES:(q + 1) * LANES]), 0.0)
        for o in range(HY_ORDER):
            lo = o * 2 * HY_WIDTH + q * LANES
            f_ref[o, :, q * LANES:(q + 1) * LANES] = jnp.where(
                fwd, h[:, lo:lo + LANES], h[:, lo + HY_WIDTH:lo + HY_WIDTH + LANES]) * win


def _hy_filter_call(n, p):
    bands = jnp.zeros((1, LANES), F32).at[0, :HY_BANDS].set(
        jnp.linspace(1e-4, HY_BANDS - 1, HY_BANDS, dtype=F32))
    max_decay = math.log(HY_DECAY_TARGET) / HY_DECAY_SHORT_PCT
    min_decay = math.log(HY_DECAY_TARGET) / HY_DECAY_LONG_PCT
    deltas = jnp.abs(jnp.linspace(min_decay, max_decay, HY_WIDTH, dtype=F32)).reshape(1, HY_WIDTH)
    w1 = p['hy_f_w1']
    w1t = w1[0:1]
    w1c = jnp.zeros((LANES, HY_FFN), F32).at[:HY_BANDS].set(w1[1:1 + HY_BANDS])
    w1s = jnp.zeros((LANES, HY_FFN), F32).at[:HY_BANDS].set(w1[1 + HY_BANDS:HY_EMB])
    nout = HY_ORDER * 2 * HY_WIDTH
    tile = min(2 * n, 512)
    full = lambda *s: pl.BlockSpec(s, lambda i: (0,) * len(s))
    return pl.pallas_call(
        functools.partial(_hy_filter_body, n=n, tile=tile),
        out_shape=jax.ShapeDtypeStruct((HY_ORDER, 2 * n, HY_WIDTH), F32),
        grid=(2 * n // tile,),
        in_specs=[full(1, LANES), full(1, HY_FFN), full(LANES, HY_FFN), full(LANES, HY_FFN), full(1, HY_FFN),
                  full(HY_FFN, HY_FFN), full(1, HY_FFN), full(1, HY_FFN), full(HY_FFN, nout), full(1, HY_WIDTH)],
        out_specs=pl.BlockSpec((HY_ORDER, tile, HY_WIDTH), lambda i: (0, i, 0)),
        compiler_params=_cparams(("parallel",), VMEM_LIMIT),
        name="hyena_filter",
    )(bands, w1t, w1c, w1s, p['hy_f_b1'].reshape(1, HY_FFN), p['hy_f_w2'], p['hy_f_b2'].reshape(1, HY_FFN),
      p['hy_f_freq'].reshape(1, HY_FFN), p['hy_f_w3'], deltas)


def _dft_body(*refs, pre, post, two, half_in, half_mid):
    it = iter(refs)
    x = next(it)[...]
    if pre:
        m1 = next(it)[...]
        m2 = next(it)[...]
        x = x * m1 + pltpu.roll(x, half_in, 1) * m2
    g1h = next(it)[...]
    g1l = next(it)[...]
    y = _dot3(x, g1h, g1l)
    if post:
        m1 = next(it)[...]
        m2 = next(it)[...]
        y = y * m1 + pltpu.roll(y, half_mid, 1) * m2
    if two:
        g2h = next(it)[...]
        g2l = next(it)[...]
        y = _dot3(y, g2h, g2l)
    o_ref = next(it)
    o_ref[...] = y


def _dft_call(x, g1, tile, pre=None, post=None, g2=None, name="hyena_dft"):
    bsz, rows, k = x.shape
    args = [x]
    specs = [pl.BlockSpec((None, tile, k), lambda b, j: (b, j, 0))]

    def add_mul(mm):
        for a in mm:
            nblk = a.shape[0] // tile
            specs.append(pl.BlockSpec((tile, a.shape[1]), lambda b, j, nblk=nblk: (j % nblk, 0)))
            args.append(a)

    def add_mat(g):
        for a in g:
            specs.append(pl.BlockSpec(a.shape, lambda b, j: (0, 0)))
            args.append(a)

    if pre is not None:
        add_mul(pre)
    add_mat(g1)
    if post is not None:
        add_mul(post)
    if g2 is not None:
        add_mat(g2)
    mid = g1[0].shape[1]
    nout = g2[0].shape[1] if g2 is not None else mid
    body = functools.partial(_dft_body, pre=pre is not None, post=post is not None, two=g2 is not None,
                             half_in=k // 2, half_mid=mid // 2)
    return pl.pallas_call(
        body,
        out_shape=jax.ShapeDtypeStruct((bsz, rows, nout), F32),
        grid=(bsz, rows // tile),
        in_specs=specs,
        out_specs=pl.BlockSpec((None, tile, nout), lambda b, j: (b, j, 0)),
        compiler_params=_cparams(("parallel", "parallel"), VMEM_LIMIT),
        name=name,
    )(*args)


def _hilo(a):
    a = np.asarray(a, np.float64)
    hi = jnp.asarray(a, F32).astype(BF16)
    lo = (jnp.asarray(a, F32) - hi.astype(F32)).astype(BF16)
    return hi, lo


def _cplx_block(f):
    return np.block([[f.real, f.imag], [-f.imag, f.real]])


@functools.lru_cache(maxsize=None)
def _dft_plan(n):
    big = 2 * n
    if big <= 512:
        t = np.arange(n)[:, None]
        k = np.arange(big)[None, :]
        fwd = np.exp(-2j * np.pi * t * k / big)
        fwd_full = np.exp(-2j * np.pi * np.arange(big)[:, None] * k / big)
        inv = np.exp(2j * np.pi * np.arange(big)[:, None] * np.arange(n)[None, :] / big) / big
        return dict(stages=1,
                    fwd=np.concatenate([fwd.real, fwd.imag], axis=1),
                    fwd_full=np.concatenate([fwd_full.real, fwd_full.imag], axis=1),
                    inv=np.concatenate([inv.real, -inv.imag], axis=0))
    n1, n2 = DFT_N1, big // DFT_N1
    k1 = np.arange(n1)[None, :, None]
    t = n2 * np.arange(n1)[None, None, :] + np.arange(n2)[:, None, None]
    ga = np.exp(-2j * np.pi * k1 * t / big)
    ga = np.concatenate([ga.real, ga.imag], axis=1)
    fb = np.exp(-2j * np.pi * np.arange(n2)[:, None] * np.arange(n2)[None, :] / n2)
    fbc = np.block([[fb.real, -fb.imag], [fb.imag, fb.real]])
    fbi = np.conj(fb) / big
    fbic = np.block([[fbi.real, -fbi.imag], [fbi.imag, fbi.real]])
    return dict(stages=2, n1=n1, n2=n2, ga=ga, fb=fbc, fb_inv=fbic)


def _long_conv_setup(n, f):
    c = HY_WIDTH
    plan = _dft_plan(n)
    if plan['stages'] == 1:
        xf = jnp.transpose(f, (0, 2, 1))
        spec = _dft_call(xf, _hilo(plan['fwd_full']), tile=min(c, 256), name="hyena_filter_dft")
        fr, fi = spec[..., :2 * n], spec[..., 2 * n:]
        return [(jnp.concatenate([fr[o], fr[o]], axis=1), jnp.concatenate([-fi[o], fi[o]], axis=1))
                for o in range(HY_ORDER)]
    return _fused_filter_call(f, plan)


def _long_conv(z, hmul):
    bsz, n, c = z.shape
    plan = _dft_plan(n)
    x = jnp.transpose(z, (0, 2, 1))
    y = _dft_call(x, _hilo(plan['fwd']), tile=min(c, 256), post=hmul, g2=_hilo(plan['inv']),
                  name="hyena_conv_short")
    return jnp.transpose(y, (0, 2, 1))


def _plane_pitch(n2):
    return n2 if (n2 // 8) % 2 == 1 else n2 + 8


def _dot3c(g_hi, g_lo, x):
    x_hi, x_lo = _split2(x)
    d = functools.partial(jnp.dot, preferred_element_type=F32)
    return d(g_hi, x_hi) + (d(g_hi, x_lo) + d(g_lo, x_hi))


def _gdot(g_hi, g_lo, x):
    if g_lo is None:
        return jnp.dot(g_hi, x.astype(BF16), preferred_element_type=F32)
    return _dot3c(g_hi, g_lo, x)


def _stage_one(x_ref, g_hi_ref, g_lo_ref, asc_ref, *, n2, nt1, n1, pitch):
    def body(t2, c):
        xs = x_ref[pl.ds(t2, nt1, stride=n2), :]
        g_lo = None if g_lo_ref is None else g_lo_ref[t2]
        asc_ref[pl.ds(t2, 2 * n1, stride=pitch), :] = _gdot(g_hi_ref[t2], g_lo, xs)
        return c
    lax.fori_loop(0, n2, body, 0, unroll=2)


def _plane(asc_ref, k1, n1, n2, pitch):
    o_re = pl.multiple_of(k1 * pitch, 8)
    o_im = pl.multiple_of((n1 + k1) * pitch, 8)
    return o_re, o_im, jnp.concatenate([asc_ref[pl.ds(o_re, n2), :], asc_ref[pl.ds(o_im, n2), :]], axis=0)


def _fused_filter_body(f_ref, g_hi_ref, g_lo_ref, fb_hi_ref, fb_lo_ref, h_ref, asc_ref, *, n1, n2, pitch):
    _stage_one(f_ref, g_hi_ref, g_lo_ref, asc_ref, n2=n2, nt1=n1, n1=n1, pitch=pitch)

    def body(k1, c):
        _, _, ain = _plane(asc_ref, k1, n1, n2, pitch)
        h_ref[k1] = _dot3c(fb_hi_ref[...], fb_lo_ref[...], ain)
        return c
    lax.fori_loop(0, n1, body, 0, unroll=2)


def _fused_filter_call(f, plan):
    order, big, c = f.shape
    n1, n2 = plan['n1'], plan['n2']
    pitch = _plane_pitch(n2)
    g_hi, g_lo = _hilo(plan['ga'])
    fb_hi, fb_lo = _hilo(plan['fb'])
    const = lambda a: pl.BlockSpec(a.shape, lambda o, j: (0,) * a.ndim)
    return pl.pallas_call(
        functools.partial(_fused_filter_body, n1=n1, n2=n2, pitch=pitch),
        out_shape=jax.ShapeDtypeStruct((order, c // LANES, n1, 2 * n2, LANES), F32),
        grid=(order, c // LANES),
        in_specs=[pl.BlockSpec((None, big, LANES), lambda o, j: (o, 0, j)),
                  const(g_hi), const(g_lo), const(fb_hi), const(fb_lo)],
        out_specs=pl.BlockSpec((None, None, n1, 2 * n2, LANES), lambda o, j: (o, j, 0, 0, 0)),
        scratch_shapes=[pltpu.VMEM((2 * n1 * pitch, LANES), F32)],
        compiler_params=_cparams(("parallel", "parallel"), VMEM_LIMIT),
        name="hyena_filter_spectrum",
    )(f, g_hi, g_lo, fb_hi, fb_lo)


def _fused_conv_body(x_ref, xg_ref, skip_ref, h_ref, g_ref, fb_ref, fbi_ref, o_ref, asc_ref, y_ref,
                     *, n1, n2, pitch):
    h1 = n1 // 2
    _stage_one(x_ref, g_ref, None, asc_ref, n2=n2, nt1=h1, n1=n1, pitch=pitch)

    def mid(k1, c):
        o_re, o_im, ain = _plane(asc_ref, k1, n1, n2, pitch)
        b = _gdot(fb_ref[...], None, ain)
        h = h_ref[k1]
        br, bi, hr, hi = b[:n2], b[n2:], h[:n2], h[n2:]
        z = jnp.concatenate([br * hr - bi * hi, br * hi + bi * hr], axis=0)
        cc = _gdot(fbi_ref[...], None, z)
        asc_ref[pl.ds(o_re, n2), :] = cc[:n2]
        asc_ref[pl.ds(o_im, n2), :] = cc[n2:]
        return c
    lax.fori_loop(0, n1, mid, 0, unroll=2)

    def last(t2, c):
        zin = asc_ref[pl.ds(t2, 2 * n1, stride=pitch), :]
        y_ref[pl.ds(t2, h1, stride=n2), :] = lax.dot_general(
            g_ref[t2], zin.astype(BF16), (((0,), (0,)), ((), ())), preferred_element_type=F32)
        return c
    lax.fori_loop(0, n2, last, 0, unroll=2)
    x = x_ref[...]
    o_ref[...] = xg_ref[...] * (y_ref[...] + x * skip_ref[...])


def _fused_conv_call(x, xg, skip, hspec, plan, row0_x, row0_g, bsz, n):
    assert row0_x % n == 0 and row0_g % n == 0
    c = x.shape[1]
    n1, n2 = plan['n1'], plan['n2']
    h1 = n1 // 2
    pitch = _plane_pitch(n2)
    g_hi = _hilo(plan['ga'][:, :, :h1])[0]
    fb_hi = _hilo(plan['fb'])[0]
    fbi_hi = _hilo(plan['fb_inv'])[0]
    const = lambda a: pl.BlockSpec(a.shape, lambda j, b: (0,) * a.ndim)
    seq = lambda row0: pl.BlockSpec((n, LANES), lambda j, b: (row0 // n + b, j))
    return pl.pallas_call(
        functools.partial(_fused_conv_body, n1=n1, n2=n2, pitch=pitch),
        out_shape=jax.ShapeDtypeStruct((bsz * n, c), F32),
        grid=(c // LANES, bsz),
        in_specs=[seq(row0_x), seq(row0_g), pl.BlockSpec((1, LANES), lambda j, b: (0, j)),
                  pl.BlockSpec((None, n1, 2 * n2, LANES), lambda j, b: (j, 0, 0, 0)),
                  const(g_hi), const(fb_hi), const(fbi_hi)],
        out_specs=pl.BlockSpec((n, LANES), lambda j, b: (b, j)),
        scratch_shapes=[pltpu.VMEM((2 * n1 * pitch, LANES), F32), pltpu.VMEM((n, LANES), F32)],
        compiler_params=_cparams(("parallel", "parallel"), VMEM_LIMIT),
        name="hyena_conv_long",
    )(x, xg, skip.reshape(1, c), hspec, g_hi, fb_hi, fbi_hi)


def _hy_gate_body(x_ref, y_ref, z_ref, s_ref, o_ref):
    o_ref[...] = x_ref[...] * (y_ref[...] + z_ref[...] * s_ref[...])


def _hy_gate_call(x, y, z, skip):
    t, c = x.shape
    tok = pl.BlockSpec((512, c), lambda i: (i, 0))
    return pl.pallas_call(
        _hy_gate_body,
        out_shape=jax.ShapeDtypeStruct((t, c), F32),
        grid=(t // 512,),
        in_specs=[tok, tok, tok, pl.BlockSpec((1, c), lambda i: (0, 0))],
        out_specs=tok,
        compiler_params=_cparams(("parallel",)),
        name="hyena_gate",
    )(x, y, z, skip.reshape(1, c))


def _hyena_branch(hy, p, groups, n_ctx_blocks):
    u, x1, x2 = _hy_pre_call(hy, p['hy_conv_w'], p['hy_conv_b'], n_ctx_blocks)
    outs = []
    for start, bsz, n in groups:
        rows = bsz * n
        plan = _dft_plan(n)
        hmul = _long_conv_setup(n, _hy_filter_call(n, p))
        if plan['stages'] == 1:
            ug, x1g, x2g = (a[start:start + rows] for a in (u, x1, x2))
            y = _long_conv(ug.reshape(bsz, n, HY_WIDTH), hmul[0]).reshape(rows, HY_WIDTH)
            z = _hy_gate_call(x1g, y, ug, p['hy_skip'][0])
            y = _long_conv(z.reshape(bsz, n, HY_WIDTH), hmul[1]).reshape(rows, HY_WIDTH)
            outs.append(_hy_gate_call(x2g, y, z, p['hy_skip'][1]))
        else:
            z = _fused_conv_call(u, x1, p['hy_skip'][0], hmul[0], plan, start, start, bsz, n)
            outs.append(_fused_conv_call(z, x2, p['hy_skip'][1], hmul[1], plan, 0, start, bsz, n))
    return jnp.concatenate(outs, axis=0)


def _merge_body(x_ref, yfw_ref, ybw_ref, bonus_ref, g_ref, yb_ref, gt_ref, g1_ref, sh2_ref, sc2_ref, seg_ref,
                gng_ref, gnb_ref, wpa_ref, wpb_ref, wout_ref, lng_ref, lnb_ref, rwh_ref, rwl_ref, rb_ref,
                x1_ref, h2_ref, ti_ref, tg_ref, *, alpha):
    seg = seg_ref[...]
    y = yfw_ref[...] + ybw_ref[...]
    inv = 1.0 / RW_HEAD
    mu = _dot_x3(y, seg) * inv
    yc = y - mu
    var = _dot_x3(yc * yc, seg) * inv
    yn = yc * lax.rsqrt(var + GN_EPS) * gng_ref[...] + gnb_ref[...]
    y_a = (yn + bonus_ref[...]) * g_ref[...]
    gt = gt_ref[...]
    merged = gt[:, :D_MODEL] * _bdot(y_a, wpa_ref[...]) + gt[:, D_MODEL:] * _bdot(yb_ref[...], wpb_ref[...])
    mix = _bdot(merged, wout_ref[...])
    x1 = _layer_norm(alpha * x_ref[...] + g1_ref[...] * mix, lng_ref[...], lnb_ref[...])
    x1_ref[...] = x1
    h2 = x1 * (1.0 + sc2_ref[...]) + sh2_ref[...]
    h2_ref[...] = h2
    h_hi, h_lo = _split2(h2)
    d = functools.partial(jnp.dot, preferred_element_type=F32)
    logits = d(h_hi, rwh_ref[...]) + (d(h_hi, rwl_ref[...]) + d(h_lo, rwh_ref[...])) + rb_ref[...]
    lane = lax.broadcasted_iota(jnp.int32, logits.shape, 1)
    neg = jnp.float32(-jnp.inf)
    cur = jnp.where(lane < N_EXPERTS, logits, neg)
    top_i = jnp.zeros(logits.shape, jnp.int32)
    top_e = jnp.zeros(logits.shape, F32)
    den = jnp.zeros((logits.shape[0], 1), F32)
    v0 = None
    for j in range(TOP_K):
        mx = jnp.max(cur, axis=-1, keepdims=True)
        idx = jnp.min(jnp.where(cur == mx, lane, LANES), axis=-1, keepdims=True)
        if j == 0:
            v0 = mx
        e = jnp.exp(mx - v0)
        den = den + e
        top_i = jnp.where(lane == j, idx, top_i)
        top_e = jnp.where(lane == j, e, top_e)
        cur = jnp.where(lane == idx, neg, cur)
    ti_ref[...] = top_i
    tg_ref[...] = top_e / den


def _merge_call(x, yfw, ybw, bonus, g, yb, gates, mod_l, p, seg, cond_idx, alpha):
    t = x.shape[0]
    w = RW_WIDTH
    tok = lambda n: pl.BlockSpec((TOKEN_BLOCK, n), lambda i: (i, 0))
    modspec = lambda j: pl.BlockSpec((None, 1, D_MODEL), lambda i: (cond_idx(i), 0, j))
    full = lambda *s: pl.BlockSpec(s, lambda i: (0,) * len(s))
    rw_pad = jnp.zeros((D_MODEL, LANES), F32).at[:, :N_EXPERTS].set(p['router_w'])
    rw_hi = rw_pad.astype(BF16)
    rw_lo = (rw_pad - rw_hi.astype(F32)).astype(BF16)
    rb = jnp.zeros((1, LANES), F32).at[0, :N_EXPERTS].set(p['router_b'])
    o = jax.ShapeDtypeStruct((t, D_MODEL), F32)
    return pl.pallas_call(
        functools.partial(_merge_body, alpha=alpha),
        out_shape=(o, o, jax.ShapeDtypeStruct((t, LANES), jnp.int32), jax.ShapeDtypeStruct((t, LANES), F32)),
        grid=(t // TOKEN_BLOCK,),
        in_specs=[tok(D_MODEL), tok(w), tok(w), tok(w), tok(w), tok(w),
                  tok(2 * D_MODEL), modspec(2), modspec(3), modspec(4), full(w, w), full(1, w), full(1, w),
                  full(w, D_MODEL), full(w, D_MODEL), full(D_MODEL, D_MODEL), full(1, D_MODEL), full(1, D_MODEL),
                  full(D_MODEL, LANES), full(D_MODEL, LANES), full(1, LANES)],
        out_specs=(tok(D_MODEL), tok(D_MODEL), tok(LANES), tok(LANES)),
        compiler_params=_cparams(("parallel",), VMEM_LIMIT),
        name="merge_ln_router",
    )(x, yfw, ybw, bonus, g, yb, gates, mod_l, mod_l, mod_l, seg, p['gn_g'].reshape(1, w), p['gn_b'].reshape(1, w),
      p['w_pa'].astype(BF16), p['w_pb'].astype(BF16), p['w_out'].astype(BF16),
      p['ln1_g'].reshape(1, D_MODEL), p['ln1_b'].reshape(1, D_MODEL), rw_hi, rw_lo, rb)


DEINT_COLS = 256


def _deint_body(w_ref, p_ref, g_ref, l_ref):
    y = jnp.dot(w_ref[...].astype(BF16), p_ref[...], preferred_element_type=F32)
    half = DEINT_COLS // 2
    g_ref[...] = y[:, :half].astype(BF16)
    l_ref[...] = y[:, half:].astype(BF16)


def _deint_call(w, layer):
    _, e, k, n2 = w.shape
    half = DEINT_COLS // 2
    sel = np.zeros((DEINT_COLS, DEINT_COLS), np.float32)
    sel[2 * np.arange(half), np.arange(half)] = 1.0
    sel[2 * np.arange(half) + 1, half + np.arange(half)] = 1.0
    o = jax.ShapeDtypeStruct((e, k, n2 // 2), BF16)
    return pl.pallas_call(
        _deint_body,
        out_shape=(o, o),
        grid=(e, n2 // DEINT_COLS),
        in_specs=[pl.BlockSpec((None, None, k, DEINT_COLS), lambda i, j: (layer, i, 0, j)),
                  pl.BlockSpec((DEINT_COLS, DEINT_COLS), lambda i, j: (0, 0))],
        out_specs=(pl.BlockSpec((None, k, half), lambda i, j: (i, 0, j)),) * 2,
        compiler_params=_cparams(("parallel", "parallel")),
        name="expert_w_split",
    )(w, jnp.asarray(sel, BF16))


def _moe_body(blk_e_ref, n_on_ref, tok_ref, tokn_ref, dst_ref, h_hbm, wg_ref, wl_ref, bg_ref, bl_ref, wdn_ref,
              bdn_ref, y_hbm, xbuf, ybuf, sem_in, sem_out, *, n_real):
    i = pl.program_id(0)
    n_on = n_on_ref[0]
    slot = i % 2

    def gather_start(tref, s):
        def body(r, c):
            pltpu.make_async_copy(h_hbm.at[pl.ds(tref[0, r], 1), :], xbuf.at[s, pl.ds(r, 1), :],
                                  sem_in.at[s]).start()
            return c
        lax.fori_loop(0, MOE_ROWS, body, 0, unroll=8)

    def gather_wait(s):
        pltpu.make_async_copy(h_hbm.at[pl.ds(0, MOE_ROWS), :], xbuf.at[s], sem_in.at[s]).wait()

    def scatter_start(s):
        def body(r, c):
            pltpu.make_async_copy(ybuf.at[s, pl.ds(r, 1), :], y_hbm.at[pl.ds(dst_ref[0, r], 1), :],
                                  sem_out.at[s]).start()
            return c
        lax.fori_loop(0, MOE_ROWS, body, 0, unroll=8)

    def scatter_wait(s):
        pltpu.make_async_copy(ybuf.at[s], y_hbm.at[pl.ds(0, MOE_ROWS), :], sem_out.at[s]).wait()

    @pl.when(i == 0)
    def _():
        ybuf[...] = jnp.zeros_like(ybuf)
        fills = [pltpu.make_async_copy(ybuf.at[s], y_hbm.at[pl.ds(n_real + s * MOE_ROWS, MOE_ROWS), :],
                                       sem_out.at[s]) for s in range(2)]
        for cp in fills:
            cp.start()
        for cp in fills:
            cp.wait()

    @pl.when(jnp.logical_and(i == 0, n_on > 0))
    def _():
        gather_start(tok_ref, 0)

    @pl.when(i < n_on)
    def _():
        @pl.when(i + 1 < n_on)
        def _():
            gather_start(tokn_ref, 1 - slot)

        gather_wait(slot)
        x = xbuf[slot].astype(BF16)
        d = functools.partial(jnp.dot, preferred_element_type=F32)
        glu = jnp.minimum(d(x, wg_ref[...]) + bg_ref[...], SWIGLU_LIMIT)
        lin = jnp.clip(d(x, wl_ref[...]) + bl_ref[...], -SWIGLU_LIMIT, SWIGLU_LIMIT)
        act = glu * _sigmoid(SWIGLU_ALPHA * glu) * (lin + 1.0)
        ybuf[slot] = d(act.astype(BF16), wdn_ref[...]) + bdn_ref[...]
        scatter_start(slot)

        @pl.when(i >= 1)
        def _():
            scatter_wait(1 - slot)

        @pl.when(i == n_on - 1)
        def _():
            scatter_wait(slot)


def _moe_call(h2, top_i, wg, wl, bg, bl, wdn, bdn):
    t = h2.shape[0]
    m = t * TOP_K
    e = N_EXPERTS
    blk = MOE_ROWS
    flat_e = top_i.reshape(-1)
    order = jnp.argsort(flat_e, stable=True).astype(jnp.int32)
    sizes = jnp.bincount(flat_e, length=e).astype(jnp.int32)
    padded = (sizes + blk - 1) // blk * blk
    pad_end = jnp.cumsum(padded)
    pad_start = pad_end - padded
    grp_start = jnp.cumsum(sizes) - sizes
    n_blocks = -(-(m + e * (blk - 1)) // blk)
    blk_first = jnp.arange(n_blocks, dtype=jnp.int32) * blk
    blk_e = jnp.minimum(jnp.sum((pad_end[None, :] <= blk_first[:, None]).astype(jnp.int32), axis=1), e - 1)
    pidx = jnp.arange(n_blocks * blk, dtype=jnp.int32)
    e_p = jnp.repeat(blk_e, blk)
    idx = pidx - pad_start[e_p]
    valid = idx < sizes[e_p]
    assign = order[jnp.clip(grp_start[e_p] + idx, 0, m - 1)]
    tok_row = jnp.where(valid, assign // TOP_K, 0).astype(jnp.int32)
    spare = m + ((pidx // blk) % 2) * blk + pidx % blk
    dst_row = jnp.where(valid, (assign % TOP_K) * t + assign // TOP_K, spare).astype(jnp.int32)
    n_on = (pad_end[-1] // blk).astype(jnp.int32).reshape(1)
    tok3 = tok_row.reshape(n_blocks, 1, blk)

    smem = lambda f: pl.BlockSpec((None, 1, blk), f, memory_space=pltpu.SMEM)
    wspec = lambda a, b: pl.BlockSpec((None, a, b), lambda i, be, no: (be[i], 0, 0))
    gs = pltpu.PrefetchScalarGridSpec(
        num_scalar_prefetch=2,
        grid=(n_blocks,),
        in_specs=[smem(lambda i, be, no: (i, 0, 0)),
                  smem(lambda i, be, no: (jnp.minimum(i + 1, n_blocks - 1), 0, 0)),
                  smem(lambda i, be, no: (i, 0, 0)),
                  pl.BlockSpec(memory_space=pl.ANY),
                  wspec(D_MODEL, D_FF), wspec(D_MODEL, D_FF), wspec(1, D_FF), wspec(1, D_FF),
                  wspec(D_FF, D_MODEL), wspec(1, D_MODEL)],
        out_specs=pl.BlockSpec(memory_space=pl.ANY),
        scratch_shapes=[pltpu.VMEM((2, blk, D_MODEL), F32), pltpu.VMEM((2, blk, D_MODEL), F32),
                        pltpu.SemaphoreType.DMA((2,)), pltpu.SemaphoreType.DMA((2,))],
    )
    return pl.pallas_call(
        functools.partial(_moe_body, n_real=m),
        out_shape=jax.ShapeDtypeStruct((m + 2 * blk, D_MODEL), F32),
        grid_spec=gs,
        compiler_params=_cparams(("arbitrary",), VMEM_LIMIT),
        name="moe_experts",
    )(blk_e, n_on, tok3, tok3, dst_row.reshape(n_blocks, 1, blk), h2, wg, wl, bg.reshape(e, 1, D_FF),
      bl.reshape(e, 1, D_FF), wdn, bdn.reshape(e, 1, D_MODEL))


def _combine_body(x1_ref, y0_ref, y1_ref, y2_ref, y3_ref, tg_ref, g2_ref, lng_ref, lnb_ref, o_ref, *, alpha):
    tg = tg_ref[...]
    moe = tg[:, 0:1] * y0_ref[...]
    for j, y_ref in enumerate((y1_ref, y2_ref, y3_ref), start=1):
        moe = moe + tg[:, j:j + 1] * y_ref[...]
    o_ref[...] = _layer_norm(alpha * x1_ref[...] + g2_ref[...] * moe, lng_ref[...], lnb_ref[...])


def _combine_call(x1, yexp, tg, mod_l, p, cond_idx, alpha):
    t = x1.shape[0]
    nb = t // TOKEN_BLOCK
    tok = lambda n: pl.BlockSpec((TOKEN_BLOCK, n), lambda i: (i, 0))
    full = lambda *s: pl.BlockSpec(s, lambda i: (0,) * len(s))
    yspec = [pl.BlockSpec((TOKEN_BLOCK, D_MODEL), lambda i, j=j: (j * nb + i, 0)) for j in range(TOP_K)]
    return pl.pallas_call(
        functools.partial(_combine_body, alpha=alpha),
        out_shape=jax.ShapeDtypeStruct((t, D_MODEL), F32),
        grid=(nb,),
        in_specs=[tok(D_MODEL)] + yspec + [tok(LANES),
                  pl.BlockSpec((None, 1, D_MODEL), lambda i: (cond_idx(i), 0, 5)),
                  full(1, D_MODEL), full(1, D_MODEL)],
        out_specs=tok(D_MODEL),
        compiler_params=_cparams(("parallel",), VMEM_LIMIT),
        name="moe_combine_ln",
    )(x1, yexp, yexp, yexp, yexp, tg, mod_l, p['ln2_g'].reshape(1, D_MODEL), p['ln2_b'].reshape(1, D_MODEL))


def kernel(x_prompt, x_sample, c, state_rwkv, c_ctx, w_mod, b_mod, w_in, mu_shift, w0, w_lora_up, a0, a_lora_up, g_up, k_k, k_a, r_k, gn_g, gn_b, hy_conv_w, hy_conv_b, hy_f_w1, hy_f_b1, hy_f_w2, hy_f_b2, hy_f_freq, hy_f_w3, hy_skip, w_pa, w_pb, w_out, ln1_g, ln1_b, ln2_g, ln2_b, router_w, router_b, ex_w_up, ex_b_up, ex_w_down, ex_b_down):
    bsz, seq, dm = x_prompt.shape
    dbsz, dseq, _ = x_sample.shape
    depth = w_mod.shape[0]
    assert dm == D_MODEL and seq == TOKEN_BLOCK and dseq % TOKEN_BLOCK == 0 and TOKEN_BLOCK % GRID_W == 0
    assert 1 + dbsz <= 8
    alpha = (2 * depth) ** 0.25
    t_ctx = bsz * seq
    n_ctx_blocks = t_ctx // TOKEN_BLOCK
    lat_blocks = dseq // TOKEN_BLOCK

    def cond_idx(i):
        return jnp.where(i < n_ctx_blocks, 0, 1 + (i - n_ctx_blocks) // lat_blocks)

    x = jnp.concatenate([x_prompt.reshape(t_ctx, dm), x_sample.reshape(dbsz * dseq, dm)], axis=0)
    cond8 = jnp.zeros((8, dm), F32).at[0].set(c_ctx).at[1:1 + dbsz].set(c)
    mod = _mod_call(cond8, w_mod, b_mod)

    hd = RW_HEAD
    seg = (np.arange(RW_WIDTH)[:, None] // hd == np.arange(RW_WIDTH)[None, :] // hd)
    seg = jnp.asarray(seg, BF16)
    seq_lens = [seq] * bsz + [dseq] * dbsz
    tbl_np, nsteps = _scan_table(seq_lens)
    tbl = jnp.asarray(tbl_np)
    groups = [(0, bsz, seq), (t_ctx, dbsz, dseq)]

    new_states = []
    for l in range(depth):
        p = dict(mu_shift=mu_shift[l], w0=w0[l], w_lora_up=w_lora_up[l], a0=a0[l], a_lora_up=a_lora_up[l],
                 g_up=g_up[l], k_k=k_k[l], k_a=k_a[l], r_k=r_k[l], gn_g=gn_g[l], gn_b=gn_b[l],
                 hy_conv_w=hy_conv_w[l], hy_conv_b=hy_conv_b[l], hy_f_w1=hy_f_w1[l], hy_f_b1=hy_f_b1[l],
                 hy_f_w2=hy_f_w2[l], hy_f_b2=hy_f_b2[l], hy_f_freq=hy_f_freq[l], hy_f_w3=hy_f_w3[l],
                 hy_skip=hy_skip[l], w_pa=w_pa[l], w_pb=w_pb[l], w_out=w_out[l], ln1_g=ln1_g[l], ln1_b=ln1_b[l],
                 ln2_g=ln2_g[l], ln2_b=ln2_b[l], router_w=router_w[l], router_b=router_b[l])
        mod_l = mod[l].reshape(8, 1, 6 * dm)
        rw, hy, gates = _inproj_call(x, mod_l, w_in[l].astype(BF16), cond_idx)
        r, kk, v, lw, b, kd, g, bonus = _rwkv_pre_call(rw, p, seg, n_ctx_blocks)
        s0 = jnp.concatenate([jnp.zeros((bsz, 2, RW_HEADS, hd, hd), F32), state_rwkv[:, l].astype(F32)], axis=0)
        yfw, ybw, sfin = _scan_call(tbl, nsteps, bsz + dbsz, r, kk, v, lw, b, kd, s0)
        new_states.append(sfin[:bsz].astype(x_prompt.dtype))
        yb = _hyena_branch(hy, p, groups, n_ctx_blocks)
        x1, h2, top_i, top_g = _merge_call(x, yfw, ybw, bonus, g, yb, gates, mod_l, p, seg, cond_idx, alpha)
        wg, wl = _deint_call(ex_w_up, l)
        yexp = _moe_call(h2, top_i[:, :TOP_K], wg, wl, ex_b_up[l][:, 0::2], ex_b_up[l][:, 1::2],
                         ex_w_down[l].astype(BF16), ex_b_down[l])
        x = _combine_call(x1, yexp, top_g, mod_l, p, cond_idx, alpha)

    y_p = x[:t_ctx].reshape(bsz, seq, dm)
    y_s = x[t_ctx:].reshape(dbsz, dseq, dm)
    return (y_p, y_s, jnp.stack(new_states, axis=1))
```

```python
import functools
import math

import numpy as np
import jax
import jax.numpy as jnp
from jax import lax
from jax.experimental import pallas as pl
from jax.experimental.pallas import tpu as pltpu

F32 = jnp.float32
BF16 = jnp.bfloat16

D_MODEL = 1024
GRID_W = 64
RW_WIDTH = 512
RW_HEAD = 64
RW_HEADS = RW_WIDTH // RW_HEAD
LORA_W = 64
LORA_A = 64
LORA_G = 128
GN_EPS = 64e-5
HY_WIDTH = 512
HY_ORDER = 2
HY_BANDS = 16
HY_EMB = 2 * HY_BANDS + 1
HY_FFN = 64
HY_DECAY_TARGET = 1e-2
HY_DECAY_SHORT_PCT = 0.3
HY_DECAY_LONG_PCT = 1.5
RW_COLS = 3 * RW_WIDTH + LORA_W + LORA_A + LORA_G
HY_COLS = (HY_ORDER + 1) * HY_WIDTH
IN_COLS = RW_COLS + HY_COLS + 2 * D_MODEL
N_EXPERTS = 32
TOP_K = 4
D_FF = D_MODEL
SWIGLU_ALPHA = 1.702
SWIGLU_LIMIT = 7.0
LN_EPS = 1e-5

TOKEN_BLOCK = 256
SCAN_CHUNK = 64
INV_BLOCK = 16
DFT_N1 = 128
MOE_ROWS = 256
LANES = 128
VMEM_LIMIT = 56 * 1024 * 1024


def _cparams(sem, vmem=None):
    return pltpu.CompilerParams(dimension_semantics=sem, vmem_limit_bytes=vmem)


def _bdot(a, b):
    return jnp.dot(a.astype(BF16), b.astype(BF16), preferred_element_type=F32)


def _bdg(a, b, ca, cb):
    return lax.dot_general(a.astype(BF16), b.astype(BF16), (((ca,), (cb,)), ((0,), (0,))), preferred_element_type=F32)


def _bmm(a, b):
    return _bdg(a, b, 2, 1)


def _bmm_nt(a, b):
    return _bdg(a, b, 2, 2)


def _bmm_tn(a, b):
    return _bdg(a, b, 1, 1)


def _split2(x):
    hi = x.astype(BF16)
    lo = (x - hi.astype(F32)).astype(BF16)
    return hi, lo


def _split3(x):
    hi = x.astype(BF16)
    r1 = x - hi.astype(F32)
    mid = r1.astype(BF16)
    lo = (r1 - mid.astype(F32)).astype(BF16)
    return hi, mid, lo


def _dot_x3(x, g_bf16):
    hi, mid, lo = _split3(x)
    d = functools.partial(jnp.dot, preferred_element_type=F32)
    return d(hi, g_bf16) + (d(mid, g_bf16) + d(lo, g_bf16))


def _dot3(x, g_hi, g_lo):
    x_hi, x_lo = _split2(x)
    d = functools.partial(jnp.dot, preferred_element_type=F32)
    return d(x_hi, g_hi) + (d(x_hi, g_lo) + d(x_lo, g_hi))


def _sigmoid(x):
    return 1.0 / (1.0 + jnp.exp(-x))


TOKEN_TILE = D_MODEL // LANES


def _store_token_tiles(ref, row0, x):
    rows = x.shape[0]
    for s in range(TOKEN_TILE):
        ref[pl.ds(row0 + s, rows, stride=TOKEN_TILE), :] = x[:, s * LANES:(s + 1) * LANES]


def _load_token_tiles(ref, row0, rows):
    return jnp.concatenate([ref[pl.ds(row0 + s, rows, stride=TOKEN_TILE), :] for s in range(TOKEN_TILE)], axis=1)


def _layer_norm(x, g, b):
    mu = jnp.mean(x, axis=-1, keepdims=True)
    xc = x - mu
    var = jnp.mean(xc * xc, axis=-1, keepdims=True)
    return xc * lax.rsqrt(var + LN_EPS) * g + b


def _mod_body(c_ref, w_ref, b_ref, o_ref):
    c = c_ref[...]
    o_ref[...] = _bdot(c * _sigmoid(c), w_ref[...]) + b_ref[...]


def _mod_call(cond8, w_mod, b_mod):
    depth = w_mod.shape[0]
    tn = 1536
    return pl.pallas_call(
        _mod_body,
        out_shape=jax.ShapeDtypeStruct((depth, 8, 6 * D_MODEL), F32),
        grid=(depth, 6 * D_MODEL // tn),
        in_specs=[pl.BlockSpec((8, D_MODEL), lambda l, j: (0, 0)),
                  pl.BlockSpec((None, D_MODEL, tn), lambda l, j: (l, 0, j)),
                  pl.BlockSpec((None, 1, tn), lambda l, j: (l, 0, j))],
        out_specs=pl.BlockSpec((None, 8, tn), lambda l, j: (l, 0, j)),
        compiler_params=_cparams(("parallel", "parallel"), VMEM_LIMIT),
        name="adaln_mod",
    )(cond8, w_mod, b_mod.reshape(depth, 1, 6 * D_MODEL))


def _inproj_body(x_ref, sh_ref, sc_ref, w_ref, rw_ref, hy_ref, gt_ref):
    h = (x_ref[...] * (1.0 + sc_ref[...]) + sh_ref[...]).astype(BF16)
    d = functools.partial(jnp.dot, preferred_element_type=F32)
    rw_ref[...] = d(h, w_ref[:, :RW_COLS])
    hy_ref[...] = d(h, w_ref[:, RW_COLS:RW_COLS + HY_COLS])
    gt_ref[...] = _sigmoid(d(h, w_ref[:, RW_COLS + HY_COLS:]))


def _inproj_call(x, mod_l, w_in_bf16, cond_idx):
    t = x.shape[0]
    nb = t // TOKEN_BLOCK
    tok = lambda n: pl.BlockSpec((TOKEN_BLOCK, n), lambda i: (i, 0))
    modspec = lambda j: pl.BlockSpec((None, 1, D_MODEL), lambda i: (cond_idx(i), 0, j))
    return pl.pallas_call(
        _inproj_body,
        out_shape=(jax.ShapeDtypeStruct((t, RW_COLS), F32), jax.ShapeDtypeStruct((t, HY_COLS), F32),
                   jax.ShapeDtypeStruct((t, 2 * D_MODEL), F32)),
        grid=(nb,),
        in_specs=[tok(D_MODEL), modspec(0), modspec(1),
                  pl.BlockSpec((D_MODEL, IN_COLS), lambda i: (0, 0))],
        out_specs=(tok(RW_COLS), tok(HY_COLS), tok(2 * D_MODEL)),
        compiler_params=_cparams(("parallel",), VMEM_LIMIT),
        name="in_proj",
    )(x, mod_l, mod_l, w_in_bf16)


def _neighbours(x, n_ctx_blocks):
    rows = x.shape[0]
    row = lax.broadcasted_iota(jnp.int32, (rows, 1), 0)
    seg_mask = jnp.where(pl.program_id(0) < n_ctx_blocks, rows - 1, GRID_W - 1)
    pos = row & seg_mask
    prev = jnp.where(pos == 0, 0.0, pltpu.roll(x, 1, 0))
    nxt = jnp.where(pos == seg_mask, 0.0, pltpu.roll(x, rows - 1, 0))
    return prev, nxt


def _rwkv_pre_body(rw_ref, mu_ref, kk_s_ref, ka_ref, w0_ref, a0_ref, wl_ref, al_ref, gup_ref, rk_ref, seg_ref,
                   r_ref, kk_ref, v_ref, lw_ref, b_ref, kd_ref, g_ref, bonus_ref, *, n_ctx_blocks):
    x = rw_ref[...]
    prev, nxt = _neighbours(x, n_ctx_blocks)
    cols = x + mu_ref[...] * (0.5 * (prev + nxt) - x)
    w = RW_WIDTH
    r = cols[:, :w]
    k = cols[:, w:2 * w]
    v = cols[:, 2 * w:3 * w]
    wd = cols[:, 3 * w:3 * w + LORA_W]
    ad = cols[:, 3 * w + LORA_W:3 * w + LORA_W + LORA_A]
    gd = cols[:, 3 * w + LORA_W + LORA_A:]
    seg = seg_ref[...]
    kkr = k * kk_s_ref[...]
    ss = _dot_x3(kkr * kkr, seg)
    kk = kkr / jnp.maximum(jnp.sqrt(ss), 1e-12)
    r_ref[...] = r
    kk_ref[...] = kk
    v_ref[...] = v
    g_ref[...] = _bdot(_sigmoid(gd), gup_ref[...])
    tw = jnp.tanh(wd)
    bonus = jnp.zeros_like(r)
    for d in range(2):
        w_logit = w0_ref[d:d + 1, :] + _bdot(tw, wl_ref[d])
        lw_ref[d] = -math.exp(-0.5) * _sigmoid(w_logit)
        a = _sigmoid(a0_ref[d:d + 1, :] + _bdot(ad, al_ref[d]))
        kd = k * (1.0 + (a - 1.0) * ka_ref[...])
        kd_ref[d] = kd
        b_ref[d] = kk * a
        bonus = bonus + _dot_x3(r * kd * rk_ref[d:d + 1, :], seg) * v
    bonus_ref[...] = bonus


def _rwkv_pre_call(rw, p, seg, n_ctx_blocks):
    t = rw.shape[0]
    nb = t // TOKEN_BLOCK
    w = RW_WIDTH
    tok = lambda n: pl.BlockSpec((TOKEN_BLOCK, n), lambda i: (i, 0))
    tok2 = pl.BlockSpec((2, TOKEN_BLOCK, w), lambda i: (0, i, 0))
    full = lambda *s: pl.BlockSpec(s, lambda i: (0,) * len(s))
    o1 = jax.ShapeDtypeStruct((t, w), F32)
    o2 = jax.ShapeDtypeStruct((2, t, w), F32)
    return pl.pallas_call(
        functools.partial(_rwkv_pre_body, n_ctx_blocks=n_ctx_blocks),
        out_shape=(o1, o1, o1, o2, o2, o2, o1, o1),
        grid=(nb,),
        in_specs=[tok(RW_COLS), full(1, RW_COLS), full(1, w), full(1, w), full(2, w), full(2, w),
                  full(2, LORA_W, w), full(2, LORA_A, w), full(LORA_G, w), full(2, w), full(w, w)],
        out_specs=(tok(w), tok(w), tok(w), tok2, tok2, tok2, tok(w), tok(w)),
        compiler_params=_cparams(("parallel",), VMEM_LIMIT),
        name="rwkv_pre",
    )(rw, p['mu_shift'].reshape(1, RW_COLS), p['k_k'].reshape(1, w), p['k_a'].reshape(1, w), p['w0'], p['a0'],
      p['w_lora_up'], p['a_lora_up'], p['g_up'], p['r_k'].reshape(2, w), seg)


def _scan_body(tbl_ref, rf_ref, kkf_ref, vf_ref, rb_ref, kkb_ref, vb_ref, lwf_ref, bf_ref, kdf_ref, lwb_ref,
               bb_ref, kdb_ref, s0_ref, yf_ref, yb_ref, sfin_ref, s_ref):
    base = pl.program_id(0) * 5
    c = SCAN_CHUNK
    hd = RW_HEAD
    nh = 2 * RW_HEADS

    @pl.when(tbl_ref[base + 3] == 1)
    def _():
        s_ref[...] = s0_ref[...].reshape(nh, hd, hd)

    rowi = lax.broadcasted_iota(jnp.int32, (c, c), 0)
    coli = lax.broadcasted_iota(jnp.int32, (c, c), 1)
    same_blk = (rowi // INV_BLOCK) == (coli // INV_BLOCK)
    eye = (rowi == coli).astype(F32)

    def both(fwd, bwd):
        return jnp.concatenate([jnp.broadcast_to(fwd, (RW_HEADS, c, c)), jnp.broadcast_to(bwd, (RW_HEADS, c, c))],
                               axis=0)

    diff = both(rowi - coli, coli - rowi)
    strict = diff > 0
    incl = diff >= 0

    def heads(a):
        return jnp.stack([a[:, h * hd:(h + 1) * hd] for h in range(RW_HEADS)], axis=0)

    def prep(r_ref, kk_ref, v_ref, lw_ref, b_ref, kd_ref, tri):
        lw = lw_ref[...]
        hi, mid, lo = _split3(lw)
        dd = functools.partial(jnp.dot, preferred_element_type=F32)
        cum = dd(tri, hi) + (dd(tri, mid) + dd(tri, lo))
        tot = jnp.sum(lw, axis=0, keepdims=True)
        e_neg = jnp.exp(-cum)
        e_rem = jnp.exp(tot - cum)
        kk = kk_ref[...]
        bb = b_ref[...]
        kd = kd_ref[...]
        return [heads(a) for a in (kk * jnp.exp(cum - lw), bb * e_neg, kd * e_neg, r_ref[...] * jnp.exp(cum),
                                   v_ref[...], kd * e_rem, bb * e_rem, jnp.exp(tot))]

    fw = prep(rf_ref, kkf_ref, vf_ref, lwf_ref, bf_ref, kdf_ref, (rowi >= coli).astype(BF16))
    bw = prep(rb_ref, kkb_ref, vb_ref, lwb_ref, bb_ref, kdb_ref, (rowi <= coli).astype(BF16))
    a_h, b_h, k_h, r_h, v_h, kapg_h, betg_h, gtot_h = [jnp.concatenate([f, b], axis=0) for f, b in zip(fw, bw)]
    ar = jnp.concatenate([a_h, r_h], axis=1)
    gb = _bmm_nt(ar, b_h)
    gk = _bmm_nt(ar, k_h)
    low = jnp.where(strict, gb[:, :c], 0.0)
    a_ka = jnp.where(strict, gk[:, :c], 0.0)
    a_br = jnp.where(incl, gb[:, c:], 0.0)
    a_kr = jnp.where(incl, gk[:, c:], 0.0)
    nd = jnp.where(same_blk, -low, 0.0)
    loff = jnp.where(same_blk, 0.0, low)
    x = eye + nd
    n2 = _bmm(nd, nd)
    x = x + _bmm(x, n2)
    n4 = _bmm(n2, n2)
    x = x + _bmm(x, n4)
    n8 = _bmm(n4, n4)
    x = x + _bmm(x, n8)
    m = _bmm(x, loff)
    m2 = _bmm(m, m)
    y1 = x + _bmm(m2, x)
    tinv = y1 - _bmm(m, y1)
    w_h = _bmm(a_ka, v_h)
    rhs = jnp.concatenate([a_h, w_h], axis=2)
    x0 = _bmm(tinv, rhs)
    res = rhs - x0 - _bmm(low, x0)
    xs = x0 + _bmm(tinv, res)
    p_h = xs[:, :, :hd]
    q_h = xs[:, :, hd:]
    s_old = s_ref[...]
    uy = _bmm_nt(jnp.concatenate([p_h, r_h], axis=1), s_old)
    u_h = uy[:, :c] + q_h
    y_h = uy[:, c:] + _bmm(a_kr, v_h) - _bmm(a_br, u_h)
    zv = jnp.concatenate([v_h, u_h], axis=1)
    zk = jnp.concatenate([kapg_h, -betg_h], axis=1)
    s_ref[...] = s_old * gtot_h + _bmm_tn(zv, zk)
    for h in range(RW_HEADS):
        yf_ref[:, h * hd:(h + 1) * hd] = y_h[h]
        yb_ref[:, h * hd:(h + 1) * hd] = y_h[RW_HEADS + h]

    @pl.when(tbl_ref[base + 4] == 1)
    def _():
        sfin_ref[...] = s_ref[...].reshape(2, RW_HEADS, hd, hd)


def _scan_call(tbl, nsteps, nseq, r, kk, v, lw, b, kd, s0):
    t = r.shape[0]
    w = RW_WIDTH
    c = SCAN_CHUNK
    blk = lambda d: pl.BlockSpec((c, w), lambda i, tb: (tb[i * 5 + d], 0))
    blk_dir = lambda d: pl.BlockSpec((None, c, w), lambda i, tb: (d, tb[i * 5 + d], 0))
    st = pl.BlockSpec((None, 2, RW_HEADS, RW_HEAD, RW_HEAD), lambda i, tb: (tb[i * 5 + 2], 0, 0, 0, 0))
    gs = pltpu.PrefetchScalarGridSpec(
        num_scalar_prefetch=1,
        grid=(nsteps,),
        in_specs=[blk(0), blk(0), blk(0), blk(1), blk(1), blk(1), blk_dir(0), blk_dir(0), blk_dir(0),
                  blk_dir(1), blk_dir(1), blk_dir(1), st],
        out_specs=(blk(0), blk(1), st),
        scratch_shapes=[pltpu.VMEM((2 * RW_HEADS, RW_HEAD, RW_HEAD), F32)],
    )
    o = jax.ShapeDtypeStruct((t, w), F32)
    return pl.pallas_call(
        _scan_body,
        out_shape=(o, o, jax.ShapeDtypeStruct((nseq, 2, RW_HEADS, RW_HEAD, RW_HEAD), F32)),
        grid_spec=gs,
        compiler_params=_cparams(("arbitrary",), VMEM_LIMIT),
        name="rwkv_scan",
    )(tbl, r, kk, v, r, kk, v, lw, b, kd, lw, b, kd, s0)


def _scan_table(seq_lens):
    c = SCAN_CHUNK
    rows = []
    start = 0
    for s, n in enumerate(seq_lens):
        nc = n // c
        for j in range(nc):
            rows.append((start + j, start + nc - 1 - j, s, int(j == 0), int(j == nc - 1)))
        start += nc
    return np.asarray(rows, np.int32).reshape(-1), len(rows)


def _hy_pre_body(hy_ref, w_ref, b_ref, u_ref, x1_ref, x2_ref, *, n_ctx_blocks):
    x = hy_ref[...]
    prev, nxt = _neighbours(x, n_ctx_blocks)
    cols = prev * w_ref[0:1, :] + x * w_ref[1:2, :] + nxt * w_ref[2:3, :] + b_ref[...]
    w = HY_WIDTH
    u_ref[...] = cols[:, :w]
    x1_ref[...] = cols[:, w:2 * w]
    x2_ref[...] = cols[:, 2 * w:]


def _hy_pre_call(hy, conv_w, conv_b, n_ctx_blocks):
    t = hy.shape[0]
    tok = lambda n: pl.BlockSpec((TOKEN_BLOCK, n), lambda i: (i, 0))
    o = jax.ShapeDtypeStruct((t, HY_WIDTH), F32)
    return pl.pallas_call(
        functools.partial(_hy_pre_body, n_ctx_blocks=n_ctx_blocks),
        out_shape=(o, o, o),
        grid=(t // TOKEN_BLOCK,),
        in_specs=[tok(HY_COLS), pl.BlockSpec((3, HY_COLS), lambda i: (0, 0)),
                  pl.BlockSpec((1, HY_COLS), lambda i: (0, 0))],
        out_specs=(tok(HY_WIDTH),) * 3,
        compiler_params=_cparams(("parallel",)),
        name="hyena_pre",
    )(hy, conv_w, conv_b.reshape(1, HY_COLS))


def _dot_f32(a, b):
    a_hi, a_lo = _split2(a)
    b_hi, b_lo = _split2(b)
    d = functools.partial(jnp.dot, preferred_element_type=F32)
    return d(a_hi, b_hi) + (d(a_hi, b_lo) + d(a_lo, b_hi))


def _hy_filter_body(bands_ref, w1t_ref, w1c_ref, w1s_ref, b1_ref, w2_ref, b2_ref, fr_ref, w3_ref, dl_ref, f_ref,
                    *, n, tile):
    row = lax.broadcasted_iota(jnp.int32, (tile, LANES), 0) + pl.program_id(0) * tile
    fwd = row < n
    pos = jnp.where(fwd, row, 2 * n - row).astype(F32)
    tcol = pos / n
    ang = (2.0 * math.pi / n) * pos * bands_ref[...]
    pre1 = (tcol[:, :HY_FFN] * w1t_ref[...] + _dot_f32(jnp.cos(ang), w1c_ref[...])
            + _dot_f32(jnp.sin(ang), w1s_ref[...]) + b1_ref[...])
    fr = fr_ref[...]
    h1 = jnp.sin(fr * pre1)
    h2 = jnp.sin(fr * (_dot_f32(h1, w2_ref[...]) + b2_ref[...]))
    h = _dot_f32(h2, w3_ref[...])
    keep = row != n
    for q in range(HY_WIDTH // LANES):
        win = jnp.where(keep, jnp.exp(-tcol * dl_ref[:, q * LANES:(q + 1) * LANES]), 0.0)
        for o in range(HY_ORDER):
            lo = o * 2 * HY_WIDTH + q * LANES
            f_ref[o, :, q * LANES:(q + 1) * LANES] = jnp.where(
                fwd, h[:, lo:lo + LANES], h[:, lo + HY_WIDTH:lo + HY_WIDTH + LANES]) * win


def _hy_filter_call(n, p):
    bands = jnp.zeros((1, LANES), F32).at[0, :HY_BANDS].set(
        jnp.linspace(1e-4, HY_BANDS - 1, HY_BANDS, dtype=F32))
    max_decay = math.log(HY_DECAY_TARGET) / HY_DECAY_SHORT_PCT
    min_decay = math.log(HY_DECAY_TARGET) / HY_DECAY_LONG_PCT
    deltas = jnp.abs(jnp.linspace(min_decay, max_decay, HY_WIDTH, dtype=F32)).reshape(1, HY_WIDTH)
    w1 = p['hy_f_w1']
    w1t = w1[0:1]
    w1c = jnp.zeros((LANES, HY_FFN), F32).at[:HY_BANDS].set(w1[1:1 + HY_BANDS])
    w1s = jnp.zeros((LANES, HY_FFN), F32).at[:HY_BANDS].set(w1[1 + HY_BANDS:HY_EMB])
    nout = HY_ORDER * 2 * HY_WIDTH
    tile = min(2 * n, 512)
    full = lambda *s: pl.BlockSpec(s, lambda i: (0,) * len(s))
    return pl.pallas_call(
        functools.partial(_hy_filter_body, n=n, tile=tile),
        out_shape=jax.ShapeDtypeStruct((HY_ORDER, 2 * n, HY_WIDTH), F32),
        grid=(2 * n // tile,),
        in_specs=[full(1, LANES), full(1, HY_FFN), full(LANES, HY_FFN), full(LANES, HY_FFN), full(1, HY_FFN),
                  full(HY_FFN, HY_FFN), full(1, HY_FFN), full(1, HY_FFN), full(HY_FFN, nout), full(1, HY_WIDTH)],
        out_specs=pl.BlockSpec((HY_ORDER, tile, HY_WIDTH), lambda i: (0, i, 0)),
        compiler_params=_cparams(("parallel",), VMEM_LIMIT),
        name="hyena_filter",
    )(bands, w1t, w1c, w1s, p['hy_f_b1'].reshape(1, HY_FFN), p['hy_f_w2'], p['hy_f_b2'].reshape(1, HY_FFN),
      p['hy_f_freq'].reshape(1, HY_FFN), p['hy_f_w3'], deltas)


def _dft_body(*refs, pre, post, two, half_in, half_mid):
    it = iter(refs)
    x = next(it)[...]
    if pre:
        m1 = next(it)[...]
        m2 = next(it)[...]
        x = x * m1 + pltpu.roll(x, half_in, 1) * m2
    g1h = next(it)[...]
    g1l = next(it)[...]
    y = _dot3(x, g1h, g1l)
    if post:
        m1 = next(it)[...]
        m2 = next(it)[...]
        y = y * m1 + pltpu.roll(y, half_mid, 1) * m2
    if two:
        g2h = next(it)[...]
        g2l = next(it)[...]
        y = _dot3(y, g2h, g2l)
    o_ref = next(it)
    o_ref[...] = y


def _dft_call(x, g1, tile, pre=None, post=None, g2=None, name="hyena_dft"):
    bsz, rows, k = x.shape
    args = [x]
    specs = [pl.BlockSpec((None, tile, k), lambda b, j: (b, j, 0))]

    def add_mul(mm):
        for a in mm:
            nblk = a.shape[0] // tile
            specs.append(pl.BlockSpec((tile, a.shape[1]), lambda b, j, nblk=nblk: (j % nblk, 0)))
            args.append(a)

    def add_mat(g):
        for a in g:
            specs.append(pl.BlockSpec(a.shape, lambda b, j: (0, 0)))
            args.append(a)

    if pre is not None:
        add_mul(pre)
    add_mat(g1)
    if post is not None:
        add_mul(post)
    if g2 is not None:
        add_mat(g2)
    mid = g1[0].shape[1]
    nout = g2[0].shape[1] if g2 is not None else mid
    body = functools.partial(_dft_body, pre=pre is not None, post=post is not None, two=g2 is not None,
                             half_in=k // 2, half_mid=mid // 2)
    return pl.pallas_call(
        body,
        out_shape=jax.ShapeDtypeStruct((bsz, rows, nout), F32),
        grid=(bsz, rows // tile),
        in_specs=specs,
        out_specs=pl.BlockSpec((None, tile, nout), lambda b, j: (b, j, 0)),
        compiler_params=_cparams(("parallel", "parallel"), VMEM_LIMIT),
        name=name,
    )(*args)


def _hilo(a):
    a = np.asarray(a, np.float64)
    hi = jnp.asarray(a, F32).astype(BF16)
    lo = (jnp.asarray(a, F32) - hi.astype(F32)).astype(BF16)
    return hi, lo


def _cplx_block(f):
    return np.block([[f.real, f.imag], [-f.imag, f.real]])


@functools.lru_cache(maxsize=None)
def _dft_plan(n):
    big = 2 * n
    if big <= 512:
        t = np.arange(n)[:, None]
        k = np.arange(big)[None, :]
        fwd = np.exp(-2j * np.pi * t * k / big)
        fwd_full = np.exp(-2j * np.pi * np.arange(big)[:, None] * k / big)
        inv = np.exp(2j * np.pi * np.arange(big)[:, None] * np.arange(n)[None, :] / big) / big
        return dict(stages=1,
                    fwd=np.concatenate([fwd.real, fwd.imag], axis=1),
                    fwd_full=np.concatenate([fwd_full.real, fwd_full.imag], axis=1),
                    inv=np.concatenate([inv.real, -inv.imag], axis=0))
    n1, n2 = DFT_N1, big // DFT_N1
    k1 = np.arange(n1)[None, :, None]
    t = n2 * np.arange(n1)[None, None, :] + np.arange(n2)[:, None, None]
    ga = np.exp(-2j * np.pi * k1 * t / big)
    ga = np.concatenate([ga.real, ga.imag], axis=1)
    fb = np.exp(-2j * np.pi * np.arange(n2)[:, None] * np.arange(n2)[None, :] / n2)
    fbc = np.block([[fb.real, -fb.imag], [fb.imag, fb.real]])
    fbi = np.conj(fb) / big
    fbic = np.block([[fbi.real, -fbi.imag], [fbi.imag, fbi.real]])
    return dict(stages=2, n1=n1, n2=n2, ga=ga, fb=fbc, fb_inv=fbic)


def _long_conv_setup(n, f):
    c = HY_WIDTH
    plan = _dft_plan(n)
    if plan['stages'] == 1:
        xf = jnp.transpose(f, (0, 2, 1))
        spec = _dft_call(xf, _hilo(plan['fwd_full']), tile=min(c, 256), name="hyena_filter_dft")
        fr, fi = spec[..., :2 * n], spec[..., 2 * n:]
        return [(jnp.concatenate([fr[o], fr[o]], axis=1), jnp.concatenate([-fi[o], fi[o]], axis=1))
                for o in range(HY_ORDER)]
    return _fused_filter_call(f, plan)


def _long_conv(z, hmul):
    bsz, n, c = z.shape
    plan = _dft_plan(n)
    x = jnp.transpose(z, (0, 2, 1))
    y = _dft_call(x, _hilo(plan['fwd']), tile=min(c, 256), post=hmul, g2=_hilo(plan['inv']),
                  name="hyena_conv_short")
    return jnp.transpose(y, (0, 2, 1))


def _plane_pitch(n2):
    return n2 if (n2 // 8) % 2 == 1 else n2 + 8


def _dot3c(g_hi, g_lo, x):
    x_hi, x_lo = _split2(x)
    d = functools.partial(jnp.dot, preferred_element_type=F32)
    return d(g_hi, x_hi) + (d(g_hi, x_lo) + d(g_lo, x_hi))


def _gdot(g_hi, g_lo, x):
    if g_lo is None:
        return jnp.dot(g_hi, x.astype(BF16), preferred_element_type=F32)
    return _dot3c(g_hi, g_lo, x)


def _stage_one(x_ref, g_hi_ref, g_lo_ref, asc_ref, *, n2, nt1, n1, pitch):
    def body(t2, c):
        xs = x_ref[pl.ds(t2, nt1, stride=n2), :]
        g_lo = None if g_lo_ref is None else g_lo_ref[t2]
        asc_ref[pl.ds(t2, 2 * n1, stride=pitch), :] = _gdot(g_hi_ref[t2], g_lo, xs)
        return c
    lax.fori_loop(0, n2, body, 0, unroll=4)


def _plane(asc_ref, k1, n1, n2, pitch):
    o_re = pl.multiple_of(k1 * pitch, 8)
    o_im = pl.multiple_of((n1 + k1) * pitch, 8)
    return o_re, o_im, jnp.concatenate([asc_ref[pl.ds(o_re, n2), :], asc_ref[pl.ds(o_im, n2), :]], axis=0)


def _fused_filter_body(f_ref, g_hi_ref, g_lo_ref, fb_hi_ref, fb_lo_ref, h_ref, asc_ref, *, n1, n2, pitch):
    _stage_one(f_ref, g_hi_ref, g_lo_ref, asc_ref, n2=n2, nt1=n1, n1=n1, pitch=pitch)

    def body(k1, c):
        _, _, ain = _plane(asc_ref, k1, n1, n2, pitch)
        h_ref[k1] = _dot3c(fb_hi_ref[...], fb_lo_ref[...], ain)
        return c
    lax.fori_loop(0, n1, body, 0, unroll=2)


def _fused_filter_call(f, plan):
    order, big, c = f.shape
    n1, n2 = plan['n1'], plan['n2']
    pitch = _plane_pitch(n2)
    g_hi, g_lo = _hilo(plan['ga'])
    fb_hi, fb_lo = _hilo(plan['fb'])
    const = lambda a: pl.BlockSpec(a.shape, lambda o, j: (0,) * a.ndim)
    return pl.pallas_call(
        functools.partial(_fused_filter_body, n1=n1, n2=n2, pitch=pitch),
        out_shape=jax.ShapeDtypeStruct((order, c // LANES, n1, 2 * n2, LANES), F32),
        grid=(order, c // LANES),
        in_specs=[pl.BlockSpec((None, big, LANES), lambda o, j: (o, 0, j)),
                  const(g_hi), const(g_lo), const(fb_hi), const(fb_lo)],
        out_specs=pl.BlockSpec((None, None, n1, 2 * n2, LANES), lambda o, j: (o, j, 0, 0, 0)),
        scratch_shapes=[pltpu.VMEM((2 * n1 * pitch, LANES), F32)],
        compiler_params=_cparams(("parallel", "parallel"), VMEM_LIMIT),
        name="hyena_filter_spectrum",
    )(f, g_hi, g_lo, fb_hi, fb_lo)


def _fused_conv_body(x_ref, xg_ref, skip_ref, h_ref, g_ref, fb_ref, fbi_ref, o_ref, asc_ref, y_ref,
                     *, n1, n2, pitch):
    h1 = n1 // 2
    _stage_one(x_ref, g_ref, None, asc_ref, n2=n2, nt1=h1, n1=n1, pitch=pitch)

    def mid(k1, c):
        o_re, o_im, ain = _plane(asc_ref, k1, n1, n2, pitch)
        b = _gdot(fb_ref[...], None, ain)
        h = h_ref[k1]
        br, bi, hr, hi = b[:n2], b[n2:], h[:n2], h[n2:]
        z = jnp.concatenate([br * hr - bi * hi, br * hi + bi * hr], axis=0)
        cc = _gdot(fbi_ref[...], None, z)
        asc_ref[pl.ds(o_re, n2), :] = cc[:n2]
        asc_ref[pl.ds(o_im, n2), :] = cc[n2:]
        return c
    lax.fori_loop(0, n1, mid, 0, unroll=8)

    def last(t2, c):
        zin = asc_ref[pl.ds(t2, 2 * n1, stride=pitch), :]
        y_ref[pl.ds(t2, h1, stride=n2), :] = lax.dot_general(
            g_ref[t2], zin.astype(BF16), (((0,), (0,)), ((), ())), preferred_element_type=F32)
        return c
    lax.fori_loop(0, n2, last, 0, unroll=4)
    x = x_ref[...]
    o_ref[...] = xg_ref[...] * (y_ref[...] + x * skip_ref[...])


def _fused_conv_call(x, xg, skip, hspec, plan, row0_x, row0_g, bsz, n):
    assert row0_x % n == 0 and row0_g % n == 0
    c = x.shape[1]
    n1, n2 = plan['n1'], plan['n2']
    h1 = n1 // 2
    pitch = _plane_pitch(n2)
    g_hi = _hilo(plan['ga'][:, :, :h1])[0]
    fb_hi = _hilo(plan['fb'])[0]
    fbi_hi = _hilo(plan['fb_inv'])[0]
    const = lambda a: pl.BlockSpec(a.shape, lambda j, b: (0,) * a.ndim)
    seq = lambda row0: pl.BlockSpec((n, LANES), lambda j, b: (row0 // n + b, j))
    return pl.pallas_call(
        functools.partial(_fused_conv_body, n1=n1, n2=n2, pitch=pitch),
        out_shape=jax.ShapeDtypeStruct((bsz * n, c), F32),
        grid=(c // LANES, bsz),
        in_specs=[seq(row0_x), seq(row0_g), pl.BlockSpec((1, LANES), lambda j, b: (0, j)),
                  pl.BlockSpec((None, n1, 2 * n2, LANES), lambda j, b: (j, 0, 0, 0)),
                  const(g_hi), const(fb_hi), const(fbi_hi)],
        out_specs=pl.BlockSpec((n, LANES), lambda j, b: (b, j)),
        scratch_shapes=[pltpu.VMEM((2 * n1 * pitch, LANES), F32), pltpu.VMEM((n, LANES), F32)],
        compiler_params=_cparams(("parallel", "parallel"), VMEM_LIMIT),
        name="hyena_conv_long",
    )(x, xg, skip.reshape(1, c), hspec, g_hi, fb_hi, fbi_hi)


def _hy_gate_body(x_ref, y_ref, z_ref, s_ref, o_ref):
    o_ref[...] = x_ref[...] * (y_ref[...] + z_ref[...] * s_ref[...])


def _hy_gate_call(x, y, z, skip):
    t, c = x.shape
    tok = pl.BlockSpec((512, c), lambda i: (i, 0))
    return pl.pallas_call(
        _hy_gate_body,
        out_shape=jax.ShapeDtypeStruct((t, c), F32),
        grid=(t // 512,),
        in_specs=[tok, tok, tok, pl.BlockSpec((1, c), lambda i: (0, 0))],
        out_specs=tok,
        compiler_params=_cparams(("parallel",)),
        name="hyena_gate",
    )(x, y, z, skip.reshape(1, c))


def _hyena_branch(hy, p, groups, n_ctx_blocks):
    u, x1, x2 = _hy_pre_call(hy, p['hy_conv_w'], p['hy_conv_b'], n_ctx_blocks)
    outs = []
    for start, bsz, n in groups:
        rows = bsz * n
        plan = _dft_plan(n)
        hmul = _long_conv_setup(n, _hy_filter_call(n, p))
        if plan['stages'] == 1:
            ug, x1g, x2g = (a[start:start + rows] for a in (u, x1, x2))
            y = _long_conv(ug.reshape(bsz, n, HY_WIDTH), hmul[0]).reshape(rows, HY_WIDTH)
            z = _hy_gate_call(x1g, y, ug, p['hy_skip'][0])
            y = _long_conv(z.reshape(bsz, n, HY_WIDTH), hmul[1]).reshape(rows, HY_WIDTH)
            outs.append(_hy_gate_call(x2g, y, z, p['hy_skip'][1]))
        else:
            z = _fused_conv_call(u, x1, p['hy_skip'][0], hmul[0], plan, start, start, bsz, n)
            outs.append(_fused_conv_call(z, x2, p['hy_skip'][1], hmul[1], plan, 0, start, bsz, n))
    return jnp.concatenate(outs, axis=0)


def _merge_body(x_ref, yfw_ref, ybw_ref, bonus_ref, g_ref, yb_ref, gt_ref, g1_ref, sh2_ref, sc2_ref, seg_ref,
                gng_ref, gnb_ref, wpa_ref, wpb_ref, wout_ref, lng_ref, lnb_ref, rwh_ref, rwl_ref, rb_ref,
                x1_ref, h2_ref, ti_ref, tg_ref, *, alpha):
    seg = seg_ref[...]
    y = yfw_ref[...] + ybw_ref[...]
    inv = 1.0 / RW_HEAD
    mu = _dot_x3(y, seg) * inv
    yc = y - mu
    var = _dot_x3(yc * yc, seg) * inv
    yn = yc * lax.rsqrt(var + GN_EPS) * gng_ref[...] + gnb_ref[...]
    y_a = (yn + bonus_ref[...]) * g_ref[...]
    gt = gt_ref[...]
    merged = gt[:, :D_MODEL] * _bdot(y_a, wpa_ref[...]) + gt[:, D_MODEL:] * _bdot(yb_ref[...], wpb_ref[...])
    mix = _bdot(merged, wout_ref[...])
    x1 = _layer_norm(alpha * x_ref[...] + g1_ref[...] * mix, lng_ref[...], lnb_ref[...])
    x1_ref[...] = x1
    h2 = x1 * (1.0 + sc2_ref[...]) + sh2_ref[...]
    _store_token_tiles(h2_ref, 0, h2)
    h_hi, h_lo = _split2(h2)
    d = functools.partial(jnp.dot, preferred_element_type=F32)
    logits = d(h_hi, rwh_ref[...]) + (d(h_hi, rwl_ref[...]) + d(h_lo, rwh_ref[...])) + rb_ref[...]
    lane = lax.broadcasted_iota(jnp.int32, logits.shape, 1)
    neg = jnp.float32(-jnp.inf)
    cur = jnp.where(lane < N_EXPERTS, logits, neg)
    top_i = jnp.zeros(logits.shape, jnp.int32)
    top_e = jnp.zeros(logits.shape, F32)
    den = jnp.zeros((logits.shape[0], 1), F32)
    v0 = None
    for j in range(TOP_K):
        mx = jnp.max(cur, axis=-1, keepdims=True)
        idx = jnp.min(jnp.where(cur == mx, lane, LANES), axis=-1, keepdims=True)
        if j == 0:
            v0 = mx
        e = jnp.exp(mx - v0)
        den = den + e
        top_i = jnp.where(lane == j, idx, top_i)
        top_e = jnp.where(lane == j, e, top_e)
        cur = jnp.where(lane == idx, neg, cur)
    ti_ref[...] = top_i
    tg_ref[...] = top_e / den


def _merge_call(x, yfw, ybw, bonus, g, yb, gates, mod_l, p, seg, cond_idx, alpha):
    t = x.shape[0]
    w = RW_WIDTH
    tok = lambda n: pl.BlockSpec((TOKEN_BLOCK, n), lambda i: (i, 0))
    modspec = lambda j: pl.BlockSpec((None, 1, D_MODEL), lambda i: (cond_idx(i), 0, j))
    full = lambda *s: pl.BlockSpec(s, lambda i: (0,) * len(s))
    rw_pad = jnp.zeros((D_MODEL, LANES), F32).at[:, :N_EXPERTS].set(p['router_w'])
    rw_hi = rw_pad.astype(BF16)
    rw_lo = (rw_pad - rw_hi.astype(F32)).astype(BF16)
    rb = jnp.zeros((1, LANES), F32).at[0, :N_EXPERTS].set(p['router_b'])
    o = jax.ShapeDtypeStruct((t, D_MODEL), F32)
    return pl.pallas_call(
        functools.partial(_merge_body, alpha=alpha),
        out_shape=(o, jax.ShapeDtypeStruct((t * TOKEN_TILE, LANES), F32),
                   jax.ShapeDtypeStruct((t, LANES), jnp.int32), jax.ShapeDtypeStruct((t, LANES), F32)),
        grid=(t // TOKEN_BLOCK,),
        in_specs=[tok(D_MODEL), tok(w), tok(w), tok(w), tok(w), tok(w),
                  tok(2 * D_MODEL), modspec(2), modspec(3), modspec(4), full(w, w), full(1, w), full(1, w),
                  full(w, D_MODEL), full(w, D_MODEL), full(D_MODEL, D_MODEL), full(1, D_MODEL), full(1, D_MODEL),
                  full(D_MODEL, LANES), full(D_MODEL, LANES), full(1, LANES)],
        out_specs=(tok(D_MODEL), pl.BlockSpec((TOKEN_BLOCK * TOKEN_TILE, LANES), lambda i: (i, 0)),
                   tok(LANES), tok(LANES)),
        compiler_params=_cparams(("parallel",), VMEM_LIMIT),
        name="merge_ln_router",
    )(x, yfw, ybw, bonus, g, yb, gates, mod_l, mod_l, mod_l, seg, p['gn_g'].reshape(1, w), p['gn_b'].reshape(1, w),
      p['w_pa'].astype(BF16), p['w_pb'].astype(BF16), p['w_out'].astype(BF16),
      p['ln1_g'].reshape(1, D_MODEL), p['ln1_b'].reshape(1, D_MODEL), rw_hi, rw_lo, rb)


DEINT_COLS = 256


def _deint_body(w_ref, p_ref, g_ref, l_ref):
    y = jnp.dot(w_ref[...].astype(BF16), p_ref[...], preferred_element_type=F32)
    half = DEINT_COLS // 2
    g_ref[...] = y[:, :half].astype(BF16)
    l_ref[...] = y[:, half:].astype(BF16)


def _deint_call(w, layer):
    _, e, k, n2 = w.shape
    half = DEINT_COLS // 2
    sel = np.zeros((DEINT_COLS, DEINT_COLS), np.float32)
    sel[2 * np.arange(half), np.arange(half)] = 1.0
    sel[2 * np.arange(half) + 1, half + np.arange(half)] = 1.0
    o = jax.ShapeDtypeStruct((e, k, n2 // 2), BF16)
    return pl.pallas_call(
        _deint_body,
        out_shape=(o, o),
        grid=(e, n2 // DEINT_COLS),
        in_specs=[pl.BlockSpec((None, None, k, DEINT_COLS), lambda i, j: (layer, i, 0, j)),
                  pl.BlockSpec((DEINT_COLS, DEINT_COLS), lambda i, j: (0, 0))],
        out_specs=(pl.BlockSpec((None, k, half), lambda i, j: (i, 0, j)),) * 2,
        compiler_params=_cparams(("parallel", "parallel")),
        name="expert_w_split",
    )(w, jnp.asarray(sel, BF16))


def _moe_body(blk_e_ref, n_on_ref, tok_ref, tokn_ref, dst_ref, h_hbm, wg_ref, wl_ref, bg_ref, bl_ref, wdn_ref,
              bdn_ref, y_hbm, xbuf, ybuf, sem_in, sem_out, *, n_real):
    i = pl.program_id(0)
    n_on = n_on_ref[0]
    slot = i % 2
    tt = TOKEN_TILE
    slot_rows = MOE_ROWS * tt

    def tile(ref, idx):
        return ref.at[pl.ds(pl.multiple_of(idx * tt, tt), tt), :]

    def gather_start(tref, s):
        def body(r, c):
            pltpu.make_async_copy(tile(h_hbm, tref[0, r]), tile(xbuf, s * MOE_ROWS + r), sem_in.at[s]).start()
            return c
        lax.fori_loop(0, MOE_ROWS, body, 0, unroll=8)

    def slot_buf(buf, s):
        return buf.at[pl.ds(pl.multiple_of(s * slot_rows, slot_rows), slot_rows), :]

    def gather_wait(s):
        pltpu.make_async_copy(h_hbm.at[pl.ds(0, slot_rows), :], slot_buf(xbuf, s), sem_in.at[s]).wait()

    def scatter_start(s):
        def body(r, c):
            pltpu.make_async_copy(tile(ybuf, s * MOE_ROWS + r), tile(y_hbm, dst_ref[0, r]), sem_out.at[s]).start()
            return c
        lax.fori_loop(0, MOE_ROWS, body, 0, unroll=8)

    def scatter_wait(s):
        pltpu.make_async_copy(slot_buf(ybuf, s), y_hbm.at[pl.ds(0, slot_rows), :], sem_out.at[s]).wait()

    @pl.when(i == 0)
    def _():
        ybuf[...] = jnp.zeros_like(ybuf)
        fills = [pltpu.make_async_copy(slot_buf(ybuf, s),
                                       y_hbm.at[pl.ds((n_real + s * MOE_ROWS) * tt, slot_rows), :],
                                       sem_out.at[s]) for s in range(2)]
        for cp in fills:
            cp.start()
        for cp in fills:
            cp.wait()

    @pl.when(jnp.logical_and(i == 0, n_on > 0))
    def _():
        gather_start(tok_ref, 0)

    @pl.when(i < n_on)
    def _():
        @pl.when(i + 1 < n_on)
        def _():
            gather_start(tokn_ref, 1 - slot)

        gather_wait(slot)
        row0 = slot * slot_rows
        x = _load_token_tiles(xbuf, row0, MOE_ROWS).astype(BF16)
        d = functools.partial(jnp.dot, preferred_element_type=F32)
        glu = jnp.minimum(d(x, wg_ref[...]) + bg_ref[...], SWIGLU_LIMIT)
        lin = jnp.clip(d(x, wl_ref[...]) + bl_ref[...], -SWIGLU_LIMIT, SWIGLU_LIMIT)
        act = glu * _sigmoid(SWIGLU_ALPHA * glu) * (lin + 1.0)
        _store_token_tiles(ybuf, row0, d(act.astype(BF16), wdn_ref[...]) + bdn_ref[...])
        scatter_start(slot)

        @pl.when(i >= 1)
        def _():
            scatter_wait(1 - slot)

        @pl.when(i == n_on - 1)
        def _():
            scatter_wait(slot)


def _moe_call(h2, top_i, wg, wl, bg, bl, wdn, bdn):
    t = h2.shape[0] // TOKEN_TILE
    m = t * TOP_K
    e = N_EXPERTS
    blk = MOE_ROWS
    flat_e = top_i.reshape(-1)
    order = jnp.argsort(flat_e, stable=True).astype(jnp.int32)
    sizes = jnp.bincount(flat_e, length=e).astype(jnp.int32)
    padded = (sizes + blk - 1) // blk * blk
    pad_end = jnp.cumsum(padded)
    pad_start = pad_end - padded
    grp_start = jnp.cumsum(sizes) - sizes
    n_blocks = -(-(m + e * (blk - 1)) // blk)
    blk_first = jnp.arange(n_blocks, dtype=jnp.int32) * blk
    blk_e = jnp.minimum(jnp.sum((pad_end[None, :] <= blk_first[:, None]).astype(jnp.int32), axis=1), e - 1)
    pidx = jnp.arange(n_blocks * blk, dtype=jnp.int32)
    e_p = jnp.repeat(blk_e, blk)
    idx = pidx - pad_start[e_p]
    valid = idx < sizes[e_p]
    assign = order[jnp.clip(grp_start[e_p] + idx, 0, m - 1)]
    tok_row = jnp.where(valid, assign // TOP_K, 0).astype(jnp.int32)
    spare = m + ((pidx // blk) % 2) * blk + pidx % blk
    dst_row = jnp.where(valid, (assign % TOP_K) * t + assign // TOP_K, spare).astype(jnp.int32)
    n_on = (pad_end[-1] // blk).astype(jnp.int32).reshape(1)
    tok3 = tok_row.reshape(n_blocks, 1, blk)

    smem = lambda f: pl.BlockSpec((None, 1, blk), f, memory_space=pltpu.SMEM)
    wspec = lambda a, b: pl.BlockSpec((None, a, b), lambda i, be, no: (be[i], 0, 0))
    gs = pltpu.PrefetchScalarGridSpec(
        num_scalar_prefetch=2,
        grid=(n_blocks,),
        in_specs=[smem(lambda i, be, no: (i, 0, 0)),
                  smem(lambda i, be, no: (jnp.minimum(i + 1, n_blocks - 1), 0, 0)),
                  smem(lambda i, be, no: (i, 0, 0)),
                  pl.BlockSpec(memory_space=pl.ANY),
                  wspec(D_MODEL, D_FF), wspec(D_MODEL, D_FF), wspec(1, D_FF), wspec(1, D_FF),
                  wspec(D_FF, D_MODEL), wspec(1, D_MODEL)],
        out_specs=pl.BlockSpec(memory_space=pl.ANY),
        scratch_shapes=[pltpu.VMEM((2 * blk * TOKEN_TILE, LANES), F32), pltpu.VMEM((2 * blk * TOKEN_TILE, LANES), F32),
                        pltpu.SemaphoreType.DMA((2,)), pltpu.SemaphoreType.DMA((2,))],
    )
    return pl.pallas_call(
        functools.partial(_moe_body, n_real=m),
        out_shape=jax.ShapeDtypeStruct(((m + 2 * blk) * TOKEN_TILE, LANES), F32),
        grid_spec=gs,
        compiler_params=_cparams(("arbitrary",), VMEM_LIMIT),
        name="moe_experts",
    )(blk_e, n_on, tok3, tok3, dst_row.reshape(n_blocks, 1, blk), h2, wg, wl, bg.reshape(e, 1, D_FF),
      bl.reshape(e, 1, D_FF), wdn, bdn.reshape(e, 1, D_MODEL))


def _combine_body(x1_ref, y0_ref, y1_ref, y2_ref, y3_ref, tg_ref, g2_ref, lng_ref, lnb_ref, o_ref, *, alpha):
    tg = tg_ref[...]
    rows = x1_ref.shape[0]
    moe = tg[:, 0:1] * _load_token_tiles(y0_ref, 0, rows)
    for j, y_ref in enumerate((y1_ref, y2_ref, y3_ref), start=1):
        moe = moe + tg[:, j:j + 1] * _load_token_tiles(y_ref, 0, rows)
    o_ref[...] = _layer_norm(alpha * x1_ref[...] + g2_ref[...] * moe, lng_ref[...], lnb_ref[...])


def _combine_call(x1, yexp, tg, mod_l, p, cond_idx, alpha):
    t = x1.shape[0]
    nb = t // TOKEN_BLOCK
    tok = lambda n: pl.BlockSpec((TOKEN_BLOCK, n), lambda i: (i, 0))
    full = lambda *s: pl.BlockSpec(s, lambda i: (0,) * len(s))
    yspec = [pl.BlockSpec((TOKEN_BLOCK * TOKEN_TILE, LANES), lambda i, j=j: (j * nb + i, 0)) for j in range(TOP_K)]
    return pl.pallas_call(
        functools.partial(_combine_body, alpha=alpha),
        out_shape=jax.ShapeDtypeStruct((t, D_MODEL), F32),
        grid=(nb,),
        in_specs=[tok(D_MODEL)] + yspec + [tok(LANES),
                  pl.BlockSpec((None, 1, D_MODEL), lambda i: (cond_idx(i), 0, 5)),
                  full(1, D_MODEL), full(1, D_MODEL)],
        out_specs=tok(D_MODEL),
        compiler_params=_cparams(("parallel",), VMEM_LIMIT),
        name="moe_combine_ln",
    )(x1, yexp, yexp, yexp, yexp, tg, mod_l, p['ln2_g'].reshape(1, D_MODEL), p['ln2_b'].reshape(1, D_MODEL))


def kernel(x_prompt, x_sample, c, state_rwkv, c_ctx, w_mod, b_mod, w_in, mu_shift, w0, w_lora_up, a0, a_lora_up, g_up, k_k, k_a, r_k, gn_g, gn_b, hy_conv_w, hy_conv_b, hy_f_w1, hy_f_b1, hy_f_w2, hy_f_b2, hy_f_freq, hy_f_w3, hy_skip, w_pa, w_pb, w_out, ln1_g, ln1_b, ln2_g, ln2_b, router_w, router_b, ex_w_up, ex_b_up, ex_w_down, ex_b_down):
    bsz, seq, dm = x_prompt.shape
    dbsz, dseq, _ = x_sample.shape
    depth = w_mod.shape[0]
    assert dm == D_MODEL and seq == TOKEN_BLOCK and dseq % TOKEN_BLOCK == 0 and TOKEN_BLOCK % GRID_W == 0
    assert 1 + dbsz <= 8
    alpha = (2 * depth) ** 0.25
    t_ctx = bsz * seq
    n_ctx_blocks = t_ctx // TOKEN_BLOCK
    lat_blocks = dseq // TOKEN_BLOCK

    def cond_idx(i):
        return jnp.where(i < n_ctx_blocks, 0, 1 + (i - n_ctx_blocks) // lat_blocks)

    x = jnp.concatenate([x_prompt.reshape(t_ctx, dm), x_sample.reshape(dbsz * dseq, dm)], axis=0)
    cond8 = jnp.zeros((8, dm), F32).at[0].set(c_ctx).at[1:1 + dbsz].set(c)
    mod = _mod_call(cond8, w_mod, b_mod)

    hd = RW_HEAD
    seg = (np.arange(RW_WIDTH)[:, None] // hd == np.arange(RW_WIDTH)[None, :] // hd)
    seg = jnp.asarray(seg, BF16)
    seq_lens = [seq] * bsz + [dseq] * dbsz
    tbl_np, nsteps = _scan_table(seq_lens)
    tbl = jnp.asarray(tbl_np)
    groups = [(0, bsz, seq), (t_ctx, dbsz, dseq)]

    new_states = []
    for l in range(depth):
        p = dict(mu_shift=mu_shift[l], w0=w0[l], w_lora_up=w_lora_up[l], a0=a0[l], a_lora_up=a_lora_up[l],
                 g_up=g_up[l], k_k=k_k[l], k_a=k_a[l], r_k=r_k[l], gn_g=gn_g[l], gn_b=gn_b[l],
                 hy_conv_w=hy_conv_w[l], hy_conv_b=hy_conv_b[l], hy_f_w1=hy_f_w1[l], hy_f_b1=hy_f_b1[l],
                 hy_f_w2=hy_f_w2[l], hy_f_b2=hy_f_b2[l], hy_f_freq=hy_f_freq[l], hy_f_w3=hy_f_w3[l],
                 hy_skip=hy_skip[l], w_pa=w_pa[l], w_pb=w_pb[l], w_out=w_out[l], ln1_g=ln1_g[l], ln1_b=ln1_b[l],
                 ln2_g=ln2_g[l], ln2_b=ln2_b[l], router_w=router_w[l], router_b=router_b[l])
        mod_l = mod[l].reshape(8, 1, 6 * dm)
        rw, hy, gates = _inproj_call(x, mod_l, w_in[l].astype(BF16), cond_idx)
        r, kk, v, lw, b, kd, g, bonus = _rwkv_pre_call(rw, p, seg, n_ctx_blocks)
        s0 = jnp.concatenate([jnp.zeros((bsz, 2, RW_HEADS, hd, hd), F32), state_rwkv[:, l].astype(F32)], axis=0)
        yfw, ybw, sfin = _scan_call(tbl, nsteps, bsz + dbsz, r, kk, v, lw, b, kd, s0)
        new_states.append(sfin[:bsz].astype(x_prompt.dtype))
        yb = _hyena_branch(hy, p, groups, n_ctx_blocks)
        x1, h2, top_i, top_g = _merge_call(x, yfw, ybw, bonus, g, yb, gates, mod_l, p, seg, cond_idx, alpha)
        wg, wl = _deint_call(ex_w_up, l)
        yexp = _moe_call(h2, top_i[:, :TOP_K], wg, wl, ex_b_up[l][:, 0::2], ex_b_up[l][:, 1::2],
                         ex_w_down[l].astype(BF16), ex_b_down[l])
        x = _combine_call(x1, yexp, top_g, mod_l, p, cond_idx, alpha)

    y_p = x[:t_ctx].reshape(bsz, seq, dm)
    y_s = x[t_ctx:].reshape(dbsz, dseq, dm)
    return (y_p, y_s, jnp.stack(new_states, axis=1))
```

```python
import functools
import math

import numpy as np
import jax
import jax.numpy as jnp
from jax import lax
from jax.experimental import pallas as pl
from jax.experimental.pallas import tpu as pltpu

F32 = jnp.float32
BF16 = jnp.bfloat16

D_MODEL = 1024
GRID_W = 64
RW_WIDTH = 512
RW_HEAD = 64
RW_HEADS = RW_WIDTH // RW_HEAD
LORA_W = 64
LORA_A = 64
LORA_G = 128
GN_EPS = 64e-5
HY_WIDTH = 512
HY_ORDER = 2
HY_BANDS = 16
HY_EMB = 2 * HY_BANDS + 1
HY_FFN = 64
HY_DECAY_TARGET = 1e-2
HY_DECAY_SHORT_PCT = 0.3
HY_DECAY_LONG_PCT = 1.5
RW_COLS = 3 * RW_WIDTH + LORA_W + LORA_A + LORA_G
HY_COLS = (HY_ORDER + 1) * HY_WIDTH
IN_COLS = RW_COLS + HY_COLS + 2 * D_MODEL
N_EXPERTS = 32
TOP_K = 4
D_FF = D_MODEL
SWIGLU_ALPHA = 1.702
SWIGLU_LIMIT = 7.0
LN_EPS = 1e-5

TOKEN_BLOCK = 256
SCAN_CHUNK = 64
INV_BLOCK = 16
DFT_N1 = 128
MOE_ROWS = 512
LANES = 128
VMEM_LIMIT = 56 * 1024 * 1024


def _cparams(sem, vmem=None):
    return pltpu.CompilerParams(dimension_semantics=sem, vmem_limit_bytes=vmem)


def _bdot(a, b):
    return jnp.dot(a.astype(BF16), b.astype(BF16), preferred_element_type=F32)


def _bdg(a, b, ca, cb):
    return lax.dot_general(a.astype(BF16), b.astype(BF16), (((ca,), (cb,)), ((0,), (0,))), preferred_element_type=F32)


def _bmm(a, b):
    return _bdg(a, b, 2, 1)


def _bmm_nt(a, b):
    return _bdg(a, b, 2, 2)


def _bmm_tn(a, b):
    return _bdg(a, b, 1, 1)


def _split2(x):
    hi = x.astype(BF16)
    lo = (x - hi.astype(F32)).astype(BF16)
    return hi, lo


def _split3(x):
    hi = x.astype(BF16)
    r1 = x - hi.astype(F32)
    mid = r1.astype(BF16)
    lo = (r1 - mid.astype(F32)).astype(BF16)
    return hi, mid, lo


def _dot_x3(x, g_bf16):
    hi, mid, lo = _split3(x)
    d = functools.partial(jnp.dot, preferred_element_type=F32)
    return d(hi, g_bf16) + (d(mid, g_bf16) + d(lo, g_bf16))


def _dot3(x, g_hi, g_lo):
    x_hi, x_lo = _split2(x)
    d = functools.partial(jnp.dot, preferred_element_type=F32)
    return d(x_hi, g_hi) + (d(x_hi, g_lo) + d(x_lo, g_hi))


def _sigmoid(x):
    return 1.0 / (1.0 + jnp.exp(-x))


TOKEN_TILE = D_MODEL // LANES


def _store_token_tiles(ref, row0, x):
    rows = x.shape[0]
    for s in range(TOKEN_TILE):
        ref[pl.ds(row0 + s, rows, stride=TOKEN_TILE), :] = x[:, s * LANES:(s + 1) * LANES]


def _load_token_tiles(ref, row0, rows):
    return jnp.concatenate([ref[pl.ds(row0 + s, rows, stride=TOKEN_TILE), :] for s in range(TOKEN_TILE)], axis=1)


def _layer_norm(x, g, b):
    mu = jnp.mean(x, axis=-1, keepdims=True)
    xc = x - mu
    var = jnp.mean(xc * xc, axis=-1, keepdims=True)
    return xc * lax.rsqrt(var + LN_EPS) * g + b


def _mod_body(c_ref, w_ref, b_ref, o_ref):
    c = c_ref[...]
    o_ref[...] = _bdot(c * _sigmoid(c), w_ref[...]) + b_ref[...]


def _mod_call(cond8, w_mod, b_mod):
    depth = w_mod.shape[0]
    tn = 1536
    return pl.pallas_call(
        _mod_body,
        out_shape=jax.ShapeDtypeStruct((depth, 8, 6 * D_MODEL), F32),
        grid=(depth, 6 * D_MODEL // tn),
        in_specs=[pl.BlockSpec((8, D_MODEL), lambda l, j: (0, 0)),
                  pl.BlockSpec((None, D_MODEL, tn), lambda l, j: (l, 0, j)),
                  pl.BlockSpec((None, 1, tn), lambda l, j: (l, 0, j))],
        out_specs=pl.BlockSpec((None, 8, tn), lambda l, j: (l, 0, j)),
        compiler_params=_cparams(("parallel", "parallel"), VMEM_LIMIT),
        name="adaln_mod",
    )(cond8, w_mod, b_mod.reshape(depth, 1, 6 * D_MODEL))


def _inproj_body(x_ref, sh_ref, sc_ref, w_ref, rw_ref, hy_ref, gt_ref):
    h = (x_ref[...] * (1.0 + sc_ref[...]) + sh_ref[...]).astype(BF16)
    d = functools.partial(jnp.dot, preferred_element_type=F32)
    rw_ref[...] = d(h, w_ref[:, :RW_COLS])
    hy_ref[...] = d(h, w_ref[:, RW_COLS:RW_COLS + HY_COLS])
    gt_ref[...] = _sigmoid(d(h, w_ref[:, RW_COLS + HY_COLS:]))


def _inproj_call(x, mod_l, w_in_bf16, cond_idx):
    t = x.shape[0]
    nb = t // TOKEN_BLOCK
    tok = lambda n: pl.BlockSpec((TOKEN_BLOCK, n), lambda i: (i, 0))
    modspec = lambda j: pl.BlockSpec((None, 1, D_MODEL), lambda i: (cond_idx(i), 0, j))
    return pl.pallas_call(
        _inproj_body,
        out_shape=(jax.ShapeDtypeStruct((t, RW_COLS), F32), jax.ShapeDtypeStruct((t, HY_COLS), F32),
                   jax.ShapeDtypeStruct((t, 2 * D_MODEL), F32)),
        grid=(nb,),
        in_specs=[tok(D_MODEL), modspec(0), modspec(1),
                  pl.BlockSpec((D_MODEL, IN_COLS), lambda i: (0, 0))],
        out_specs=(tok(RW_COLS), tok(HY_COLS), tok(2 * D_MODEL)),
        compiler_params=_cparams(("parallel",), VMEM_LIMIT),
        name="in_proj",
    )(x, mod_l, mod_l, w_in_bf16)


def _neighbours(x, n_ctx_blocks):
    rows = x.shape[0]
    row = lax.broadcasted_iota(jnp.int32, (rows, 1), 0)
    seg_mask = jnp.where(pl.program_id(0) < n_ctx_blocks, rows - 1, GRID_W - 1)
    pos = row & seg_mask
    prev = jnp.where(pos == 0, 0.0, pltpu.roll(x, 1, 0))
    nxt = jnp.where(pos == seg_mask, 0.0, pltpu.roll(x, rows - 1, 0))
    return prev, nxt


def _rwkv_pre_body(rw_ref, mu_ref, kk_s_ref, ka_ref, w0_ref, a0_ref, wl_ref, al_ref, gup_ref, rk_ref, seg_ref,
                   r_ref, kk_ref, v_ref, lw_ref, b_ref, kd_ref, g_ref, bonus_ref, *, n_ctx_blocks):
    x = rw_ref[...]
    prev, nxt = _neighbours(x, n_ctx_blocks)
    cols = x + mu_ref[...] * (0.5 * (prev + nxt) - x)
    w = RW_WIDTH
    r = cols[:, :w]
    k = cols[:, w:2 * w]
    v = cols[:, 2 * w:3 * w]
    wd = cols[:, 3 * w:3 * w + LORA_W]
    ad = cols[:, 3 * w + LORA_W:3 * w + LORA_W + LORA_A]
    gd = cols[:, 3 * w + LORA_W + LORA_A:]
    seg = seg_ref[...]
    kkr = k * kk_s_ref[...]
    ss = _dot_x3(kkr * kkr, seg)
    kk = kkr / jnp.maximum(jnp.sqrt(ss), 1e-12)
    r_ref[...] = r
    kk_ref[...] = kk
    v_ref[...] = v
    g_ref[...] = _bdot(_sigmoid(gd), gup_ref[...])
    tw = jnp.tanh(wd)
    bonus = jnp.zeros_like(r)
    for d in range(2):
        w_logit = w0_ref[d:d + 1, :] + _bdot(tw, wl_ref[d])
        lw_ref[d] = -math.exp(-0.5) * _sigmoid(w_logit)
        a = _sigmoid(a0_ref[d:d + 1, :] + _bdot(ad, al_ref[d]))
        kd = k * (1.0 + (a - 1.0) * ka_ref[...])
        kd_ref[d] = kd
        b_ref[d] = kk * a
        bonus = bonus + _dot_x3(r * kd * rk_ref[d:d + 1, :], seg) * v
    bonus_ref[...] = bonus


def _rwkv_pre_call(rw, p, seg, n_ctx_blocks):
    t = rw.shape[0]
    nb = t // TOKEN_BLOCK
    w = RW_WIDTH
    tok = lambda n: pl.BlockSpec((TOKEN_BLOCK, n), lambda i: (i, 0))
    tok2 = pl.BlockSpec((2, TOKEN_BLOCK, w), lambda i: (0, i, 0))
    full = lambda *s: pl.BlockSpec(s, lambda i: (0,) * len(s))
    o1 = jax.ShapeDtypeStruct((t, w), F32)
    o2 = jax.ShapeDtypeStruct((2, t, w), F32)
    return pl.pallas_call(
        functools.partial(_rwkv_pre_body, n_ctx_blocks=n_ctx_blocks),
        out_shape=(o1, o1, o1, o2, o2, o2, o1, o1),
        grid=(nb,),
        in_specs=[tok(RW_COLS), full(1, RW_COLS), full(1, w), full(1, w), full(2, w), full(2, w),
                  full(2, LORA_W, w), full(2, LORA_A, w), full(LORA_G, w), full(2, w), full(w, w)],
        out_specs=(tok(w), tok(w), tok(w), tok2, tok2, tok2, tok(w), tok(w)),
        compiler_params=_cparams(("parallel",), VMEM_LIMIT),
        name="rwkv_pre",
    )(rw, p['mu_shift'].reshape(1, RW_COLS), p['k_k'].reshape(1, w), p['k_a'].reshape(1, w), p['w0'], p['a0'],
      p['w_lora_up'], p['a_lora_up'], p['g_up'], p['r_k'].reshape(2, w), seg)


def _scan_body(tbl_ref, rf_ref, kkf_ref, vf_ref, rb_ref, kkb_ref, vb_ref, lwf_ref, bf_ref, kdf_ref, lwb_ref,
               bb_ref, kdb_ref, s0_ref, yf_ref, yb_ref, sfin_ref, s_ref):
    base = pl.program_id(0) * 5
    c = SCAN_CHUNK
    hd = RW_HEAD
    nh = 2 * RW_HEADS

    @pl.when(tbl_ref[base + 3] == 1)
    def _():
        s_ref[...] = s0_ref[...].reshape(nh, hd, hd)

    rowi = lax.broadcasted_iota(jnp.int32, (c, c), 0)
    coli = lax.broadcasted_iota(jnp.int32, (c, c), 1)
    same_blk = (rowi // INV_BLOCK) == (coli // INV_BLOCK)
    eye = (rowi == coli).astype(F32)

    def both(fwd, bwd):
        return jnp.concatenate([jnp.broadcast_to(fwd, (RW_HEADS, c, c)), jnp.broadcast_to(bwd, (RW_HEADS, c, c))],
                               axis=0)

    diff = both(rowi - coli, coli - rowi)
    strict = diff > 0
    incl = diff >= 0

    def heads(a):
        return jnp.stack([a[:, h * hd:(h + 1) * hd] for h in range(RW_HEADS)], axis=0)

    def prep(r_ref, kk_ref, v_ref, lw_ref, b_ref, kd_ref, tri):
        lw = lw_ref[...]
        hi, mid, lo = _split3(lw)
        dd = functools.partial(jnp.dot, preferred_element_type=F32)
        cum = dd(tri, hi) + (dd(tri, mid) + dd(tri, lo))
        tot = jnp.sum(lw, axis=0, keepdims=True)
        e_neg = jnp.exp(-cum)
        e_rem = jnp.exp(tot - cum)
        kk = kk_ref[...]
        bb = b_ref[...]
        kd = kd_ref[...]
        return [heads(a) for a in (kk * jnp.exp(cum - lw), bb * e_neg, kd * e_neg, r_ref[...] * jnp.exp(cum),
                                   v_ref[...], kd * e_rem, bb * e_rem, jnp.exp(tot))]

    fw = prep(rf_ref, kkf_ref, vf_ref, lwf_ref, bf_ref, kdf_ref, (rowi >= coli).astype(BF16))
    bw = prep(rb_ref, kkb_ref, vb_ref, lwb_ref, bb_ref, kdb_ref, (rowi <= coli).astype(BF16))
    a_h, b_h, k_h, r_h, v_h, kapg_h, betg_h, gtot_h = [jnp.concatenate([f, b], axis=0) for f, b in zip(fw, bw)]
    ar = jnp.concatenate([a_h, r_h], axis=1)
    gb = _bmm_nt(ar, b_h)
    gk = _bmm_nt(ar, k_h)
    low = jnp.where(strict, gb[:, :c], 0.0)
    a_ka = jnp.where(strict, gk[:, :c], 0.0)
    a_br = jnp.where(incl, gb[:, c:], 0.0)
    a_kr = jnp.where(incl, gk[:, c:], 0.0)
    nd = jnp.where(same_blk, -low, 0.0)
    loff = jnp.where(same_blk, 0.0, low)
    x = eye + nd
    n2 = _bmm(nd, nd)
    x = x + _bmm(x, n2)
    n4 = _bmm(n2, n2)
    x = x + _bmm(x, n4)
    n8 = _bmm(n4, n4)
    x = x + _bmm(x, n8)
    m = _bmm(x, loff)
    m2 = _bmm(m, m)
    y1 = x + _bmm(m2, x)
    tinv = y1 - _bmm(m, y1)
    w_h = _bmm(a_ka, v_h)
    rhs = jnp.concatenate([a_h, w_h], axis=2)
    x0 = _bmm(tinv, rhs)
    res = rhs - x0 - _bmm(low, x0)
    xs = x0 + _bmm(tinv, res)
    p_h = xs[:, :, :hd]
    q_h = xs[:, :, hd:]
    s_old = s_ref[...]
    uy = _bmm_nt(jnp.concatenate([p_h, r_h], axis=1), s_old)
    u_h = uy[:, :c] + q_h
    y_h = uy[:, c:] + _bmm(a_kr, v_h) - _bmm(a_br, u_h)
    zv = jnp.concatenate([v_h, u_h], axis=1)
    zk = jnp.concatenate([kapg_h, -betg_h], axis=1)
    s_ref[...] = s_old * gtot_h + _bmm_tn(zv, zk)
    for h in range(RW_HEADS):
        yf_ref[:, h * hd:(h + 1) * hd] = y_h[h]
        yb_ref[:, h * hd:(h + 1) * hd] = y_h[RW_HEADS + h]

    @pl.when(tbl_ref[base + 4] == 1)
    def _():
        sfin_ref[...] = s_ref[...].reshape(2, RW_HEADS, hd, hd)


def _scan_call(tbl, nsteps, nseq, r, kk, v, lw, b, kd, s0):
    t = r.shape[0]
    w = RW_WIDTH
    c = SCAN_CHUNK
    blk = lambda d: pl.BlockSpec((c, w), lambda i, tb: (tb[i * 5 + d], 0))
    blk_dir = lambda d: pl.BlockSpec((None, c, w), lambda i, tb: (d, tb[i * 5 + d], 0))
    st = pl.BlockSpec((None, 2, RW_HEADS, RW_HEAD, RW_HEAD), lambda i, tb: (tb[i * 5 + 2], 0, 0, 0, 0))
    gs = pltpu.PrefetchScalarGridSpec(
        num_scalar_prefetch=1,
        grid=(nsteps,),
        in_specs=[blk(0), blk(0), blk(0), blk(1), blk(1), blk(1), blk_dir(0), blk_dir(0), blk_dir(0),
                  blk_dir(1), blk_dir(1), blk_dir(1), st],
        out_specs=(blk(0), blk(1), st),
        scratch_shapes=[pltpu.VMEM((2 * RW_HEADS, RW_HEAD, RW_HEAD), F32)],
    )
    o = jax.ShapeDtypeStruct((t, w), F32)
    return pl.pallas_call(
        _scan_body,
        out_shape=(o, o, jax.ShapeDtypeStruct((nseq, 2, RW_HEADS, RW_HEAD, RW_HEAD), F32)),
        grid_spec=gs,
        compiler_params=_cparams(("arbitrary",), VMEM_LIMIT),
        name="rwkv_scan",
    )(tbl, r, kk, v, r, kk, v, lw, b, kd, lw, b, kd, s0)


def _scan_table(seq_lens):
    c = SCAN_CHUNK
    rows = []
    start = 0
    for s, n in enumerate(seq_lens):
        nc = n // c
        for j in range(nc):
            rows.append((start + j, start + nc - 1 - j, s, int(j == 0), int(j == nc - 1)))
        start += nc
    return np.asarray(rows, np.int32).reshape(-1), len(rows)


def _hy_pre_body(hy_ref, w_ref, b_ref, u_ref, x1_ref, x2_ref, *, n_ctx_blocks):
    x = hy_ref[...]
    prev, nxt = _neighbours(x, n_ctx_blocks)
    cols = prev * w_ref[0:1, :] + x * w_ref[1:2, :] + nxt * w_ref[2:3, :] + b_ref[...]
    w = HY_WIDTH
    u_ref[...] = cols[:, :w]
    x1_ref[...] = cols[:, w:2 * w]
    x2_ref[...] = cols[:, 2 * w:]


def _hy_pre_call(hy, conv_w, conv_b, n_ctx_blocks):
    t = hy.shape[0]
    tok = lambda n: pl.BlockSpec((TOKEN_BLOCK, n), lambda i: (i, 0))
    o = jax.ShapeDtypeStruct((t, HY_WIDTH), F32)
    return pl.pallas_call(
        functools.partial(_hy_pre_body, n_ctx_blocks=n_ctx_blocks),
        out_shape=(o, o, o),
        grid=(t // TOKEN_BLOCK,),
        in_specs=[tok(HY_COLS), pl.BlockSpec((3, HY_COLS), lambda i: (0, 0)),
                  pl.BlockSpec((1, HY_COLS), lambda i: (0, 0))],
        out_specs=(tok(HY_WIDTH),) * 3,
        compiler_params=_cparams(("parallel",)),
        name="hyena_pre",
    )(hy, conv_w, conv_b.reshape(1, HY_COLS))


def _dot_f32(a, b):
    a_hi, a_lo = _split2(a)
    b_hi, b_lo = _split2(b)
    d = functools.partial(jnp.dot, preferred_element_type=F32)
    return d(a_hi, b_hi) + (d(a_hi, b_lo) + d(a_lo, b_hi))


def _hy_filter_body(bands_ref, w1t_ref, w1c_ref, w1s_ref, b1_ref, w2_ref, b2_ref, fr_ref, w3_ref, dl_ref, f_ref,
                    *, n, tile):
    row = lax.broadcasted_iota(jnp.int32, (tile, LANES), 0) + pl.program_id(0) * tile
    fwd = row < n
    pos = jnp.where(fwd, row, 2 * n - row).astype(F32)
    tcol = pos / n
    ang = (2.0 * math.pi / n) * pos * bands_ref[...]
    pre1 = (tcol[:, :HY_FFN] * w1t_ref[...] + _dot_f32(jnp.cos(ang), w1c_ref[...])
            + _dot_f32(jnp.sin(ang), w1s_ref[...]) + b1_ref[...])
    fr = fr_ref[...]
    h1 = jnp.sin(fr * pre1)
    h2 = jnp.sin(fr * (_dot_f32(h1, w2_ref[...]) + b2_ref[...]))
    h = _dot_f32(h2, w3_ref[...])
    keep = row != n
    for q in range(HY_WIDTH // LANES):
        win = jnp.where(keep, jnp.exp(-tcol * dl_ref[:, q * LANES:(q + 1) * LANES]), 0.0)
        for o in range(HY_ORDER):
            lo = o * 2 * HY_WIDTH + q * LANES
            f_ref[o, :, q * LANES:(q + 1) * LANES] = jnp.where(
                fwd, h[:, lo:lo + LANES], h[:, lo + HY_WIDTH:lo + HY_WIDTH + LANES]) * win


def _hy_filter_call(n, p):
    bands = jnp.zeros((1, LANES), F32).at[0, :HY_BANDS].set(
        jnp.linspace(1e-4, HY_BANDS - 1, HY_BANDS, dtype=F32))
    max_decay = math.log(HY_DECAY_TARGET) / HY_DECAY_SHORT_PCT
    min_decay = math.log(HY_DECAY_TARGET) / HY_DECAY_LONG_PCT
    deltas = jnp.abs(jnp.linspace(min_decay, max_decay, HY_WIDTH, dtype=F32)).reshape(1, HY_WIDTH)
    w1 = p['hy_f_w1']
    w1t = w1[0:1]
    w1c = jnp.zeros((LANES, HY_FFN), F32).at[:HY_BANDS].set(w1[1:1 + HY_BANDS])
    w1s = jnp.zeros((LANES, HY_FFN), F32).at[:HY_BANDS].set(w1[1 + HY_BANDS:HY_EMB])
    nout = HY_ORDER * 2 * HY_WIDTH
    tile = min(2 * n, 512)
    full = lambda *s: pl.BlockSpec(s, lambda i: (0,) * len(s))
    return pl.pallas_call(
        functools.partial(_hy_filter_body, n=n, tile=tile),
        out_shape=jax.ShapeDtypeStruct((HY_ORDER, 2 * n, HY_WIDTH), F32),
        grid=(2 * n // tile,),
        in_specs=[full(1, LANES), full(1, HY_FFN), full(LANES, HY_FFN), full(LANES, HY_FFN), full(1, HY_FFN),
                  full(HY_FFN, HY_FFN), full(1, HY_FFN), full(1, HY_FFN), full(HY_FFN, nout), full(1, HY_WIDTH)],
        out_specs=pl.BlockSpec((HY_ORDER, tile, HY_WIDTH), lambda i: (0, i, 0)),
        compiler_params=_cparams(("parallel",), VMEM_LIMIT),
        name="hyena_filter",
    )(bands, w1t, w1c, w1s, p['hy_f_b1'].reshape(1, HY_FFN), p['hy_f_w2'], p['hy_f_b2'].reshape(1, HY_FFN),
      p['hy_f_freq'].reshape(1, HY_FFN), p['hy_f_w3'], deltas)


def _dft_body(*refs, pre, post, two, half_in, half_mid):
    it = iter(refs)
    x = next(it)[...]
    if pre:
        m1 = next(it)[...]
        m2 = next(it)[...]
        x = x * m1 + pltpu.roll(x, half_in, 1) * m2
    g1h = next(it)[...]
    g1l = next(it)[...]
    y = _dot3(x, g1h, g1l)
    if post:
        m1 = next(it)[...]
        m2 = next(it)[...]
        y = y * m1 + pltpu.roll(y, half_mid, 1) * m2
    if two:
        g2h = next(it)[...]
        g2l = next(it)[...]
        y = _dot3(y, g2h, g2l)
    o_ref = next(it)
    o_ref[...] = y


def _dft_call(x, g1, tile, pre=None, post=None, g2=None, name="hyena_dft"):
    bsz, rows, k = x.shape
    args = [x]
    specs = [pl.BlockSpec((None, tile, k), lambda b, j: (b, j, 0))]

    def add_mul(mm):
        for a in mm:
            nblk = a.shape[0] // tile
            specs.append(pl.BlockSpec((tile, a.shape[1]), lambda b, j, nblk=nblk: (j % nblk, 0)))
            args.append(a)

    def add_mat(g):
        for a in g:
            specs.append(pl.BlockSpec(a.shape, lambda b, j: (0, 0)))
            args.append(a)

    if pre is not None:
        add_mul(pre)
    add_mat(g1)
    if post is not None:
        add_mul(post)
    if g2 is not None:
        add_mat(g2)
    mid = g1[0].shape[1]
    nout = g2[0].shape[1] if g2 is not None else mid
    body = functools.partial(_dft_body, pre=pre is not None, post=post is not None, two=g2 is not None,
                             half_in=k // 2, half_mid=mid // 2)
    return pl.pallas_call(
        body,
        out_shape=jax.ShapeDtypeStruct((bsz, rows, nout), F32),
        grid=(bsz, rows // tile),
        in_specs=specs,
        out_specs=pl.BlockSpec((None, tile, nout), lambda b, j: (b, j, 0)),
        compiler_params=_cparams(("parallel", "parallel"), VMEM_LIMIT),
        name=name,
    )(*args)


def _hilo(a):
    a = np.asarray(a, np.float64)
    hi = jnp.asarray(a, F32).astype(BF16)
    lo = (jnp.asarray(a, F32) - hi.astype(F32)).astype(BF16)
    return hi, lo


@functools.lru_cache(maxsize=None)
def _dft_plan(n):
    big = 2 * n
    if big <= 512:
        t = np.arange(n)[:, None]
        k = np.arange(big)[None, :]
        fwd = np.exp(-2j * np.pi * t * k / big)
        fwd_full = np.exp(-2j * np.pi * np.arange(big)[:, None] * k / big)
        inv = np.exp(2j * np.pi * np.arange(big)[:, None] * np.arange(n)[None, :] / big) / big
        return dict(stages=1,
                    fwd=np.concatenate([fwd.real, fwd.imag], axis=1),
                    fwd_full=np.concatenate([fwd_full.real, fwd_full.imag], axis=1),
                    inv=np.concatenate([inv.real, -inv.imag], axis=0))
    n1, n2 = DFT_N1, big // DFT_N1
    k1 = np.arange(n1)[None, :, None]
    t = n2 * np.arange(n1)[None, None, :] + np.arange(n2)[:, None, None]
    ga = np.exp(-2j * np.pi * k1 * t / big)
    ga = np.concatenate([ga.real, ga.imag], axis=1)
    fb = np.exp(-2j * np.pi * np.arange(n2)[:, None] * np.arange(n2)[None, :] / n2)
    fbc = np.block([[fb.real, -fb.imag], [fb.imag, fb.real]])
    fbi = np.conj(fb) / big
    fbic = np.block([[fbi.real, -fbi.imag], [fbi.imag, fbi.real]])
    return dict(stages=2, n1=n1, n2=n2, ga=ga, fb=fbc, fb_inv=fbic)


def _long_conv_setup(n, f):
    c = HY_WIDTH
    plan = _dft_plan(n)
    if plan['stages'] == 1:
        xf = jnp.transpose(f, (0, 2, 1))
        spec = _dft_call(xf, _hilo(plan['fwd_full']), tile=min(c, 256), name="hyena_filter_dft")
        fr, fi = spec[..., :2 * n], spec[..., 2 * n:]
        return [(jnp.concatenate([fr[o], fr[o]], axis=1), jnp.concatenate([-fi[o], fi[o]], axis=1))
                for o in range(HY_ORDER)]
    return _fused_filter_call(f, plan)


def _long_conv(z, hmul):
    bsz, n, c = z.shape
    plan = _dft_plan(n)
    x = jnp.transpose(z, (0, 2, 1))
    y = _dft_call(x, _hilo(plan['fwd']), tile=min(c, 256), post=hmul, g2=_hilo(plan['inv']),
                  name="hyena_conv_short")
    return jnp.transpose(y, (0, 2, 1))


def _plane_pitch(n2):
    return n2 if (n2 // 8) % 2 == 1 else n2 + 8


def _dot3c(g_hi, g_lo, x):
    x_hi, x_lo = _split2(x)
    d = functools.partial(jnp.dot, preferred_element_type=F32)
    return d(g_hi, x_hi) + (d(g_hi, x_lo) + d(g_lo, x_hi))


def _gdot(g_hi, g_lo, x):
    if g_lo is None:
        return jnp.dot(g_hi, x.astype(BF16), preferred_element_type=F32)
    return _dot3c(g_hi, g_lo, x)


def _stage_one(x_ref, g_hi_ref, g_lo_ref, asc_ref, *, n2, nt1, n1, pitch):
    def body(t2, c):
        xs = x_ref[pl.ds(t2, nt1, stride=n2), :]
        g_lo = None if g_lo_ref is None else g_lo_ref[t2]
        asc_ref[pl.ds(t2, 2 * n1, stride=pitch), :] = _gdot(g_hi_ref[t2], g_lo, xs)
        return c
    lax.fori_loop(0, n2, body, 0, unroll=4)


def _plane(asc_ref, k1, n1, n2, pitch):
    o_re = pl.multiple_of(k1 * pitch, 8)
    o_im = pl.multiple_of((n1 + k1) * pitch, 8)
    return o_re, o_im, jnp.concatenate([asc_ref[pl.ds(o_re, n2), :], asc_ref[pl.ds(o_im, n2), :]], axis=0)


def _fused_filter_body(f_ref, g_hi_ref, g_lo_ref, fb_hi_ref, fb_lo_ref, h_ref, asc_ref, *, n1, n2, pitch):
    _stage_one(f_ref, g_hi_ref, g_lo_ref, asc_ref, n2=n2, nt1=n1, n1=n1, pitch=pitch)

    def body(k1, c):
        _, _, ain = _plane(asc_ref, k1, n1, n2, pitch)
        h_ref[k1] = _dot3c(fb_hi_ref[...], fb_lo_ref[...], ain)
        return c
    lax.fori_loop(0, n1, body, 0, unroll=8)


def _fused_filter_call(f, plan):
    order, big, c = f.shape
    n1, n2 = plan['n1'], plan['n2']
    pitch = _plane_pitch(n2)
    g_hi, g_lo = _hilo(plan['ga'])
    fb_hi, fb_lo = _hilo(plan['fb'])
    const = lambda a: pl.BlockSpec(a.shape, lambda o, j: (0,) * a.ndim)
    return pl.pallas_call(
        functools.partial(_fused_filter_body, n1=n1, n2=n2, pitch=pitch),
        out_shape=jax.ShapeDtypeStruct((order, c // LANES, n1, 2 * n2, LANES), F32),
        grid=(order, c // LANES),
        in_specs=[pl.BlockSpec((None, big, LANES), lambda o, j: (o, 0, j)),
                  const(g_hi), const(g_lo), const(fb_hi), const(fb_lo)],
        out_specs=pl.BlockSpec((None, None, n1, 2 * n2, LANES), lambda o, j: (o, j, 0, 0, 0)),
        scratch_shapes=[pltpu.VMEM((2 * n1 * pitch, LANES), F32)],
        compiler_params=_cparams(("parallel", "parallel"), VMEM_LIMIT),
        name="hyena_filter_spectrum",
    )(f, g_hi, g_lo, fb_hi, fb_lo)


def _fused_conv_body(x_ref, xg_ref, skip_ref, h_ref, g_ref, fb_ref, fbi_ref, o_ref, asc_ref, y_ref,
                     *, n1, n2, pitch):
    h1 = n1 // 2
    _stage_one(x_ref, g_ref, None, asc_ref, n2=n2, nt1=h1, n1=n1, pitch=pitch)

    def mid(k1, c):
        o_re, o_im, ain = _plane(asc_ref, k1, n1, n2, pitch)
        b = _gdot(fb_ref[...], None, ain)
        h = h_ref[k1]
        br, bi, hr, hi = b[:n2], b[n2:], h[:n2], h[n2:]
        z = jnp.concatenate([br * hr - bi * hi, br * hi + bi * hr], axis=0)
        cc = _gdot(fbi_ref[...], None, z)
        asc_ref[pl.ds(o_re, n2), :] = cc[:n2]
        asc_ref[pl.ds(o_im, n2), :] = cc[n2:]
        return c
    lax.fori_loop(0, n1, mid, 0, unroll=8)

    def last(t2, c):
        zin = asc_ref[pl.ds(t2, 2 * n1, stride=pitch), :]
        y_ref[pl.ds(t2, h1, stride=n2), :] = lax.dot_general(
            g_ref[t2], zin.astype(BF16), (((0,), (0,)), ((), ())), preferred_element_type=F32)
        return c
    lax.fori_loop(0, n2, last, 0, unroll=4)
    x = x_ref[...]
    o_ref[...] = xg_ref[...] * (y_ref[...] + x * skip_ref[...])


def _fused_conv_call(x, xg, skip, hspec, plan, row0_x, row0_g, bsz, n):
    assert row0_x % n == 0 and row0_g % n == 0
    c = x.shape[1]
    n1, n2 = plan['n1'], plan['n2']
    h1 = n1 // 2
    pitch = _plane_pitch(n2)
    g_hi = _hilo(plan['ga'][:, :, :h1])[0]
    fb_hi = _hilo(plan['fb'])[0]
    fbi_hi = _hilo(plan['fb_inv'])[0]
    const = lambda a: pl.BlockSpec(a.shape, lambda j, b: (0,) * a.ndim)
    seq = lambda row0: pl.BlockSpec((n, LANES), lambda j, b: (row0 // n + b, j))
    return pl.pallas_call(
        functools.partial(_fused_conv_body, n1=n1, n2=n2, pitch=pitch),
        out_shape=jax.ShapeDtypeStruct((bsz * n, c), F32),
        grid=(c // LANES, bsz),
        in_specs=[seq(row0_x), seq(row0_g), pl.BlockSpec((1, LANES), lambda j, b: (0, j)),
                  pl.BlockSpec((None, n1, 2 * n2, LANES), lambda j, b: (j, 0, 0, 0)),
                  const(g_hi), const(fb_hi), const(fbi_hi)],
        out_specs=pl.BlockSpec((n, LANES), lambda j, b: (b, j)),
        scratch_shapes=[pltpu.VMEM((2 * n1 * pitch, LANES), F32), pltpu.VMEM((n, LANES), F32)],
        compiler_params=_cparams(("parallel", "parallel"), VMEM_LIMIT),
        name="hyena_conv_long",
    )(x, xg, skip.reshape(1, c), hspec, g_hi, fb_hi, fbi_hi)


def _hy_gate_body(x_ref, y_ref, z_ref, s_ref, o_ref):
    o_ref[...] = x_ref[...] * (y_ref[...] + z_ref[...] * s_ref[...])


def _hy_gate_call(x, y, z, skip):
    t, c = x.shape
    tok = pl.BlockSpec((512, c), lambda i: (i, 0))
    return pl.pallas_call(
        _hy_gate_body,
        out_shape=jax.ShapeDtypeStruct((t, c), F32),
        grid=(t // 512,),
        in_specs=[tok, tok, tok, pl.BlockSpec((1, c), lambda i: (0, 0))],
        out_specs=tok,
        compiler_params=_cparams(("parallel",)),
        name="hyena_gate",
    )(x, y, z, skip.reshape(1, c))


def _hyena_branch(hy, p, groups, n_ctx_blocks):
    u, x1, x2 = _hy_pre_call(hy, p['hy_conv_w'], p['hy_conv_b'], n_ctx_blocks)
    outs = []
    for start, bsz, n in groups:
        rows = bsz * n
        plan = _dft_plan(n)
        hmul = _long_conv_setup(n, _hy_filter_call(n, p))
        if plan['stages'] == 1:
            ug, x1g, x2g = (a[start:start + rows] for a in (u, x1, x2))
            y = _long_conv(ug.reshape(bsz, n, HY_WIDTH), hmul[0]).reshape(rows, HY_WIDTH)
            z = _hy_gate_call(x1g, y, ug, p['hy_skip'][0])
            y = _long_conv(z.reshape(bsz, n, HY_WIDTH), hmul[1]).reshape(rows, HY_WIDTH)
            outs.append(_hy_gate_call(x2g, y, z, p['hy_skip'][1]))
        else:
            z = _fused_conv_call(u, x1, p['hy_skip'][0], hmul[0], plan, start, start, bsz, n)
            outs.append(_fused_conv_call(z, x2, p['hy_skip'][1], hmul[1], plan, 0, start, bsz, n))
    return jnp.concatenate(outs, axis=0)


def _merge_body(x_ref, yfw_ref, ybw_ref, bonus_ref, g_ref, yb_ref, gt_ref, g1_ref, sh2_ref, sc2_ref, seg_ref,
                gng_ref, gnb_ref, wpa_ref, wpb_ref, wout_ref, lng_ref, lnb_ref, rwh_ref, rwl_ref, rb_ref,
                x1_ref, h2_ref, ti_ref, tg_ref, *, alpha):
    seg = seg_ref[...]
    y = yfw_ref[...] + ybw_ref[...]
    inv = 1.0 / RW_HEAD
    mu = _dot_x3(y, seg) * inv
    yc = y - mu
    var = _dot_x3(yc * yc, seg) * inv
    yn = yc * lax.rsqrt(var + GN_EPS) * gng_ref[...] + gnb_ref[...]
    y_a = (yn + bonus_ref[...]) * g_ref[...]
    gt = gt_ref[...]
    merged = gt[:, :D_MODEL] * _bdot(y_a, wpa_ref[...]) + gt[:, D_MODEL:] * _bdot(yb_ref[...], wpb_ref[...])
    mix = _bdot(merged, wout_ref[...])
    x1 = _layer_norm(alpha * x_ref[...] + g1_ref[...] * mix, lng_ref[...], lnb_ref[...])
    x1_ref[...] = x1
    h2 = x1 * (1.0 + sc2_ref[...]) + sh2_ref[...]
    _store_token_tiles(h2_ref, 0, h2)
    h_hi, h_lo = _split2(h2)
    d = functools.partial(jnp.dot, preferred_element_type=F32)
    logits = d(h_hi, rwh_ref[...]) + (d(h_hi, rwl_ref[...]) + d(h_lo, rwh_ref[...])) + rb_ref[...]
    lane = lax.broadcasted_iota(jnp.int32, logits.shape, 1)
    neg = jnp.float32(-jnp.inf)
    cur = jnp.where(lane < N_EXPERTS, logits, neg)
    top_i = jnp.zeros(logits.shape, jnp.int32)
    top_e = jnp.zeros(logits.shape, F32)
    den = jnp.zeros((logits.shape[0], 1), F32)
    v0 = None
    for j in range(TOP_K):
        mx = jnp.max(cur, axis=-1, keepdims=True)
        idx = jnp.min(jnp.where(cur == mx, lane, LANES), axis=-1, keepdims=True)
        if j == 0:
            v0 = mx
        e = jnp.exp(mx - v0)
        den = den + e
        top_i = jnp.where(lane == j, idx, top_i)
        top_e = jnp.where(lane == j, e, top_e)
        cur = jnp.where(lane == idx, neg, cur)
    ti_ref[...] = top_i
    tg_ref[...] = top_e / den


def _merge_call(x, yfw, ybw, bonus, g, yb, gates, mod_l, p, seg, cond_idx, alpha):
    t = x.shape[0]
    w = RW_WIDTH
    tok = lambda n: pl.BlockSpec((TOKEN_BLOCK, n), lambda i: (i, 0))
    modspec = lambda j: pl.BlockSpec((None, 1, D_MODEL), lambda i: (cond_idx(i), 0, j))
    full = lambda *s: pl.BlockSpec(s, lambda i: (0,) * len(s))
    rw_pad = jnp.zeros((D_MODEL, LANES), F32).at[:, :N_EXPERTS].set(p['router_w'])
    rw_hi = rw_pad.astype(BF16)
    rw_lo = (rw_pad - rw_hi.astype(F32)).astype(BF16)
    rb = jnp.zeros((1, LANES), F32).at[0, :N_EXPERTS].set(p['router_b'])
    o = jax.ShapeDtypeStruct((t, D_MODEL), F32)
    return pl.pallas_call(
        functools.partial(_merge_body, alpha=alpha),
        out_shape=(o, jax.ShapeDtypeStruct((t * TOKEN_TILE, LANES), F32),
                   jax.ShapeDtypeStruct((t, LANES), jnp.int32), jax.ShapeDtypeStruct((t, LANES), F32)),
        grid=(t // TOKEN_BLOCK,),
        in_specs=[tok(D_MODEL), tok(w), tok(w), tok(w), tok(w), tok(w),
                  tok(2 * D_MODEL), modspec(2), modspec(3), modspec(4), full(w, w), full(1, w), full(1, w),
                  full(w, D_MODEL), full(w, D_MODEL), full(D_MODEL, D_MODEL), full(1, D_MODEL), full(1, D_MODEL),
                  full(D_MODEL, LANES), full(D_MODEL, LANES), full(1, LANES)],
        out_specs=(tok(D_MODEL), pl.BlockSpec((TOKEN_BLOCK * TOKEN_TILE, LANES), lambda i: (i, 0)),
                   tok(LANES), tok(LANES)),
        compiler_params=_cparams(("parallel",), VMEM_LIMIT),
        name="merge_ln_router",
    )(x, yfw, ybw, bonus, g, yb, gates, mod_l, mod_l, mod_l, seg, p['gn_g'].reshape(1, w), p['gn_b'].reshape(1, w),
      p['w_pa'].astype(BF16), p['w_pb'].astype(BF16), p['w_out'].astype(BF16),
      p['ln1_g'].reshape(1, D_MODEL), p['ln1_b'].reshape(1, D_MODEL), rw_hi, rw_lo, rb)


DEINT_COLS = 256


def _deint_body(w_ref, p_ref, g_ref, l_ref):
    y = jnp.dot(w_ref[...].astype(BF16), p_ref[...], preferred_element_type=F32)
    half = DEINT_COLS // 2
    g_ref[...] = y[:, :half].astype(BF16)
    l_ref[...] = y[:, half:].astype(BF16)


def _deint_call(w, layer):
    _, e, k, n2 = w.shape
    half = DEINT_COLS // 2
    sel = np.zeros((DEINT_COLS, DEINT_COLS), np.float32)
    sel[2 * np.arange(half), np.arange(half)] = 1.0
    sel[2 * np.arange(half) + 1, half + np.arange(half)] = 1.0
    o = jax.ShapeDtypeStruct((e, k, n2 // 2), BF16)
    return pl.pallas_call(
        _deint_body,
        out_shape=(o, o),
        grid=(e, n2 // DEINT_COLS),
        in_specs=[pl.BlockSpec((None, None, k, DEINT_COLS), lambda i, j: (layer, i, 0, j)),
                  pl.BlockSpec((DEINT_COLS, DEINT_COLS), lambda i, j: (0, 0))],
        out_specs=(pl.BlockSpec((None, k, half), lambda i, j: (i, 0, j)),) * 2,
        compiler_params=_cparams(("parallel", "parallel")),
        name="expert_w_split",
    )(w, jnp.asarray(sel, BF16))


def _moe_body(blk_e_ref, n_on_ref, tok_ref, tokn_ref, dst_ref, h_hbm, wg_ref, wl_ref, bg_ref, bl_ref, wdn_ref,
              bdn_ref, y_hbm, xbuf, ybuf, sem_in, sem_out, *, n_real):
    i = pl.program_id(0)
    n_on = n_on_ref[0]
    slot = i % 2
    tt = TOKEN_TILE
    slot_rows = MOE_ROWS * tt

    def tile(ref, idx):
        return ref.at[pl.ds(pl.multiple_of(idx * tt, tt), tt), :]

    def gather_start(tref, s):
        def body(r, c):
            pltpu.make_async_copy(tile(h_hbm, tref[0, r]), tile(xbuf, s * MOE_ROWS + r), sem_in.at[s]).start()
            return c
        lax.fori_loop(0, MOE_ROWS, body, 0, unroll=8)

    def slot_buf(buf, s):
        return buf.at[pl.ds(pl.multiple_of(s * slot_rows, slot_rows), slot_rows), :]

    def gather_wait(s):
        pltpu.make_async_copy(h_hbm.at[pl.ds(0, slot_rows), :], slot_buf(xbuf, s), sem_in.at[s]).wait()

    def scatter_start(s):
        def body(r, c):
            pltpu.make_async_copy(tile(ybuf, s * MOE_ROWS + r), tile(y_hbm, dst_ref[0, r]), sem_out.at[s]).start()
            return c
        lax.fori_loop(0, MOE_ROWS, body, 0, unroll=8)

    def scatter_wait(s):
        pltpu.make_async_copy(slot_buf(ybuf, s), y_hbm.at[pl.ds(0, slot_rows), :], sem_out.at[s]).wait()

    @pl.when(i == 0)
    def _():
        ybuf[...] = jnp.zeros_like(ybuf)
        fills = [pltpu.make_async_copy(slot_buf(ybuf, s),
                                       y_hbm.at[pl.ds((n_real + s * MOE_ROWS) * tt, slot_rows), :],
                                       sem_out.at[s]) for s in range(2)]
        for cp in fills:
            cp.start()
        for cp in fills:
            cp.wait()

    @pl.when(jnp.logical_and(i == 0, n_on > 0))
    def _():
        gather_start(tok_ref, 0)

    @pl.when(i < n_on)
    def _():
        @pl.when(i + 1 < n_on)
        def _():
            gather_start(tokn_ref, 1 - slot)

        gather_wait(slot)
        row0 = slot * slot_rows
        x = _load_token_tiles(xbuf, row0, MOE_ROWS).astype(BF16)
        d = functools.partial(jnp.dot, preferred_element_type=F32)
        glu = jnp.minimum(d(x, wg_ref[...]) + bg_ref[...], SWIGLU_LIMIT)
        lin = jnp.clip(d(x, wl_ref[...]) + bl_ref[...], -SWIGLU_LIMIT, SWIGLU_LIMIT)
        act = glu * _sigmoid(SWIGLU_ALPHA * glu) * (lin + 1.0)
        _store_token_tiles(ybuf, row0, d(act.astype(BF16), wdn_ref[...]) + bdn_ref[...])
        scatter_start(slot)

        @pl.when(i >= 1)
        def _():
            scatter_wait(1 - slot)

        @pl.when(i == n_on - 1)
        def _():
            scatter_wait(slot)


def _moe_call(h2, top_i, wg, wl, bg, bl, wdn, bdn):
    t = h2.shape[0] // TOKEN_TILE
    m = t * TOP_K
    e = N_EXPERTS
    blk = MOE_ROWS
    flat_e = top_i.reshape(-1)
    order = jnp.argsort(flat_e, stable=True).astype(jnp.int32)
    sizes = jnp.bincount(flat_e, length=e).astype(jnp.int32)
    padded = (sizes + blk - 1) // blk * blk
    pad_end = jnp.cumsum(padded)
    pad_start = pad_end - padded
    grp_start = jnp.cumsum(sizes) - sizes
    n_blocks = -(-(m + e * (blk - 1)) // blk)
    blk_first = jnp.arange(n_blocks, dtype=jnp.int32) * blk
    blk_e = jnp.minimum(jnp.sum((pad_end[None, :] <= blk_first[:, None]).astype(jnp.int32), axis=1), e - 1)
    pidx = jnp.arange(n_blocks * blk, dtype=jnp.int32)
    e_p = jnp.repeat(blk_e, blk)
    idx = pidx - pad_start[e_p]
    valid = idx < sizes[e_p]
    assign = order[jnp.clip(grp_start[e_p] + idx, 0, m - 1)]
    tok_row = jnp.where(valid, assign // TOP_K, 0).astype(jnp.int32)
    spare = m + ((pidx // blk) % 2) * blk + pidx % blk
    dst_row = jnp.where(valid, (assign % TOP_K) * t + assign // TOP_K, spare).astype(jnp.int32)
    n_on = (pad_end[-1] // blk).astype(jnp.int32).reshape(1)
    tok3 = tok_row.reshape(n_blocks, 1, blk)

    smem = lambda f: pl.BlockSpec((None, 1, blk), f, memory_space=pltpu.SMEM)
    wspec = lambda a, b: pl.BlockSpec((None, a, b), lambda i, be, no: (be[i], 0, 0))
    gs = pltpu.PrefetchScalarGridSpec(
        num_scalar_prefetch=2,
        grid=(n_blocks,),
        in_specs=[smem(lambda i, be, no: (i, 0, 0)),
                  smem(lambda i, be, no: (jnp.minimum(i + 1, n_blocks - 1), 0, 0)),
                  smem(lambda i, be, no: (i, 0, 0)),
                  pl.BlockSpec(memory_space=pl.ANY),
                  wspec(D_MODEL, D_FF), wspec(D_MODEL, D_FF), wspec(1, D_FF), wspec(1, D_FF),
                  wspec(D_FF, D_MODEL), wspec(1, D_MODEL)],
        out_specs=pl.BlockSpec(memory_space=pl.ANY),
        scratch_shapes=[pltpu.VMEM((2 * blk * TOKEN_TILE, LANES), F32), pltpu.VMEM((2 * blk * TOKEN_TILE, LANES), F32),
                        pltpu.SemaphoreType.DMA((2,)), pltpu.SemaphoreType.DMA((2,))],
    )
    return pl.pallas_call(
        functools.partial(_moe_body, n_real=m),
        out_shape=jax.ShapeDtypeStruct(((m + 2 * blk) * TOKEN_TILE, LANES), F32),
        grid_spec=gs,
        compiler_params=_cparams(("arbitrary",), VMEM_LIMIT),
        name="moe_experts",
    )(blk_e, n_on, tok3, tok3, dst_row.reshape(n_blocks, 1, blk), h2, wg, wl, bg.reshape(e, 1, D_FF),
      bl.reshape(e, 1, D_FF), wdn, bdn.reshape(e, 1, D_MODEL))


def _combine_body(x1_ref, y0_ref, y1_ref, y2_ref, y3_ref, tg_ref, g2_ref, lng_ref, lnb_ref, o_ref, *, alpha):
    tg = tg_ref[...]
    rows = x1_ref.shape[0]
    moe = tg[:, 0:1] * _load_token_tiles(y0_ref, 0, rows)
    for j, y_ref in enumerate((y1_ref, y2_ref, y3_ref), start=1):
        moe = moe + tg[:, j:j + 1] * _load_token_tiles(y_ref, 0, rows)
    o_ref[...] = _layer_norm(alpha * x1_ref[...] + g2_ref[...] * moe, lng_ref[...], lnb_ref[...])


def _combine_call(x1, yexp, tg, mod_l, p, cond_idx, alpha):
    t = x1.shape[0]
    nb = t // TOKEN_BLOCK
    tok = lambda n: pl.BlockSpec((TOKEN_BLOCK, n), lambda i: (i, 0))
    full = lambda *s: pl.BlockSpec(s, lambda i: (0,) * len(s))
    yspec = [pl.BlockSpec((TOKEN_BLOCK * TOKEN_TILE, LANES), lambda i, j=j: (j * nb + i, 0)) for j in range(TOP_K)]
    return pl.pallas_call(
        functools.partial(_combine_body, alpha=alpha),
        out_shape=jax.ShapeDtypeStruct((t, D_MODEL), F32),
        grid=(nb,),
        in_specs=[tok(D_MODEL)] + yspec + [tok(LANES),
                  pl.BlockSpec((None, 1, D_MODEL), lambda i: (cond_idx(i), 0, 5)),
                  full(1, D_MODEL), full(1, D_MODEL)],
        out_specs=tok(D_MODEL),
        compiler_params=_cparams(("parallel",), VMEM_LIMIT),
        name="moe_combine_ln",
    )(x1, yexp, yexp, yexp, yexp, tg, mod_l, p['ln2_g'].reshape(1, D_MODEL), p['ln2_b'].reshape(1, D_MODEL))


def kernel(x_prompt, x_sample, c, state_rwkv, c_ctx, w_mod, b_mod, w_in, mu_shift, w0, w_lora_up, a0, a_lora_up, g_up, k_k, k_a, r_k, gn_g, gn_b, hy_conv_w, hy_conv_b, hy_f_w1, hy_f_b1, hy_f_w2, hy_f_b2, hy_f_freq, hy_f_w3, hy_skip, w_pa, w_pb, w_out, ln1_g, ln1_b, ln2_g, ln2_b, router_w, router_b, ex_w_up, ex_b_up, ex_w_down, ex_b_down):
    bsz, seq, dm = x_prompt.shape
    dbsz, dseq, _ = x_sample.shape
    depth = w_mod.shape[0]
    assert dm == D_MODEL and seq == TOKEN_BLOCK and dseq % TOKEN_BLOCK == 0 and TOKEN_BLOCK % GRID_W == 0
    assert 1 + dbsz <= 8
    alpha = (2 * depth) ** 0.25
    t_ctx = bsz * seq
    n_ctx_blocks = t_ctx // TOKEN_BLOCK
    lat_blocks = dseq // TOKEN_BLOCK

    def cond_idx(i):
        return jnp.where(i < n_ctx_blocks, 0, 1 + (i - n_ctx_blocks) // lat_blocks)

    x = jnp.concatenate([x_prompt.reshape(t_ctx, dm), x_sample.reshape(dbsz * dseq, dm)], axis=0)
    cond8 = jnp.zeros((8, dm), F32).at[0].set(c_ctx).at[1:1 + dbsz].set(c)
    mod = _mod_call(cond8, w_mod, b_mod)

    hd = RW_HEAD
    seg = (np.arange(RW_WIDTH)[:, None] // hd == np.arange(RW_WIDTH)[None, :] // hd)
    seg = jnp.asarray(seg, BF16)
    seq_lens = [seq] * bsz + [dseq] * dbsz
    tbl_np, nsteps = _scan_table(seq_lens)
    tbl = jnp.asarray(tbl_np)
    groups = [(0, bsz, seq), (t_ctx, dbsz, dseq)]

    new_states = []
    for l in range(depth):
        p = dict(mu_shift=mu_shift[l], w0=w0[l], w_lora_up=w_lora_up[l], a0=a0[l], a_lora_up=a_lora_up[l],
                 g_up=g_up[l], k_k=k_k[l], k_a=k_a[l], r_k=r_k[l], gn_g=gn_g[l], gn_b=gn_b[l],
                 hy_conv_w=hy_conv_w[l], hy_conv_b=hy_conv_b[l], hy_f_w1=hy_f_w1[l], hy_f_b1=hy_f_b1[l],
                 hy_f_w2=hy_f_w2[l], hy_f_b2=hy_f_b2[l], hy_f_freq=hy_f_freq[l], hy_f_w3=hy_f_w3[l],
                 hy_skip=hy_skip[l], w_pa=w_pa[l], w_pb=w_pb[l], w_out=w_out[l], ln1_g=ln1_g[l], ln1_b=ln1_b[l],
                 ln2_g=ln2_g[l], ln2_b=ln2_b[l], router_w=router_w[l], router_b=router_b[l])
        mod_l = mod[l].reshape(8, 1, 6 * dm)
        rw, hy, gates = _inproj_call(x, mod_l, w_in[l].astype(BF16), cond_idx)
        r, kk, v, lw, b, kd, g, bonus = _rwkv_pre_call(rw, p, seg, n_ctx_blocks)
        s0 = jnp.concatenate([jnp.zeros((bsz, 2, RW_HEADS, hd, hd), F32), state_rwkv[:, l].astype(F32)], axis=0)
        yfw, ybw, sfin = _scan_call(tbl, nsteps, bsz + dbsz, r, kk, v, lw, b, kd, s0)
        new_states.append(sfin[:bsz].astype(x_prompt.dtype))
        yb = _hyena_branch(hy, p, groups, n_ctx_blocks)
        x1, h2, top_i, top_g = _merge_call(x, yfw, ybw, bonus, g, yb, gates, mod_l, p, seg, cond_idx, alpha)
        wg, wl = _deint_call(ex_w_up, l)
        yexp = _moe_call(h2, top_i[:, :TOP_K], wg, wl, ex_b_up[l][:, 0::2], ex_b_up[l][:, 1::2],
                         ex_w_down[l].astype(BF16), ex_b_down[l])
        x = _combine_call(x1, yexp, top_g, mod_l, p, cond_idx, alpha)

    y_p = x[:t_ctx].reshape(bsz, seq, dm)
    y_s = x[t_ctx:].reshape(dbsz, dseq, dm)
    return (y_p, y_s, jnp.stack(new_states, axis=1))
```

```python
import functools
import math

import numpy as np
import jax
import jax.numpy as jnp
from jax import lax
from jax.experimental import pallas as pl
from jax.experimental.pallas import tpu as pltpu

F32 = jnp.float32
BF16 = jnp.bfloat16

D_MODEL = 1024
GRID_W = 64
RW_WIDTH = 512
RW_HEAD = 64
RW_HEADS = RW_WIDTH // RW_HEAD
LORA_W = 64
LORA_A = 64
LORA_G = 128
GN_EPS = 64e-5
HY_WIDTH = 512
HY_ORDER = 2
HY_BANDS = 16
HY_EMB = 2 * HY_BANDS + 1
HY_FFN = 64
HY_DECAY_TARGET = 1e-2
HY_DECAY_SHORT_PCT = 0.3
HY_DECAY_LONG_PCT = 1.5
RW_COLS = 3 * RW_WIDTH + LORA_W + LORA_A + LORA_G
HY_COLS = (HY_ORDER + 1) * HY_WIDTH
IN_COLS = RW_COLS + HY_COLS + 2 * D_MODEL
N_EXPERTS = 32
TOP_K = 4
D_FF = D_MODEL
SWIGLU_ALPHA = 1.702
SWIGLU_LIMIT = 7.0
LN_EPS = 1e-5

TOKEN_BLOCK = 256
SCAN_CHUNK = 64
INV_BLOCK = 16
DFT_N1 = 128
MOE_ROWS = 256
LANES = 128
VMEM_LIMIT = 56 * 1024 * 1024


def _cparams(sem, vmem=None):
    return pltpu.CompilerParams(dimension_semantics=sem, vmem_limit_bytes=vmem)


def _bdot(a, b):
    return jnp.dot(a.astype(BF16), b.astype(BF16), preferred_element_type=F32)


def _bdg(a, b, ca, cb):
    return lax.dot_general(a.astype(BF16), b.astype(BF16), (((ca,), (cb,)), ((0,), (0,))), preferred_element_type=F32)


def _bmm(a, b):
    return _bdg(a, b, 2, 1)


def _bmm_nt(a, b):
    return _bdg(a, b, 2, 2)


def _bmm_tn(a, b):
    return _bdg(a, b, 1, 1)


def _split2(x):
    hi = x.astype(BF16)
    lo = (x - hi.astype(F32)).astype(BF16)
    return hi, lo


def _split3(x):
    hi = x.astype(BF16)
    r1 = x - hi.astype(F32)
    mid = r1.astype(BF16)
    lo = (r1 - mid.astype(F32)).astype(BF16)
    return hi, mid, lo


def _dot_x3(x, g_bf16):
    hi, mid, lo = _split3(x)
    d = functools.partial(jnp.dot, preferred_element_type=F32)
    return d(hi, g_bf16) + (d(mid, g_bf16) + d(lo, g_bf16))


def _dot3(x, g_hi, g_lo):
    x_hi, x_lo = _split2(x)
    d = functools.partial(jnp.dot, preferred_element_type=F32)
    return d(x_hi, g_hi) + (d(x_hi, g_lo) + d(x_lo, g_hi))


def _sigmoid(x):
    return 1.0 / (1.0 + jnp.exp(-x))


TOKEN_TILE = D_MODEL // LANES


def _store_token_tiles(ref, row0, x):
    rows = x.shape[0]
    for s in range(TOKEN_TILE):
        ref[pl.ds(row0 + s, rows, stride=TOKEN_TILE), :] = x[:, s * LANES:(s + 1) * LANES]


def _load_token_tiles(ref, row0, rows):
    return jnp.concatenate([ref[pl.ds(row0 + s, rows, stride=TOKEN_TILE), :] for s in range(TOKEN_TILE)], axis=1)


def _layer_norm(x, g, b):
    mu = jnp.mean(x, axis=-1, keepdims=True)
    xc = x - mu
    var = jnp.mean(xc * xc, axis=-1, keepdims=True)
    return xc * lax.rsqrt(var + LN_EPS) * g + b


def _mod_body(c_ref, w_ref, b_ref, o_ref):
    c = c_ref[...]
    o_ref[...] = _bdot(c * _sigmoid(c), w_ref[...]) + b_ref[...]


def _mod_call(cond8, w_mod, b_mod):
    depth = w_mod.shape[0]
    tn = 1536
    return pl.pallas_call(
        _mod_body,
        out_shape=jax.ShapeDtypeStruct((depth, 8, 6 * D_MODEL), F32),
        grid=(depth, 6 * D_MODEL // tn),
        in_specs=[pl.BlockSpec((8, D_MODEL), lambda l, j: (0, 0)),
                  pl.BlockSpec((None, D_MODEL, tn), lambda l, j: (l, 0, j)),
                  pl.BlockSpec((None, 1, tn), lambda l, j: (l, 0, j))],
        out_specs=pl.BlockSpec((None, 8, tn), lambda l, j: (l, 0, j)),
        compiler_params=_cparams(("parallel", "parallel"), VMEM_LIMIT),
        name="adaln_mod",
    )(cond8, w_mod, b_mod.reshape(depth, 1, 6 * D_MODEL))


def _inproj_body(x_ref, sh_ref, sc_ref, w_ref, rw_ref, hy_ref, gt_ref):
    h = (x_ref[...] * (1.0 + sc_ref[...]) + sh_ref[...]).astype(BF16)
    d = functools.partial(jnp.dot, preferred_element_type=F32)
    rw_ref[...] = d(h, w_ref[:, :RW_COLS])
    hy_ref[...] = d(h, w_ref[:, RW_COLS:RW_COLS + HY_COLS])
    gt_ref[...] = _sigmoid(d(h, w_ref[:, RW_COLS + HY_COLS:]))


def _inproj_call(x, mod_l, w_in_bf16, cond_idx):
    t = x.shape[0]
    nb = t // TOKEN_BLOCK
    tok = lambda n: pl.BlockSpec((TOKEN_BLOCK, n), lambda i: (i, 0))
    modspec = lambda j: pl.BlockSpec((None, 1, D_MODEL), lambda i: (cond_idx(i), 0, j))
    return pl.pallas_call(
        _inproj_body,
        out_shape=(jax.ShapeDtypeStruct((t, RW_COLS), F32), jax.ShapeDtypeStruct((t, HY_COLS), F32),
                   jax.ShapeDtypeStruct((t, 2 * D_MODEL), F32)),
        grid=(nb,),
        in_specs=[tok(D_MODEL), modspec(0), modspec(1),
                  pl.BlockSpec((D_MODEL, IN_COLS), lambda i: (0, 0))],
        out_specs=(tok(RW_COLS), tok(HY_COLS), tok(2 * D_MODEL)),
        compiler_params=_cparams(("parallel",), VMEM_LIMIT),
        name="in_proj",
    )(x, mod_l, mod_l, w_in_bf16)


def _neighbours(x, n_ctx_blocks):
    rows = x.shape[0]
    row = lax.broadcasted_iota(jnp.int32, (rows, 1), 0)
    seg_mask = jnp.where(pl.program_id(0) < n_ctx_blocks, rows - 1, GRID_W - 1)
    pos = row & seg_mask
    prev = jnp.where(pos == 0, 0.0, pltpu.roll(x, 1, 0))
    nxt = jnp.where(pos == seg_mask, 0.0, pltpu.roll(x, rows - 1, 0))
    return prev, nxt


def _rwkv_pre_body(rw_ref, mu_ref, kk_s_ref, ka_ref, w0_ref, a0_ref, wl_ref, al_ref, gup_ref, rk_ref, seg_ref,
                   r_ref, kk_ref, v_ref, lw_ref, b_ref, kd_ref, g_ref, bonus_ref, *, n_ctx_blocks):
    x = rw_ref[...]
    prev, nxt = _neighbours(x, n_ctx_blocks)
    cols = x + mu_ref[...] * (0.5 * (prev + nxt) - x)
    w = RW_WIDTH
    r = cols[:, :w]
    k = cols[:, w:2 * w]
    v = cols[:, 2 * w:3 * w]
    wd = cols[:, 3 * w:3 * w + LORA_W]
    ad = cols[:, 3 * w + LORA_W:3 * w + LORA_W + LORA_A]
    gd = cols[:, 3 * w + LORA_W + LORA_A:]
    seg = seg_ref[...]
    kkr = k * kk_s_ref[...]
    ss = _dot_x3(kkr * kkr, seg)
    kk = kkr / jnp.maximum(jnp.sqrt(ss), 1e-12)
    r_ref[...] = r
    kk_ref[...] = kk
    v_ref[...] = v
    g_ref[...] = _bdot(_sigmoid(gd), gup_ref[...])
    tw = jnp.tanh(wd)
    bonus = jnp.zeros_like(r)
    for d in range(2):
        w_logit = w0_ref[d:d + 1, :] + _bdot(tw, wl_ref[d])
        lw_ref[d] = -math.exp(-0.5) * _sigmoid(w_logit)
        a = _sigmoid(a0_ref[d:d + 1, :] + _bdot(ad, al_ref[d]))
        kd = k * (1.0 + (a - 1.0) * ka_ref[...])
        kd_ref[d] = kd
        b_ref[d] = kk * a
        bonus = bonus + _dot_x3(r * kd * rk_ref[d:d + 1, :], seg) * v
    bonus_ref[...] = bonus


def _rwkv_pre_call(rw, p, seg, n_ctx_blocks):
    t = rw.shape[0]
    nb = t // TOKEN_BLOCK
    w = RW_WIDTH
    tok = lambda n: pl.BlockSpec((TOKEN_BLOCK, n), lambda i: (i, 0))
    tok2 = pl.BlockSpec((2, TOKEN_BLOCK, w), lambda i: (0, i, 0))
    full = lambda *s: pl.BlockSpec(s, lambda i: (0,) * len(s))
    o1 = jax.ShapeDtypeStruct((t, w), F32)
    o2 = jax.ShapeDtypeStruct((2, t, w), F32)
    return pl.pallas_call(
        functools.partial(_rwkv_pre_body, n_ctx_blocks=n_ctx_blocks),
        out_shape=(o1, o1, o1, o2, o2, o2, o1, o1),
        grid=(nb,),
        in_specs=[tok(RW_COLS), full(1, RW_COLS), full(1, w), full(1, w), full(2, w), full(2, w),
                  full(2, LORA_W, w), full(2, LORA_A, w), full(LORA_G, w), full(2, w), full(w, w)],
        out_specs=(tok(w), tok(w), tok(w), tok2, tok2, tok2, tok(w), tok(w)),
        compiler_params=_cparams(("parallel",), VMEM_LIMIT),
        name="rwkv_pre",
    )(rw, p['mu_shift'].reshape(1, RW_COLS), p['k_k'].reshape(1, w), p['k_a'].reshape(1, w), p['w0'], p['a0'],
      p['w_lora_up'], p['a_lora_up'], p['g_up'], p['r_k'].reshape(2, w), seg)


def _scan_body(tbl_ref, rf_ref, kkf_ref, vf_ref, rb_ref, kkb_ref, vb_ref, lwf_ref, bf_ref, kdf_ref, lwb_ref,
               bb_ref, kdb_ref, s0_ref, yf_ref, yb_ref, sfin_ref, s_ref):
    base = pl.program_id(0) * 5
    c = SCAN_CHUNK
    hd = RW_HEAD
    nh = 2 * RW_HEADS

    @pl.when(tbl_ref[base + 3] == 1)
    def _():
        s_ref[...] = s0_ref[...].reshape(nh, hd, hd)

    rowi = lax.broadcasted_iota(jnp.int32, (c, c), 0)
    coli = lax.broadcasted_iota(jnp.int32, (c, c), 1)
    same_blk = (rowi // INV_BLOCK) == (coli // INV_BLOCK)
    eye = (rowi == coli).astype(F32)

    def both(fwd, bwd):
        return jnp.concatenate([jnp.broadcast_to(fwd, (RW_HEADS, c, c)), jnp.broadcast_to(bwd, (RW_HEADS, c, c))],
                               axis=0)

    diff = both(rowi - coli, coli - rowi)
    strict = diff > 0
    incl = diff >= 0

    def heads(a):
        return jnp.stack([a[:, h * hd:(h + 1) * hd] for h in range(RW_HEADS)], axis=0)

    def prep(r_ref, kk_ref, v_ref, lw_ref, b_ref, kd_ref, tri):
        lw = lw_ref[...]
        hi, mid, lo = _split3(lw)
        dd = functools.partial(jnp.dot, preferred_element_type=F32)
        cum = dd(tri, hi) + (dd(tri, mid) + dd(tri, lo))
        tot = jnp.sum(lw, axis=0, keepdims=True)
        e_neg = jnp.exp(-cum)
        e_rem = jnp.exp(tot - cum)
        kk = kk_ref[...]
        bb = b_ref[...]
        kd = kd_ref[...]
        return [heads(a) for a in (kk * jnp.exp(cum - lw), bb * e_neg, kd * e_neg, r_ref[...] * jnp.exp(cum),
                                   v_ref[...], kd * e_rem, bb * e_rem, jnp.exp(tot))]

    fw = prep(rf_ref, kkf_ref, vf_ref, lwf_ref, bf_ref, kdf_ref, (rowi >= coli).astype(BF16))
    bw = prep(rb_ref, kkb_ref, vb_ref, lwb_ref, bb_ref, kdb_ref, (rowi <= coli).astype(BF16))
    a_h, b_h, k_h, r_h, v_h, kapg_h, betg_h, gtot_h = [jnp.concatenate([f, b], axis=0) for f, b in zip(fw, bw)]
    ar = jnp.concatenate([a_h, r_h], axis=1)
    gb = _bmm_nt(ar, b_h)
    gk = _bmm_nt(ar, k_h)
    low = jnp.where(strict, gb[:, :c], 0.0)
    a_ka = jnp.where(strict, gk[:, :c], 0.0)
    a_br = jnp.where(incl, gb[:, c:], 0.0)
    a_kr = jnp.where(incl, gk[:, c:], 0.0)
    nd = jnp.where(same_blk, -low, 0.0)
    loff = jnp.where(same_blk, 0.0, low)
    x = eye + nd
    n2 = _bmm(nd, nd)
    x = x + _bmm(x, n2)
    n4 = _bmm(n2, n2)
    x = x + _bmm(x, n4)
    n8 = _bmm(n4, n4)
    x = x + _bmm(x, n8)
    m = _bmm(x, loff)
    m2 = _bmm(m, m)
    y1 = x + _bmm(m2, x)
    tinv = y1 - _bmm(m, y1)
    w_h = _bmm(a_ka, v_h)
    rhs = jnp.concatenate([a_h, w_h], axis=2)
    x0 = _bmm(tinv, rhs)
    res = rhs - x0 - _bmm(low, x0)
    xs = x0 + _bmm(tinv, res)
    p_h = xs[:, :, :hd]
    q_h = xs[:, :, hd:]
    s_old = s_ref[...]
    uy = _bmm_nt(jnp.concatenate([p_h, r_h], axis=1), s_old)
    u_h = uy[:, :c] + q_h
    y_h = uy[:, c:] + _bmm(a_kr, v_h) - _bmm(a_br, u_h)
    zv = jnp.concatenate([v_h, u_h], axis=1)
    zk = jnp.concatenate([kapg_h, -betg_h], axis=1)
    s_ref[...] = s_old * gtot_h + _bmm_tn(zv, zk)
    for h in range(RW_HEADS):
        yf_ref[:, h * hd:(h + 1) * hd] = y_h[h]
        yb_ref[:, h * hd:(h + 1) * hd] = y_h[RW_HEADS + h]

    @pl.when(tbl_ref[base + 4] == 1)
    def _():
        sfin_ref[...] = s_ref[...].reshape(2, RW_HEADS, hd, hd)


def _scan_call(tbl, nsteps, nseq, r, kk, v, lw, b, kd, s0):
    t = r.shape[0]
    w = RW_WIDTH
    c = SCAN_CHUNK
    blk = lambda d: pl.BlockSpec((c, w), lambda i, tb: (tb[i * 5 + d], 0))
    blk_dir = lambda d: pl.BlockSpec((None, c, w), lambda i, tb: (d, tb[i * 5 + d], 0))
    st = pl.BlockSpec((None, 2, RW_HEADS, RW_HEAD, RW_HEAD), lambda i, tb: (tb[i * 5 + 2], 0, 0, 0, 0))
    gs = pltpu.PrefetchScalarGridSpec(
        num_scalar_prefetch=1,
        grid=(nsteps,),
        in_specs=[blk(0), blk(0), blk(0), blk(1), blk(1), blk(1), blk_dir(0), blk_dir(0), blk_dir(0),
                  blk_dir(1), blk_dir(1), blk_dir(1), st],
        out_specs=(blk(0), blk(1), st),
        scratch_shapes=[pltpu.VMEM((2 * RW_HEADS, RW_HEAD, RW_HEAD), F32)],
    )
    o = jax.ShapeDtypeStruct((t, w), F32)
    return pl.pallas_call(
        _scan_body,
        out_shape=(o, o, jax.ShapeDtypeStruct((nseq, 2, RW_HEADS, RW_HEAD, RW_HEAD), F32)),
        grid_spec=gs,
        compiler_params=_cparams(("arbitrary",), VMEM_LIMIT),
        name="rwkv_scan",
    )(tbl, r, kk, v, r, kk, v, lw, b, kd, lw, b, kd, s0)


def _scan_table(seq_lens):
    c = SCAN_CHUNK
    rows = []
    start = 0
    for s, n in enumerate(seq_lens):
        nc = n // c
        for j in range(nc):
            rows.append((start + j, start + nc - 1 - j, s, int(j == 0), int(j == nc - 1)))
        start += nc
    return np.asarray(rows, np.int32).reshape(-1), len(rows)


def _hy_pre_body(hy_ref, w_ref, b_ref, u_ref, x1_ref, x2_ref, *, n_ctx_blocks):
    x = hy_ref[...]
    prev, nxt = _neighbours(x, n_ctx_blocks)
    cols = prev * w_ref[0:1, :] + x * w_ref[1:2, :] + nxt * w_ref[2:3, :] + b_ref[...]
    w = HY_WIDTH
    u_ref[...] = cols[:, :w]
    x1_ref[...] = cols[:, w:2 * w]
    x2_ref[...] = cols[:, 2 * w:]


def _hy_pre_call(hy, conv_w, conv_b, n_ctx_blocks):
    t = hy.shape[0]
    tok = lambda n: pl.BlockSpec((TOKEN_BLOCK, n), lambda i: (i, 0))
    o = jax.ShapeDtypeStruct((t, HY_WIDTH), F32)
    return pl.pallas_call(
        functools.partial(_hy_pre_body, n_ctx_blocks=n_ctx_blocks),
        out_shape=(o, o, o),
        grid=(t // TOKEN_BLOCK,),
        in_specs=[tok(HY_COLS), pl.BlockSpec((3, HY_COLS), lambda i: (0, 0)),
                  pl.BlockSpec((1, HY_COLS), lambda i: (0, 0))],
        out_specs=(tok(HY_WIDTH),) * 3,
        compiler_params=_cparams(("parallel",)),
        name="hyena_pre",
    )(hy, conv_w, conv_b.reshape(1, HY_COLS))


def _dot_f32(a, b):
    a_hi, a_lo = _split2(a)
    b_hi, b_lo = _split2(b)
    d = functools.partial(jnp.dot, preferred_element_type=F32)
    return d(a_hi, b_hi) + (d(a_hi, b_lo) + d(a_lo, b_hi))


def _hy_filter_body(bands_ref, w1t_ref, w1c_ref, w1s_ref, b1_ref, w2_ref, b2_ref, fr_ref, w3_ref, dl_ref, f_ref,
                    *, n, tile):
    row = lax.broadcasted_iota(jnp.int32, (tile, LANES), 0) + pl.program_id(0) * tile
    fwd = row < n
    pos = jnp.where(fwd, row, 2 * n - row).astype(F32)
    tcol = pos / n
    ang = (2.0 * math.pi / n) * pos * bands_ref[...]
    pre1 = (tcol[:, :HY_FFN] * w1t_ref[...] + _dot_f32(jnp.cos(ang), w1c_ref[...])
            + _dot_f32(jnp.sin(ang), w1s_ref[...]) + b1_ref[...])
    fr = fr_ref[...]
    h1 = jnp.sin(fr * pre1)
    h2 = jnp.sin(fr * (_dot_f32(h1, w2_ref[...]) + b2_ref[...]))
    h = _dot_f32(h2, w3_ref[...])
    keep = row != n
    for q in range(HY_WIDTH // LANES):
        win = jnp.where(keep, jnp.exp(-tcol * dl_ref[:, q * LANES:(q + 1) * LANES]), 0.0)
        for o in range(HY_ORDER):
            lo = o * 2 * HY_WIDTH + q * LANES
            f_ref[o, :, q * LANES:(q + 1) * LANES] = jnp.where(
                fwd, h[:, lo:lo + LANES], h[:, lo + HY_WIDTH:lo + HY_WIDTH + LANES]) * win


def _hy_filter_call(n, p):
    bands = jnp.zeros((1, LANES), F32).at[0, :HY_BANDS].set(
        jnp.linspace(1e-4, HY_BANDS - 1, HY_BANDS, dtype=F32))
    max_decay = math.log(HY_DECAY_TARGET) / HY_DECAY_SHORT_PCT
    min_decay = math.log(HY_DECAY_TARGET) / HY_DECAY_LONG_PCT
    deltas = jnp.abs(jnp.linspace(min_decay, max_decay, HY_WIDTH, dtype=F32)).reshape(1, HY_WIDTH)
    w1 = p['hy_f_w1']
    w1t = w1[0:1]
    w1c = jnp.zeros((LANES, HY_FFN), F32).at[:HY_BANDS].set(w1[1:1 + HY_BANDS])
    w1s = jnp.zeros((LANES, HY_FFN), F32).at[:HY_BANDS].set(w1[1 + HY_BANDS:HY_EMB])
    nout = HY_ORDER * 2 * HY_WIDTH
    tile = min(2 * n, 512)
    full = lambda *s: pl.BlockSpec(s, lambda i: (0,) * len(s))
    return pl.pallas_call(
        functools.partial(_hy_filter_body, n=n, tile=tile),
        out_shape=jax.ShapeDtypeStruct((HY_ORDER, 2 * n, HY_WIDTH), F32),
        grid=(2 * n // tile,),
        in_specs=[full(1, LANES), full(1, HY_FFN), full(LANES, HY_FFN), full(LANES, HY_FFN), full(1, HY_FFN),
                  full(HY_FFN, HY_FFN), full(1, HY_FFN), full(1, HY_FFN), full(HY_FFN, nout), full(1, HY_WIDTH)],
        out_specs=pl.BlockSpec((HY_ORDER, tile, HY_WIDTH), lambda i: (0, i, 0)),
        compiler_params=_cparams(("parallel",), VMEM_LIMIT),
        name="hyena_filter",
    )(bands, w1t, w1c, w1s, p['hy_f_b1'].reshape(1, HY_FFN), p['hy_f_w2'], p['hy_f_b2'].reshape(1, HY_FFN),
      p['hy_f_freq'].reshape(1, HY_FFN), p['hy_f_w3'], deltas)


def _dft_body(*refs, pre, post, two, half_in, half_mid):
    it = iter(refs)
    x = next(it)[...]
    if pre:
        m1 = next(it)[...]
        m2 = next(it)[...]
        x = x * m1 + pltpu.roll(x, half_in, 1) * m2
    g1h = next(it)[...]
    g1l = next(it)[...]
    y = _dot3(x, g1h, g1l)
    if post:
        m1 = next(it)[...]
        m2 = next(it)[...]
        y = y * m1 + pltpu.roll(y, half_mid, 1) * m2
    if two:
        g2h = next(it)[...]
        g2l = next(it)[...]
        y = _dot3(y, g2h, g2l)
    o_ref = next(it)
    o_ref[...] = y


def _dft_call(x, g1, tile, pre=None, post=None, g2=None, name="hyena_dft"):
    bsz, rows, k = x.shape
    args = [x]
    specs = [pl.BlockSpec((None, tile, k), lambda b, j: (b, j, 0))]

    def add_mul(mm):
        for a in mm:
            nblk = a.shape[0] // tile
            specs.append(pl.BlockSpec((tile, a.shape[1]), lambda b, j, nblk=nblk: (j % nblk, 0)))
            args.append(a)

    def add_mat(g):
        for a in g:
            specs.append(pl.BlockSpec(a.shape, lambda b, j: (0, 0)))
            args.append(a)

    if pre is not None:
        add_mul(pre)
    add_mat(g1)
    if post is not None:
        add_mul(post)
    if g2 is not None:
        add_mat(g2)
    mid = g1[0].shape[1]
    nout = g2[0].shape[1] if g2 is not None else mid
    body = functools.partial(_dft_body, pre=pre is not None, post=post is not None, two=g2 is not None,
                             half_in=k // 2, half_mid=mid // 2)
    return pl.pallas_call(
        body,
        out_shape=jax.ShapeDtypeStruct((bsz, rows, nout), F32),
        grid=(bsz, rows // tile),
        in_specs=specs,
        out_specs=pl.BlockSpec((None, tile, nout), lambda b, j: (b, j, 0)),
        compiler_params=_cparams(("parallel", "parallel"), VMEM_LIMIT),
        name=name,
    )(*args)


def _hilo(a):
    a = np.asarray(a, np.float64)
    hi = jnp.asarray(a, F32).astype(BF16)
    lo = (jnp.asarray(a, F32) - hi.astype(F32)).astype(BF16)
    return hi, lo


@functools.lru_cache(maxsize=None)
def _dft_plan(n):
    big = 2 * n
    if big <= 512:
        t = np.arange(n)[:, None]
        k = np.arange(big)[None, :]
        fwd = np.exp(-2j * np.pi * t * k / big)
        fwd_full = np.exp(-2j * np.pi * np.arange(big)[:, None] * k / big)
        inv = np.exp(2j * np.pi * np.arange(big)[:, None] * np.arange(n)[None, :] / big) / big
        return dict(stages=1,
                    fwd=np.concatenate([fwd.real, fwd.imag], axis=1),
                    fwd_full=np.concatenate([fwd_full.real, fwd_full.imag], axis=1),
                    inv=np.concatenate([inv.real, -inv.imag], axis=0))
    n1, n2 = DFT_N1, big // DFT_N1
    k1 = np.arange(n1)[None, :, None]
    t = n2 * np.arange(n1)[None, None, :] + np.arange(n2)[:, None, None]
    ga = np.exp(-2j * np.pi * k1 * t / big)
    ga = np.concatenate([ga.real, ga.imag], axis=1)
    fb = np.exp(-2j * np.pi * np.arange(n2)[:, None] * np.arange(n2)[None, :] / n2)
    fbc = np.block([[fb.real, -fb.imag], [fb.imag, fb.real]])
    fbi = np.conj(fb) / big
    fbic = np.block([[fbi.real, -fbi.imag], [fbi.imag, fbi.real]])
    return dict(stages=2, n1=n1, n2=n2, ga=ga, fb=fbc, fb_inv=fbic)


def _long_conv_setup(n, f):
    c = HY_WIDTH
    plan = _dft_plan(n)
    if plan['stages'] == 1:
        xf = jnp.transpose(f, (0, 2, 1))
        spec = _dft_call(xf, _hilo(plan['fwd_full']), tile=min(c, 256), name="hyena_filter_dft")
        fr, fi = spec[..., :2 * n], spec[..., 2 * n:]
        return [(jnp.concatenate([fr[o], fr[o]], axis=1), jnp.concatenate([-fi[o], fi[o]], axis=1))
                for o in range(HY_ORDER)]
    return _fused_filter_call(f, plan)


def _long_conv(z, hmul):
    bsz, n, c = z.shape
    plan = _dft_plan(n)
    x = jnp.transpose(z, (0, 2, 1))
    y = _dft_call(x, _hilo(plan['fwd']), tile=min(c, 256), post=hmul, g2=_hilo(plan['inv']),
                  name="hyena_conv_short")
    return jnp.transpose(y, (0, 2, 1))


def _plane_pitch(n2):
    return n2 if (n2 // 8) % 2 == 1 else n2 + 8


def _dot3c(g_hi, g_lo, x):
    x_hi, x_lo = _split2(x)
    d = functools.partial(jnp.dot, preferred_element_type=F32)
    return d(g_hi, x_hi) + (d(g_hi, x_lo) + d(g_lo, x_hi))


def _gdot(g_hi, g_lo, x):
    if g_lo is None:
        return jnp.dot(g_hi, x.astype(BF16), preferred_element_type=F32)
    return _dot3c(g_hi, g_lo, x)


def _stage_one(x_ref, g_hi_ref, g_lo_ref, asc_ref, *, n2, nt1, n1, pitch):
    def body(t2, c):
        xs = x_ref[pl.ds(t2, nt1, stride=n2), :]
        g_lo = None if g_lo_ref is None else g_lo_ref[t2]
        asc_ref[pl.ds(t2, 2 * n1, stride=pitch), :] = _gdot(g_hi_ref[t2], g_lo, xs)
        return c
    lax.fori_loop(0, n2, body, 0, unroll=4)


def _plane(asc_ref, k1, n1, n2, pitch):
    o_re = pl.multiple_of(k1 * pitch, 8)
    o_im = pl.multiple_of((n1 + k1) * pitch, 8)
    return o_re, o_im, jnp.concatenate([asc_ref[pl.ds(o_re, n2), :], asc_ref[pl.ds(o_im, n2), :]], axis=0)


def _fused_filter_body(f_ref, g_hi_ref, g_lo_ref, fb_hi_ref, fb_lo_ref, h_ref, asc_ref, *, n1, n2, pitch):
    _stage_one(f_ref, g_hi_ref, g_lo_ref, asc_ref, n2=n2, nt1=n1, n1=n1, pitch=pitch)

    def body(k1, c):
        _, _, ain = _plane(asc_ref, k1, n1, n2, pitch)
        h_ref[k1] = _dot3c(fb_hi_ref[...], fb_lo_ref[...], ain)
        return c
    lax.fori_loop(0, n1, body, 0, unroll=8)


def _fused_filter_call(f, plan):
    order, big, c = f.shape
    n1, n2 = plan['n1'], plan['n2']
    pitch = _plane_pitch(n2)
    g_hi, g_lo = _hilo(plan['ga'])
    fb_hi, fb_lo = _hilo(plan['fb'])
    const = lambda a: pl.BlockSpec(a.shape, lambda o, j: (0,) * a.ndim)
    return pl.pallas_call(
        functools.partial(_fused_filter_body, n1=n1, n2=n2, pitch=pitch),
        out_shape=jax.ShapeDtypeStruct((order, c // LANES, n1, 2 * n2, LANES), F32),
        grid=(order, c // LANES),
        in_specs=[pl.BlockSpec((None, big, LANES), lambda o, j: (o, 0, j)),
                  const(g_hi), const(g_lo), const(fb_hi), const(fb_lo)],
        out_specs=pl.BlockSpec((None, None, n1, 2 * n2, LANES), lambda o, j: (o, j, 0, 0, 0)),
        scratch_shapes=[pltpu.VMEM((2 * n1 * pitch, LANES), F32)],
        compiler_params=_cparams(("parallel", "parallel"), VMEM_LIMIT),
        name="hyena_filter_spectrum",
    )(f, g_hi, g_lo, fb_hi, fb_lo)


def _fused_conv_body(x_ref, xg_ref, skip_ref, h_ref, g_ref, fb_ref, fbi_ref, o_ref, asc_ref, y_ref,
                     *, n1, n2, pitch):
    h1 = n1 // 2
    _stage_one(x_ref, g_ref, None, asc_ref, n2=n2, nt1=h1, n1=n1, pitch=pitch)

    def mid(k1, c):
        o_re, o_im, ain = _plane(asc_ref, k1, n1, n2, pitch)
        b = _gdot(fb_ref[...], None, ain)
        h = h_ref[k1]
        br, bi, hr, hi = b[:n2], b[n2:], h[:n2], h[n2:]
        z = jnp.concatenate([br * hr - bi * hi, br * hi + bi * hr], axis=0)
        cc = _gdot(fbi_ref[...], None, z)
        asc_ref[pl.ds(o_re, n2), :] = cc[:n2]
        asc_ref[pl.ds(o_im, n2), :] = cc[n2:]
        return c
    lax.fori_loop(0, n1, mid, 0, unroll=8)

    def last(t2, c):
        zin = asc_ref[pl.ds(t2, 2 * n1, stride=pitch), :]
        y_ref[pl.ds(t2, h1, stride=n2), :] = lax.dot_general(
            g_ref[t2], zin.astype(BF16), (((0,), (0,)), ((), ())), preferred_element_type=F32)
        return c
    lax.fori_loop(0, n2, last, 0, unroll=4)
    x = x_ref[...]
    o_ref[...] = xg_ref[...] * (y_ref[...] + x * skip_ref[...])


def _fused_conv_call(x, xg, skip, hspec, plan, row0_x, row0_g, bsz, n):
    assert row0_x % n == 0 and row0_g % n == 0
    c = x.shape[1]
    n1, n2 = plan['n1'], plan['n2']
    h1 = n1 // 2
    pitch = _plane_pitch(n2)
    g_hi = _hilo(plan['ga'][:, :, :h1])[0]
    fb_hi = _hilo(plan['fb'])[0]
    fbi_hi = _hilo(plan['fb_inv'])[0]
    const = lambda a: pl.BlockSpec(a.shape, lambda j, b: (0,) * a.ndim)
    seq = lambda row0: pl.BlockSpec((n, LANES), lambda j, b: (row0 // n + b, j))
    return pl.pallas_call(
        functools.partial(_fused_conv_body, n1=n1, n2=n2, pitch=pitch),
        out_shape=jax.ShapeDtypeStruct((bsz * n, c), F32),
        grid=(c // LANES, bsz),
        in_specs=[seq(row0_x), seq(row0_g), pl.BlockSpec((1, LANES), lambda j, b: (0, j)),
                  pl.BlockSpec((None, n1, 2 * n2, LANES), lambda j, b: (j, 0, 0, 0)),
                  const(g_hi), const(fb_hi), const(fbi_hi)],
        out_specs=pl.BlockSpec((n, LANES), lambda j, b: (b, j)),
        scratch_shapes=[pltpu.VMEM((2 * n1 * pitch, LANES), F32), pltpu.VMEM((n, LANES), F32)],
        compiler_params=_cparams(("parallel", "parallel"), VMEM_LIMIT),
        name="hyena_conv_long",
    )(x, xg, skip.reshape(1, c), hspec, g_hi, fb_hi, fbi_hi)


def _hy_gate_body(x_ref, y_ref, z_ref, s_ref, o_ref):
    o_ref[...] = x_ref[...] * (y_ref[...] + z_ref[...] * s_ref[...])


def _hy_gate_call(x, y, z, skip):
    t, c = x.shape
    tok = pl.BlockSpec((512, c), lambda i: (i, 0))
    return pl.pallas_call(
        _hy_gate_body,
        out_shape=jax.ShapeDtypeStruct((t, c), F32),
        grid=(t // 512,),
        in_specs=[tok, tok, tok, pl.BlockSpec((1, c), lambda i: (0, 0))],
        out_specs=tok,
        compiler_params=_cparams(("parallel",)),
        name="hyena_gate",
    )(x, y, z, skip.reshape(1, c))


def _hyena_branch(hy, p, groups, n_ctx_blocks):
    u, x1, x2 = _hy_pre_call(hy, p['hy_conv_w'], p['hy_conv_b'], n_ctx_blocks)
    outs = []
    for start, bsz, n in groups:
        rows = bsz * n
        plan = _dft_plan(n)
        hmul = _long_conv_setup(n, _hy_filter_call(n, p))
        if plan['stages'] == 1:
            ug, x1g, x2g = (a[start:start + rows] for a in (u, x1, x2))
            y = _long_conv(ug.reshape(bsz, n, HY_WIDTH), hmul[0]).reshape(rows, HY_WIDTH)
            z = _hy_gate_call(x1g, y, ug, p['hy_skip'][0])
            y = _long_conv(z.reshape(bsz, n, HY_WIDTH), hmul[1]).reshape(rows, HY_WIDTH)
            outs.append(_hy_gate_call(x2g, y, z, p['hy_skip'][1]))
        else:
            z = _fused_conv_call(u, x1, p['hy_skip'][0], hmul[0], plan, start, start, bsz, n)
            outs.append(_fused_conv_call(z, x2, p['hy_skip'][1], hmul[1], plan, 0, start, bsz, n))
    return jnp.concatenate(outs, axis=0)


def _merge_body(x_ref, yfw_ref, ybw_ref, bonus_ref, g_ref, yb_ref, gt_ref, g1_ref, sh2_ref, sc2_ref, seg_ref,
                gng_ref, gnb_ref, wpa_ref, wpb_ref, wout_ref, lng_ref, lnb_ref, rwh_ref, rwl_ref, rb_ref,
                x1_ref, h2_ref, ti_ref, tg_ref, *, alpha):
    seg = seg_ref[...]
    y = yfw_ref[...] + ybw_ref[...]
    inv = 1.0 / RW_HEAD
    mu = _dot_x3(y, seg) * inv
    yc = y - mu
    var = _dot_x3(yc * yc, seg) * inv
    yn = yc * lax.rsqrt(var + GN_EPS) * gng_ref[...] + gnb_ref[...]
    y_a = (yn + bonus_ref[...]) * g_ref[...]
    gt = gt_ref[...]
    merged = gt[:, :D_MODEL] * _bdot(y_a, wpa_ref[...]) + gt[:, D_MODEL:] * _bdot(yb_ref[...], wpb_ref[...])
    mix = _bdot(merged, wout_ref[...])
    x1 = _layer_norm(alpha * x_ref[...] + g1_ref[...] * mix, lng_ref[...], lnb_ref[...])
    x1_ref[...] = x1
    h2 = x1 * (1.0 + sc2_ref[...]) + sh2_ref[...]
    _store_token_tiles(h2_ref, 0, h2)
    h_hi, h_lo = _split2(h2)
    d = functools.partial(jnp.dot, preferred_element_type=F32)
    logits = d(h_hi, rwh_ref[...]) + (d(h_hi, rwl_ref[...]) + d(h_lo, rwh_ref[...])) + rb_ref[...]
    lane = lax.broadcasted_iota(jnp.int32, logits.shape, 1)
    neg = jnp.float32(-jnp.inf)
    cur = jnp.where(lane < N_EXPERTS, logits, neg)
    top_i = jnp.zeros(logits.shape, jnp.int32)
    top_e = jnp.zeros(logits.shape, F32)
    den = jnp.zeros((logits.shape[0], 1), F32)
    v0 = None
    for j in range(TOP_K):
        mx = jnp.max(cur, axis=-1, keepdims=True)
        idx = jnp.min(jnp.where(cur == mx, lane, LANES), axis=-1, keepdims=True)
        if j == 0:
            v0 = mx
        e = jnp.exp(mx - v0)
        den = den + e
        top_i = jnp.where(lane == j, idx, top_i)
        top_e = jnp.where(lane == j, e, top_e)
        cur = jnp.where(lane == idx, neg, cur)
    ti_ref[...] = top_i
    tg_ref[...] = top_e / den


def _merge_call(x, yfw, ybw, bonus, g, yb, gates, mod_l, p, seg, cond_idx, alpha):
    t = x.shape[0]
    w = RW_WIDTH
    tok = lambda n: pl.BlockSpec((TOKEN_BLOCK, n), lambda i: (i, 0))
    modspec = lambda j: pl.BlockSpec((None, 1, D_MODEL), lambda i: (cond_idx(i), 0, j))
    full = lambda *s: pl.BlockSpec(s, lambda i: (0,) * len(s))
    rw_pad = jnp.zeros((D_MODEL, LANES), F32).at[:, :N_EXPERTS].set(p['router_w'])
    rw_hi = rw_pad.astype(BF16)
    rw_lo = (rw_pad - rw_hi.astype(F32)).astype(BF16)
    rb = jnp.zeros((1, LANES), F32).at[0, :N_EXPERTS].set(p['router_b'])
    o = jax.ShapeDtypeStruct((t, D_MODEL), F32)
    return pl.pallas_call(
        functools.partial(_merge_body, alpha=alpha),
        out_shape=(o, jax.ShapeDtypeStruct((t * TOKEN_TILE, LANES), F32),
                   jax.ShapeDtypeStruct((t, LANES), jnp.int32), jax.ShapeDtypeStruct((t, LANES), F32)),
        grid=(t // TOKEN_BLOCK,),
        in_specs=[tok(D_MODEL), tok(w), tok(w), tok(w), tok(w), tok(w),
                  tok(2 * D_MODEL), modspec(2), modspec(3), modspec(4), full(w, w), full(1, w), full(1, w),
                  full(w, D_MODEL), full(w, D_MODEL), full(D_MODEL, D_MODEL), full(1, D_MODEL), full(1, D_MODEL),
                  full(D_MODEL, LANES), full(D_MODEL, LANES), full(1, LANES)],
        out_specs=(tok(D_MODEL), pl.BlockSpec((TOKEN_BLOCK * TOKEN_TILE, LANES), lambda i: (i, 0)),
                   tok(LANES), tok(LANES)),
        compiler_params=_cparams(("parallel",), VMEM_LIMIT),
        name="merge_ln_router",
    )(x, yfw, ybw, bonus, g, yb, gates, mod_l, mod_l, mod_l, seg, p['gn_g'].reshape(1, w), p['gn_b'].reshape(1, w),
      p['w_pa'].astype(BF16), p['w_pb'].astype(BF16), p['w_out'].astype(BF16),
      p['ln1_g'].reshape(1, D_MODEL), p['ln1_b'].reshape(1, D_MODEL), rw_hi, rw_lo, rb)


DEINT_COLS = 256
DEINT_GROUPS = 2


def _deint_body(w_ref, p_ref, g_ref, l_ref):
    half = DEINT_COLS // 2
    for q in range(DEINT_GROUPS):
        y = jnp.dot(w_ref[:, q * DEINT_COLS:(q + 1) * DEINT_COLS].astype(BF16), p_ref[...],
                    preferred_element_type=F32)
        g_ref[:, q * half:(q + 1) * half] = y[:, :half].astype(BF16)
        l_ref[:, q * half:(q + 1) * half] = y[:, half:].astype(BF16)


def _deint_call(w, layer):
    _, e, k, n2 = w.shape
    half = DEINT_COLS // 2
    step_cols = DEINT_COLS * DEINT_GROUPS
    sel = np.zeros((DEINT_COLS, DEINT_COLS), np.float32)
    sel[2 * np.arange(half), np.arange(half)] = 1.0
    sel[2 * np.arange(half) + 1, half + np.arange(half)] = 1.0
    o = jax.ShapeDtypeStruct((e, k, n2 // 2), BF16)
    return pl.pallas_call(
        _deint_body,
        out_shape=(o, o),
        grid=(e, n2 // step_cols),
        in_specs=[pl.BlockSpec((None, None, k, step_cols), lambda i, j: (layer, i, 0, j)),
                  pl.BlockSpec((DEINT_COLS, DEINT_COLS), lambda i, j: (0, 0))],
        out_specs=(pl.BlockSpec((None, k, step_cols // 2), lambda i, j: (i, 0, j)),) * 2,
        compiler_params=_cparams(("parallel", "parallel")),
        name="expert_w_split",
    )(w, jnp.asarray(sel, BF16))


def _moe_body(blk_e_ref, n_on_ref, tok_ref, tokn_ref, dst_ref, h_hbm, wg_ref, wl_ref, bg_ref, bl_ref, wdn_ref,
              bdn_ref, y_hbm, xbuf, ybuf, sem_in, sem_out, *, n_real):
    i = pl.program_id(0)
    n_on = n_on_ref[0]
    slot = i % 2
    tt = TOKEN_TILE
    slot_rows = MOE_ROWS * tt

    def tile(ref, idx):
        return ref.at[pl.ds(pl.multiple_of(idx * tt, tt), tt), :]

    def gather_start(tref, s):
        def body(r, c):
            pltpu.make_async_copy(tile(h_hbm, tref[0, r]), tile(xbuf, s * MOE_ROWS + r), sem_in.at[s]).start()
            return c
        lax.fori_loop(0, MOE_ROWS, body, 0, unroll=8)

    def slot_buf(buf, s):
        return buf.at[pl.ds(pl.multiple_of(s * slot_rows, slot_rows), slot_rows), :]

    def gather_wait(s):
        pltpu.make_async_copy(h_hbm.at[pl.ds(0, slot_rows), :], slot_buf(xbuf, s), sem_in.at[s]).wait()

    def scatter_start(s):
        def body(r, c):
            pltpu.make_async_copy(tile(ybuf, s * MOE_ROWS + r), tile(y_hbm, dst_ref[0, r]), sem_out.at[s]).start()
            return c
        lax.fori_loop(0, MOE_ROWS, body, 0, unroll=8)

    def scatter_wait(s):
        pltpu.make_async_copy(slot_buf(ybuf, s), y_hbm.at[pl.ds(0, slot_rows), :], sem_out.at[s]).wait()

    @pl.when(i == 0)
    def _():
        ybuf[...] = jnp.zeros_like(ybuf)
        fills = [pltpu.make_async_copy(slot_buf(ybuf, s),
                                       y_hbm.at[pl.ds((n_real + s * MOE_ROWS) * tt, slot_rows), :],
                                       sem_out.at[s]) for s in range(2)]
        for cp in fills:
            cp.start()
        for cp in fills:
            cp.wait()

    @pl.when(jnp.logical_and(i == 0, n_on > 0))
    def _():
        gather_start(tok_ref, 0)

    @pl.when(i < n_on)
    def _():
        @pl.when(i + 1 < n_on)
        def _():
            gather_start(tokn_ref, 1 - slot)

        gather_wait(slot)
        row0 = slot * slot_rows
        x = _load_token_tiles(xbuf, row0, MOE_ROWS).astype(BF16)
        d = functools.partial(jnp.dot, preferred_element_type=F32)
        glu = jnp.minimum(d(x, wg_ref[...]) + bg_ref[...], SWIGLU_LIMIT)
        lin = jnp.clip(d(x, wl_ref[...]) + bl_ref[...], -SWIGLU_LIMIT, SWIGLU_LIMIT)
        act = glu * _sigmoid(SWIGLU_ALPHA * glu) * (lin + 1.0)
        _store_token_tiles(ybuf, row0, d(act.astype(BF16), wdn_ref[...]) + bdn_ref[...])
        scatter_start(slot)

        @pl.when(i >= 1)
        def _():
            scatter_wait(1 - slot)

        @pl.when(i == n_on - 1)
        def _():
            scatter_wait(slot)


def _moe_call(h2, top_i, wg, wl, bg, bl, wdn, bdn):
    t = h2.shape[0] // TOKEN_TILE
    m = t * TOP_K
    e = N_EXPERTS
    blk = MOE_ROWS
    flat_e = top_i.reshape(-1)
    order = jnp.argsort(flat_e, stable=True).astype(jnp.int32)
    sizes = jnp.bincount(flat_e, length=e).astype(jnp.int32)
    padded = (sizes + blk - 1) // blk * blk
    pad_end = jnp.cumsum(padded)
    pad_start = pad_end - padded
    grp_start = jnp.cumsum(sizes) - sizes
    n_blocks = -(-(m + e * (blk - 1)) // blk)
    blk_first = jnp.arange(n_blocks, dtype=jnp.int32) * blk
    blk_e = jnp.minimum(jnp.sum((pad_end[None, :] <= blk_first[:, None]).astype(jnp.int32), axis=1), e - 1)
    pidx = jnp.arange(n_blocks * blk, dtype=jnp.int32)
    e_p = jnp.repeat(blk_e, blk)
    idx = pidx - pad_start[e_p]
    valid = idx < sizes[e_p]
    assign = order[jnp.clip(grp_start[e_p] + idx, 0, m - 1)]
    tok_row = jnp.where(valid, assign // TOP_K, 0).astype(jnp.int32)
    spare = m + ((pidx // blk) % 2) * blk + pidx % blk
    dst_row = jnp.where(valid, (assign % TOP_K) * t + assign // TOP_K, spare).astype(jnp.int32)
    n_on = (pad_end[-1] // blk).astype(jnp.int32).reshape(1)
    tok3 = tok_row.reshape(n_blocks, 1, blk)

    smem = lambda f: pl.BlockSpec((None, 1, blk), f, memory_space=pltpu.SMEM)
    wspec = lambda a, b: pl.BlockSpec((None, a, b), lambda i, be, no: (be[i], 0, 0))
    gs = pltpu.PrefetchScalarGridSpec(
        num_scalar_prefetch=2,
        grid=(n_blocks,),
        in_specs=[smem(lambda i, be, no: (i, 0, 0)),
                  smem(lambda i, be, no: (jnp.minimum(i + 1, n_blocks - 1), 0, 0)),
                  smem(lambda i, be, no: (i, 0, 0)),
                  pl.BlockSpec(memory_space=pl.ANY),
                  wspec(D_MODEL, D_FF), wspec(D_MODEL, D_FF), wspec(1, D_FF), wspec(1, D_FF),
                  wspec(D_FF, D_MODEL), wspec(1, D_MODEL)],
        out_specs=pl.BlockSpec(memory_space=pl.ANY),
        scratch_shapes=[pltpu.VMEM((2 * blk * TOKEN_TILE, LANES), F32), pltpu.VMEM((2 * blk * TOKEN_TILE, LANES), F32),
                        pltpu.SemaphoreType.DMA((2,)), pltpu.SemaphoreType.DMA((2,))],
    )
    return pl.pallas_call(
        functools.partial(_moe_body, n_real=m),
        out_shape=jax.ShapeDtypeStruct(((m + 2 * blk) * TOKEN_TILE, LANES), F32),
        grid_spec=gs,
        compiler_params=_cparams(("arbitrary",), VMEM_LIMIT),
        name="moe_experts",
    )(blk_e, n_on, tok3, tok3, dst_row.reshape(n_blocks, 1, blk), h2, wg, wl, bg.reshape(e, 1, D_FF),
      bl.reshape(e, 1, D_FF), wdn, bdn.reshape(e, 1, D_MODEL))


def _combine_body(x1_ref, y0_ref, y1_ref, y2_ref, y3_ref, tg_ref, g2_ref, lng_ref, lnb_ref, o_ref, *, alpha):
    tg = tg_ref[...]
    rows = x1_ref.shape[0]
    moe = tg[:, 0:1] * _load_token_tiles(y0_ref, 0, rows)
    for j, y_ref in enumerate((y1_ref, y2_ref, y3_ref), start=1):
        moe = moe + tg[:, j:j + 1] * _load_token_tiles(y_ref, 0, rows)
    o_ref[...] = _layer_norm(alpha * x1_ref[...] + g2_ref[...] * moe, lng_ref[...], lnb_ref[...])


def _combine_call(x1, yexp, tg, mod_l, p, cond_idx, alpha):
    t = x1.shape[0]
    nb = t // TOKEN_BLOCK
    tok = lambda n: pl.BlockSpec((TOKEN_BLOCK, n), lambda i: (i, 0))
    full = lambda *s: pl.BlockSpec(s, lambda i: (0,) * len(s))
    yspec = [pl.BlockSpec((TOKEN_BLOCK * TOKEN_TILE, LANES), lambda i, j=j: (j * nb + i, 0)) for j in range(TOP_K)]
    return pl.pallas_call(
        functools.partial(_combine_body, alpha=alpha),
        out_shape=jax.ShapeDtypeStruct((t, D_MODEL), F32),
        grid=(nb,),
        in_specs=[tok(D_MODEL)] + yspec + [tok(LANES),
                  pl.BlockSpec((None, 1, D_MODEL), lambda i: (cond_idx(i), 0, 5)),
                  full(1, D_MODEL), full(1, D_MODEL)],
        out_specs=tok(D_MODEL),
        compiler_params=_cparams(("parallel",), VMEM_LIMIT),
        name="moe_combine_ln",
    )(x1, yexp, yexp, yexp, yexp, tg, mod_l, p['ln2_g'].reshape(1, D_MODEL), p['ln2_b'].reshape(1, D_MODEL))


def kernel(x_prompt, x_sample, c, state_rwkv, c_ctx, w_mod, b_mod, w_in, mu_shift, w0, w_lora_up, a0, a_lora_up, g_up, k_k, k_a, r_k, gn_g, gn_b, hy_conv_w, hy_conv_b, hy_f_w1, hy_f_b1, hy_f_w2, hy_f_b2, hy_f_freq, hy_f_w3, hy_skip, w_pa, w_pb, w_out, ln1_g, ln1_b, ln2_g, ln2_b, router_w, router_b, ex_w_up, ex_b_up, ex_w_down, ex_b_down):
    bsz, seq, dm = x_prompt.shape
    dbsz, dseq, _ = x_sample.shape
    depth = w_mod.shape[0]
    assert dm == D_MODEL and seq == TOKEN_BLOCK and dseq % TOKEN_BLOCK == 0 and TOKEN_BLOCK % GRID_W == 0
    assert 1 + dbsz <= 8
    alpha = (2 * depth) ** 0.25
    t_ctx = bsz * seq
    n_ctx_blocks = t_ctx // TOKEN_BLOCK
    lat_blocks = dseq // TOKEN_BLOCK

    def cond_idx(i):
        return jnp.where(i < n_ctx_blocks, 0, 1 + (i - n_ctx_blocks) // lat_blocks)

    x = jnp.concatenate([x_prompt.reshape(t_ctx, dm), x_sample.reshape(dbsz * dseq, dm)], axis=0)
    cond8 = jnp.zeros((8, dm), F32).at[0].set(c_ctx).at[1:1 + dbsz].set(c)
    mod = _mod_call(cond8, w_mod, b_mod)

    hd = RW_HEAD
    seg = (np.arange(RW_WIDTH)[:, None] // hd == np.arange(RW_WIDTH)[None, :] // hd)
    seg = jnp.asarray(seg, BF16)
    seq_lens = [seq] * bsz + [dseq] * dbsz
    tbl_np, nsteps = _scan_table(seq_lens)
    tbl = jnp.asarray(tbl_np)
    groups = [(0, bsz, seq), (t_ctx, dbsz, dseq)]

    new_states = []
    for l in range(depth):
        p = dict(mu_shift=mu_shift[l], w0=w0[l], w_lora_up=w_lora_up[l], a0=a0[l], a_lora_up=a_lora_up[l],
                 g_up=g_up[l], k_k=k_k[l], k_a=k_a[l], r_k=r_k[l], gn_g=gn_g[l], gn_b=gn_b[l],
                 hy_conv_w=hy_conv_w[l], hy_conv_b=hy_conv_b[l], hy_f_w1=hy_f_w1[l], hy_f_b1=hy_f_b1[l],
                 hy_f_w2=hy_f_w2[l], hy_f_b2=hy_f_b2[l], hy_f_freq=hy_f_freq[l], hy_f_w3=hy_f_w3[l],
                 hy_skip=hy_skip[l], w_pa=w_pa[l], w_pb=w_pb[l], w_out=w_out[l], ln1_g=ln1_g[l], ln1_b=ln1_b[l],
                 ln2_g=ln2_g[l], ln2_b=ln2_b[l], router_w=router_w[l], router_b=router_b[l])
        mod_l = mod[l].reshape(8, 1, 6 * dm)
        rw, hy, gates = _inproj_call(x, mod_l, w_in[l].astype(BF16), cond_idx)
        r, kk, v, lw, b, kd, g, bonus = _rwkv_pre_call(rw, p, seg, n_ctx_blocks)
        s0 = jnp.concatenate([jnp.zeros((bsz, 2, RW_HEADS, hd, hd), F32), state_rwkv[:, l].astype(F32)], axis=0)
        yfw, ybw, sfin = _scan_call(tbl, nsteps, bsz + dbsz, r, kk, v, lw, b, kd, s0)
        new_states.append(sfin[:bsz].astype(x_prompt.dtype))
        yb = _hyena_branch(hy, p, groups, n_ctx_blocks)
        x1, h2, top_i, top_g = _merge_call(x, yfw, ybw, bonus, g, yb, gates, mod_l, p, seg, cond_idx, alpha)
        wg, wl = _deint_call(ex_w_up, l)
        yexp = _moe_call(h2, top_i[:, :TOP_K], wg, wl, ex_b_up[l][:, 0::2], ex_b_up[l][:, 1::2],
                         ex_w_down[l].astype(BF16), ex_b_down[l])
        x = _combine_call(x1, yexp, top_g, mod_l, p, cond_idx, alpha)

    y_p = x[:t_ctx].reshape(bsz, seq, dm)
    y_s = x[t_ctx:].reshape(dbsz, dseq, dm)
    return (y_p, y_s, jnp.stack(new_states, axis=1))
```

```python
import functools
import math

import numpy as np
import jax
import jax.numpy as jnp
from jax import lax
from jax.experimental import pallas as pl
from jax.experimental.pallas import tpu as pltpu

F32 = jnp.float32
BF16 = jnp.bfloat16

D_MODEL = 1024
GRID_W = 64
RW_WIDTH = 512
RW_HEAD = 64
RW_HEADS = RW_WIDTH // RW_HEAD
LORA_W = 64
LORA_A = 64
LORA_G = 128
GN_EPS = 64e-5
HY_WIDTH = 512
HY_ORDER = 2
HY_BANDS = 16
HY_EMB = 2 * HY_BANDS + 1
HY_FFN = 64
HY_DECAY_TARGET = 1e-2
HY_DECAY_SHORT_PCT = 0.3
HY_DECAY_LONG_PCT = 1.5
RW_COLS = 3 * RW_WIDTH + LORA_W + LORA_A + LORA_G
HY_COLS = (HY_ORDER + 1) * HY_WIDTH
IN_COLS = RW_COLS + HY_COLS + 2 * D_MODEL
N_EXPERTS = 32
TOP_K = 4
D_FF = D_MODEL
SWIGLU_ALPHA = 1.702
SWIGLU_LIMIT = 7.0
LN_EPS = 1e-5

TOKEN_BLOCK = 256
SCAN_CHUNK = 64
INV_BLOCK = 16
DFT_N1 = 128
MOE_ROWS = 256
LANES = 128
VMEM_LIMIT = 56 * 1024 * 1024


def _cparams(sem, vmem=None):
    return pltpu.CompilerParams(dimension_semantics=sem, vmem_limit_bytes=vmem)


def _bdot(a, b):
    return jnp.dot(a.astype(BF16), b.astype(BF16), preferred_element_type=F32)


def _bdg(a, b, ca, cb):
    return lax.dot_general(a.astype(BF16), b.astype(BF16), (((ca,), (cb,)), ((0,), (0,))), preferred_element_type=F32)


def _bmm(a, b):
    return _bdg(a, b, 2, 1)


def _bmm_nt(a, b):
    return _bdg(a, b, 2, 2)


def _bmm_tn(a, b):
    return _bdg(a, b, 1, 1)


def _split2(x):
    hi = x.astype(BF16)
    lo = (x - hi.astype(F32)).astype(BF16)
    return hi, lo


def _split3(x):
    hi = x.astype(BF16)
    r1 = x - hi.astype(F32)
    mid = r1.astype(BF16)
    lo = (r1 - mid.astype(F32)).astype(BF16)
    return hi, mid, lo


def _dot_x2(x, g_bf16):
    hi, lo = _split2(x)
    d = functools.partial(jnp.dot, preferred_element_type=F32)
    return d(hi, g_bf16) + d(lo, g_bf16)


def _dot3(x, g_hi, g_lo):
    x_hi, x_lo = _split2(x)
    d = functools.partial(jnp.dot, preferred_element_type=F32)
    return d(x_hi, g_hi) + (d(x_hi, g_lo) + d(x_lo, g_hi))


def _sigmoid(x):
    return 1.0 / (1.0 + jnp.exp(-x))


TOKEN_TILE = D_MODEL // LANES


def _store_token_tiles(ref, row0, x):
    rows = x.shape[0]
    for s in range(TOKEN_TILE):
        ref[pl.ds(row0 + s, rows, stride=TOKEN_TILE), :] = x[:, s * LANES:(s + 1) * LANES]


def _load_token_tiles(ref, row0, rows):
    return jnp.concatenate([ref[pl.ds(row0 + s, rows, stride=TOKEN_TILE), :] for s in range(TOKEN_TILE)], axis=1)


def _layer_norm(x, g, b):
    mu = jnp.mean(x, axis=-1, keepdims=True)
    xc = x - mu
    var = jnp.mean(xc * xc, axis=-1, keepdims=True)
    return xc * lax.rsqrt(var + LN_EPS) * g + b


def _mod_body(c_ref, w_ref, b_ref, o_ref):
    c = c_ref[...]
    o_ref[...] = _bdot(c * _sigmoid(c), w_ref[...]) + b_ref[...]


def _mod_call(cond8, w_mod, b_mod):
    depth = w_mod.shape[0]
    tn = 1536
    return pl.pallas_call(
        _mod_body,
        out_shape=jax.ShapeDtypeStruct((depth, 8, 6 * D_MODEL), F32),
        grid=(depth, 6 * D_MODEL // tn),
        in_specs=[pl.BlockSpec((8, D_MODEL), lambda l, j: (0, 0)),
                  pl.BlockSpec((None, D_MODEL, tn), lambda l, j: (l, 0, j)),
                  pl.BlockSpec((None, 1, tn), lambda l, j: (l, 0, j))],
        out_specs=pl.BlockSpec((None, 8, tn), lambda l, j: (l, 0, j)),
        compiler_params=_cparams(("parallel", "parallel"), VMEM_LIMIT),
        name="adaln_mod",
    )(cond8, w_mod, b_mod.reshape(depth, 1, 6 * D_MODEL))


def _inproj_body(x_ref, sh_ref, sc_ref, w_ref, rw_ref, hy_ref, gt_ref):
    h = (x_ref[...] * (1.0 + sc_ref[...]) + sh_ref[...]).astype(BF16)
    d = functools.partial(jnp.dot, preferred_element_type=F32)
    rw_ref[...] = d(h, w_ref[:, :RW_COLS])
    hy_ref[...] = d(h, w_ref[:, RW_COLS:RW_COLS + HY_COLS])
    gt_ref[...] = _sigmoid(d(h, w_ref[:, RW_COLS + HY_COLS:]))


def _inproj_call(x, mod_l, w_in_bf16, cond_idx):
    t = x.shape[0]
    nb = t // TOKEN_BLOCK
    tok = lambda n: pl.BlockSpec((TOKEN_BLOCK, n), lambda i: (i, 0))
    modspec = lambda j: pl.BlockSpec((None, 1, D_MODEL), lambda i: (cond_idx(i), 0, j))
    return pl.pallas_call(
        _inproj_body,
        out_shape=(jax.ShapeDtypeStruct((t, RW_COLS), F32), jax.ShapeDtypeStruct((t, HY_COLS), F32),
                   jax.ShapeDtypeStruct((t, 2 * D_MODEL), F32)),
        grid=(nb,),
        in_specs=[tok(D_MODEL), modspec(0), modspec(1),
                  pl.BlockSpec((D_MODEL, IN_COLS), lambda i: (0, 0))],
        out_specs=(tok(RW_COLS), tok(HY_COLS), tok(2 * D_MODEL)),
        compiler_params=_cparams(("parallel",), VMEM_LIMIT),
        name="in_proj",
    )(x, mod_l, mod_l, w_in_bf16)


def _neighbours(x, n_ctx_blocks):
    rows = x.shape[0]
    row = lax.broadcasted_iota(jnp.int32, (rows, 1), 0)
    seg_mask = jnp.where(pl.program_id(0) < n_ctx_blocks, rows - 1, GRID_W - 1)
    pos = row & seg_mask
    prev = jnp.where(pos == 0, 0.0, pltpu.roll(x, 1, 0))
    nxt = jnp.where(pos == seg_mask, 0.0, pltpu.roll(x, rows - 1, 0))
    return prev, nxt


def _rwkv_pre_body(rw_ref, mu_ref, kk_s_ref, ka_ref, w0_ref, a0_ref, wl_ref, al_ref, gup_ref, rk_ref, seg_ref,
                   r_ref, kk_ref, v_ref, lw_ref, b_ref, kd_ref, g_ref, bonus_ref, *, n_ctx_blocks):
    x = rw_ref[...]
    prev, nxt = _neighbours(x, n_ctx_blocks)
    cols = x + mu_ref[...] * (0.5 * (prev + nxt) - x)
    w = RW_WIDTH
    r = cols[:, :w]
    k = cols[:, w:2 * w]
    v = cols[:, 2 * w:3 * w]
    wd = cols[:, 3 * w:3 * w + LORA_W]
    ad = cols[:, 3 * w + LORA_W:3 * w + LORA_W + LORA_A]
    gd = cols[:, 3 * w + LORA_W + LORA_A:]
    seg = seg_ref[...]
    kkr = k * kk_s_ref[...]
    ss = _dot_x2(kkr * kkr, seg)
    kk = kkr / jnp.maximum(jnp.sqrt(ss), 1e-12)
    r_ref[...] = r
    kk_ref[...] = kk
    v_ref[...] = v
    g_ref[...] = _bdot(_sigmoid(gd), gup_ref[...])
    tw = jnp.tanh(wd)
    bonus = jnp.zeros_like(r)
    for d in range(2):
        w_logit = w0_ref[d:d + 1, :] + _bdot(tw, wl_ref[d])
        lw_ref[d] = -math.exp(-0.5) * _sigmoid(w_logit)
        a = _sigmoid(a0_ref[d:d + 1, :] + _bdot(ad, al_ref[d]))
        kd = k * (1.0 + (a - 1.0) * ka_ref[...])
        kd_ref[d] = kd
        b_ref[d] = kk * a
        bonus = bonus + _dot_x2(r * kd * rk_ref[d:d + 1, :], seg) * v
    bonus_ref[...] = bonus


def _rwkv_pre_call(rw, p, seg, n_ctx_blocks):
    t = rw.shape[0]
    nb = t // TOKEN_BLOCK
    w = RW_WIDTH
    tok = lambda n: pl.BlockSpec((TOKEN_BLOCK, n), lambda i: (i, 0))
    tok2 = pl.BlockSpec((2, TOKEN_BLOCK, w), lambda i: (0, i, 0))
    full = lambda *s: pl.BlockSpec(s, lambda i: (0,) * len(s))
    o1 = jax.ShapeDtypeStruct((t, w), F32)
    o2 = jax.ShapeDtypeStruct((2, t, w), F32)
    return pl.pallas_call(
        functools.partial(_rwkv_pre_body, n_ctx_blocks=n_ctx_blocks),
        out_shape=(o1, o1, o1, o2, o2, o2, o1, o1),
        grid=(nb,),
        in_specs=[tok(RW_COLS), full(1, RW_COLS), full(1, w), full(1, w), full(2, w), full(2, w),
                  full(2, LORA_W, w), full(2, LORA_A, w), full(LORA_G, w), full(2, w), full(w, w)],
        out_specs=(tok(w), tok(w), tok(w), tok2, tok2, tok2, tok(w), tok(w)),
        compiler_params=_cparams(("parallel",), VMEM_LIMIT),
        name="rwkv_pre",
    )(rw, p['mu_shift'].reshape(1, RW_COLS), p['k_k'].reshape(1, w), p['k_a'].reshape(1, w), p['w0'], p['a0'],
      p['w_lora_up'], p['a_lora_up'], p['g_up'], p['r_k'].reshape(2, w), seg)


def _scan_body(tbl_ref, rf_ref, kkf_ref, vf_ref, rb_ref, kkb_ref, vb_ref, lwf_ref, bf_ref, kdf_ref, lwb_ref,
               bb_ref, kdb_ref, s0_ref, yf_ref, yb_ref, sfin_ref, s_ref):
    base = pl.program_id(0) * 5
    c = SCAN_CHUNK
    hd = RW_HEAD
    nh = 2 * RW_HEADS

    @pl.when(tbl_ref[base + 3] == 1)
    def _():
        s_ref[...] = s0_ref[...].reshape(nh, hd, hd)

    rowi = lax.broadcasted_iota(jnp.int32, (c, c), 0)
    coli = lax.broadcasted_iota(jnp.int32, (c, c), 1)
    same_blk = (rowi // INV_BLOCK) == (coli // INV_BLOCK)
    eye = (rowi == coli).astype(F32)

    def both(fwd, bwd):
        return jnp.concatenate([jnp.broadcast_to(fwd, (RW_HEADS, c, c)), jnp.broadcast_to(bwd, (RW_HEADS, c, c))],
                               axis=0)

    diff = both(rowi - coli, coli - rowi)
    strict = diff > 0
    incl = diff >= 0

    def heads(a):
        return jnp.stack([a[:, h * hd:(h + 1) * hd] for h in range(RW_HEADS)], axis=0)

    def prep(r_ref, kk_ref, v_ref, lw_ref, b_ref, kd_ref, tri):
        lw = lw_ref[...]
        hi, mid, lo = _split3(lw)
        dd = functools.partial(jnp.dot, preferred_element_type=F32)
        cum = dd(tri, hi) + (dd(tri, mid) + dd(tri, lo))
        tot = jnp.sum(lw, axis=0, keepdims=True)
        e_neg = jnp.exp(-cum)
        e_rem = jnp.exp(tot - cum)
        kk = kk_ref[...]
        bb = b_ref[...]
        kd = kd_ref[...]
        return [heads(a) for a in (kk * jnp.exp(cum - lw), bb * e_neg, kd * e_neg, r_ref[...] * jnp.exp(cum),
                                   v_ref[...], kd * e_rem, bb * e_rem, jnp.exp(tot))]

    fw = prep(rf_ref, kkf_ref, vf_ref, lwf_ref, bf_ref, kdf_ref, (rowi >= coli).astype(BF16))
    bw = prep(rb_ref, kkb_ref, vb_ref, lwb_ref, bb_ref, kdb_ref, (rowi <= coli).astype(BF16))
    a_h, b_h, k_h, r_h, v_h, kapg_h, betg_h, gtot_h = [jnp.concatenate([f, b], axis=0) for f, b in zip(fw, bw)]
    ar = jnp.concatenate([a_h, r_h], axis=1)
    gb = _bmm_nt(ar, b_h)
    gk = _bmm_nt(ar, k_h)
    low = jnp.where(strict, gb[:, :c], 0.0)
    a_ka = jnp.where(strict, gk[:, :c], 0.0)
    a_br = jnp.where(incl, gb[:, c:], 0.0)
    a_kr = jnp.where(incl, gk[:, c:], 0.0)
    nd = jnp.where(same_blk, -low, 0.0)
    loff = jnp.where(same_blk, 0.0, low)
    x = eye + nd
    n2 = _bmm(nd, nd)
    x = x + _bmm(x, n2)
    n4 = _bmm(n2, n2)
    x = x + _bmm(x, n4)
    n8 = _bmm(n4, n4)
    x = x + _bmm(x, n8)
    m = _bmm(x, loff)
    m2 = _bmm(m, m)
    y1 = x + _bmm(m2, x)
    tinv = y1 - _bmm(m, y1)
    w_h = _bmm(a_ka, v_h)
    rhs = jnp.concatenate([a_h, w_h], axis=2)
    x0 = _bmm(tinv, rhs)
    res = rhs - x0 - _bmm(low, x0)
    xs = x0 + _bmm(tinv, res)
    p_h = xs[:, :, :hd]
    q_h = xs[:, :, hd:]
    s_old = s_ref[...]
    uy = _bmm_nt(jnp.concatenate([p_h, r_h], axis=1), s_old)
    u_h = uy[:, :c] + q_h
    y_h = uy[:, c:] + _bmm(a_kr, v_h) - _bmm(a_br, u_h)
    zv = jnp.concatenate([v_h, u_h], axis=1)
    zk = jnp.concatenate([kapg_h, -betg_h], axis=1)
    s_ref[...] = s_old * gtot_h + _bmm_tn(zv, zk)
    for h in range(RW_HEADS):
        yf_ref[:, h * hd:(h + 1) * hd] = y_h[h]
        yb_ref[:, h * hd:(h + 1) * hd] = y_h[RW_HEADS + h]

    @pl.when(tbl_ref[base + 4] == 1)
    def _():
        sfin_ref[...] = s_ref[...].reshape(2, RW_HEADS, hd, hd)


def _scan_call(tbl, nsteps, nseq, r, kk, v, lw, b, kd, s0):
    t = r.shape[0]
    w = RW_WIDTH
    c = SCAN_CHUNK
    blk = lambda d: pl.BlockSpec((c, w), lambda i, tb: (tb[i * 5 + d], 0))
    blk_dir = lambda d: pl.BlockSpec((None, c, w), lambda i, tb: (d, tb[i * 5 + d], 0))
    st = pl.BlockSpec((None, 2, RW_HEADS, RW_HEAD, RW_HEAD), lambda i, tb: (tb[i * 5 + 2], 0, 0, 0, 0))
    gs = pltpu.PrefetchScalarGridSpec(
        num_scalar_prefetch=1,
        grid=(nsteps,),
        in_specs=[blk(0), blk(0), blk(0), blk(1), blk(1), blk(1), blk_dir(0), blk_dir(0), blk_dir(0),
                  blk_dir(1), blk_dir(1), blk_dir(1), st],
        out_specs=(blk(0), blk(1), st),
        scratch_shapes=[pltpu.VMEM((2 * RW_HEADS, RW_HEAD, RW_HEAD), F32)],
    )
    o = jax.ShapeDtypeStruct((t, w), F32)
    return pl.pallas_call(
        _scan_body,
        out_shape=(o, o, jax.ShapeDtypeStruct((nseq, 2, RW_HEADS, RW_HEAD, RW_HEAD), F32)),
        grid_spec=gs,
        compiler_params=_cparams(("arbitrary",), VMEM_LIMIT),
        name="rwkv_scan",
    )(tbl, r, kk, v, r, kk, v, lw, b, kd, lw, b, kd, s0)


def _scan_table(seq_lens):
    c = SCAN_CHUNK
    rows = []
    start = 0
    for s, n in enumerate(seq_lens):
        nc = n // c
        for j in range(nc):
            rows.append((start + j, start + nc - 1 - j, s, int(j == 0), int(j == nc - 1)))
        start += nc
    return np.asarray(rows, np.int32).reshape(-1), len(rows)


def _hy_pre_body(hy_ref, w_ref, b_ref, u_ref, x1_ref, x2_ref, *, n_ctx_blocks):
    x = hy_ref[...]
    prev, nxt = _neighbours(x, n_ctx_blocks)
    cols = prev * w_ref[0:1, :] + x * w_ref[1:2, :] + nxt * w_ref[2:3, :] + b_ref[...]
    w = HY_WIDTH
    u_ref[...] = cols[:, :w]
    x1_ref[...] = cols[:, w:2 * w]
    x2_ref[...] = cols[:, 2 * w:]


def _hy_pre_call(hy, conv_w, conv_b, n_ctx_blocks):
    t = hy.shape[0]
    tok = lambda n: pl.BlockSpec((TOKEN_BLOCK, n), lambda i: (i, 0))
    o = jax.ShapeDtypeStruct((t, HY_WIDTH), F32)
    return pl.pallas_call(
        functools.partial(_hy_pre_body, n_ctx_blocks=n_ctx_blocks),
        out_shape=(o, o, o),
        grid=(t // TOKEN_BLOCK,),
        in_specs=[tok(HY_COLS), pl.BlockSpec((3, HY_COLS), lambda i: (0, 0)),
                  pl.BlockSpec((1, HY_COLS), lambda i: (0, 0))],
        out_specs=(tok(HY_WIDTH),) * 3,
        compiler_params=_cparams(("parallel",)),
        name="hyena_pre",
    )(hy, conv_w, conv_b.reshape(1, HY_COLS))


def _dot_f32(a, b):
    a_hi, a_lo = _split2(a)
    b_hi, b_lo = _split2(b)
    d = functools.partial(jnp.dot, preferred_element_type=F32)
    return d(a_hi, b_hi) + (d(a_hi, b_lo) + d(a_lo, b_hi))


def _hy_filter_body(bands_ref, w1t_ref, w1c_ref, w1s_ref, b1_ref, w2_ref, b2_ref, fr_ref, w3_ref, dl_ref, f_ref,
                    *, n, tile):
    row = lax.broadcasted_iota(jnp.int32, (tile, LANES), 0) + pl.program_id(0) * tile
    fwd = row < n
    pos = jnp.where(fwd, row, 2 * n - row).astype(F32)
    tcol = pos / n
    ang = (2.0 * math.pi / n) * pos * bands_ref[...]
    pre1 = (tcol[:, :HY_FFN] * w1t_ref[...] + _dot_f32(jnp.cos(ang), w1c_ref[...])
            + _dot_f32(jnp.sin(ang), w1s_ref[...]) + b1_ref[...])
    fr = fr_ref[...]
    h1 = jnp.sin(fr * pre1)
    h2 = jnp.sin(fr * (_dot_f32(h1, w2_ref[...]) + b2_ref[...]))
    h = _dot_f32(h2, w3_ref[...])
    keep = row != n
    for q in range(HY_WIDTH // LANES):
        win = jnp.where(keep, jnp.exp(-tcol * dl_ref[:, q * LANES:(q + 1) * LANES]), 0.0)
        for o in range(HY_ORDER):
            lo = o * 2 * HY_WIDTH + q * LANES
            f_ref[o, :, q * LANES:(q + 1) * LANES] = jnp.where(
                fwd, h[:, lo:lo + LANES], h[:, lo + HY_WIDTH:lo + HY_WIDTH + LANES]) * win


def _hy_filter_call(n, p):
    bands = jnp.zeros((1, LANES), F32).at[0, :HY_BANDS].set(
        jnp.linspace(1e-4, HY_BANDS - 1, HY_BANDS, dtype=F32))
    max_decay = math.log(HY_DECAY_TARGET) / HY_DECAY_SHORT_PCT
    min_decay = math.log(HY_DECAY_TARGET) / HY_DECAY_LONG_PCT
    deltas = jnp.abs(jnp.linspace(min_decay, max_decay, HY_WIDTH, dtype=F32)).reshape(1, HY_WIDTH)
    w1 = p['hy_f_w1']
    w1t = w1[0:1]
    w1c = jnp.zeros((LANES, HY_FFN), F32).at[:HY_BANDS].set(w1[1:1 + HY_BANDS])
    w1s = jnp.zeros((LANES, HY_FFN), F32).at[:HY_BANDS].set(w1[1 + HY_BANDS:HY_EMB])
    nout = HY_ORDER * 2 * HY_WIDTH
    tile = min(2 * n, 512)
    full = lambda *s: pl.BlockSpec(s, lambda i: (0,) * len(s))
    return pl.pallas_call(
        functools.partial(_hy_filter_body, n=n, tile=tile),
        out_shape=jax.ShapeDtypeStruct((HY_ORDER, 2 * n, HY_WIDTH), F32),
        grid=(2 * n // tile,),
        in_specs=[full(1, LANES), full(1, HY_FFN), full(LANES, HY_FFN), full(LANES, HY_FFN), full(1, HY_FFN),
                  full(HY_FFN, HY_FFN), full(1, HY_FFN), full(1, HY_FFN), full(HY_FFN, nout), full(1, HY_WIDTH)],
        out_specs=pl.BlockSpec((HY_ORDER, tile, HY_WIDTH), lambda i: (0, i, 0)),
        compiler_params=_cparams(("parallel",), VMEM_LIMIT),
        name="hyena_filter",
    )(bands, w1t, w1c, w1s, p['hy_f_b1'].reshape(1, HY_FFN), p['hy_f_w2'], p['hy_f_b2'].reshape(1, HY_FFN),
      p['hy_f_freq'].reshape(1, HY_FFN), p['hy_f_w3'], deltas)


def _dft_body(*refs, pre, post, two, half_in, half_mid):
    it = iter(refs)
    x = next(it)[...]
    if pre:
        m1 = next(it)[...]
        m2 = next(it)[...]
        x = x * m1 + pltpu.roll(x, half_in, 1) * m2
    g1h = next(it)[...]
    g1l = next(it)[...]
    y = _dot3(x, g1h, g1l)
    if post:
        m1 = next(it)[...]
        m2 = next(it)[...]
        y = y * m1 + pltpu.roll(y, half_mid, 1) * m2
    if two:
        g2h = next(it)[...]
        g2l = next(it)[...]
        y = _dot3(y, g2h, g2l)
    o_ref = next(it)
    o_ref[...] = y


def _dft_call(x, g1, tile, pre=None, post=None, g2=None, name="hyena_dft"):
    bsz, rows, k = x.shape
    args = [x]
    specs = [pl.BlockSpec((None, tile, k), lambda b, j: (b, j, 0))]

    def add_mul(mm):
        for a in mm:
            nblk = a.shape[0] // tile
            specs.append(pl.BlockSpec((tile, a.shape[1]), lambda b, j, nblk=nblk: (j % nblk, 0)))
            args.append(a)

    def add_mat(g):
        for a in g:
            specs.append(pl.BlockSpec(a.shape, lambda b, j: (0, 0)))
            args.append(a)

    if pre is not None:
        add_mul(pre)
    add_mat(g1)
    if post is not None:
        add_mul(post)
    if g2 is not None:
        add_mat(g2)
    mid = g1[0].shape[1]
    nout = g2[0].shape[1] if g2 is not None else mid
    body = functools.partial(_dft_body, pre=pre is not None, post=post is not None, two=g2 is not None,
                             half_in=k // 2, half_mid=mid // 2)
    return pl.pallas_call(
        body,
        out_shape=jax.ShapeDtypeStruct((bsz, rows, nout), F32),
        grid=(bsz, rows // tile),
        in_specs=specs,
        out_specs=pl.BlockSpec((None, tile, nout), lambda b, j: (b, j, 0)),
        compiler_params=_cparams(("parallel", "parallel"), VMEM_LIMIT),
        name=name,
    )(*args)


def _hilo(a):
    a = np.asarray(a, np.float64)
    hi = jnp.asarray(a, F32).astype(BF16)
    lo = (jnp.asarray(a, F32) - hi.astype(F32)).astype(BF16)
    return hi, lo


@functools.lru_cache(maxsize=None)
def _dft_plan(n):
    big = 2 * n
    if big <= 512:
        t = np.arange(n)[:, None]
        k = np.arange(big)[None, :]
        fwd = np.exp(-2j * np.pi * t * k / big)
        fwd_full = np.exp(-2j * np.pi * np.arange(big)[:, None] * k / big)
        inv = np.exp(2j * np.pi * np.arange(big)[:, None] * np.arange(n)[None, :] / big) / big
        return dict(stages=1,
                    fwd=np.concatenate([fwd.real, fwd.imag], axis=1),
                    fwd_full=np.concatenate([fwd_full.real, fwd_full.imag], axis=1),
                    inv=np.concatenate([inv.real, -inv.imag], axis=0))
    n1, n2 = DFT_N1, big // DFT_N1
    k1 = np.arange(n1)[None, :, None]
    t = n2 * np.arange(n1)[None, None, :] + np.arange(n2)[:, None, None]
    ga = np.exp(-2j * np.pi * k1 * t / big)
    ga = np.concatenate([ga.real, ga.imag], axis=1)
    fb = np.exp(-2j * np.pi * np.arange(n2)[:, None] * np.arange(n2)[None, :] / n2)
    fbc = np.block([[fb.real, -fb.imag], [fb.imag, fb.real]])
    fbi = np.conj(fb) / big
    fbic = np.block([[fbi.real, -fbi.imag], [fbi.imag, fbi.real]])
    return dict(stages=2, n1=n1, n2=n2, ga=ga, fb=fbc, fb_inv=fbic)


def _long_conv_setup(n, f):
    c = HY_WIDTH
    plan = _dft_plan(n)
    if plan['stages'] == 1:
        xf = jnp.transpose(f, (0, 2, 1))
        spec = _dft_call(xf, _hilo(plan['fwd_full']), tile=min(c, 256), name="hyena_filter_dft")
        fr, fi = spec[..., :2 * n], spec[..., 2 * n:]
        return [(jnp.concatenate([fr[o], fr[o]], axis=1), jnp.concatenate([-fi[o], fi[o]], axis=1))
                for o in range(HY_ORDER)]
    return _fused_filter_call(f, plan)


def _long_conv(z, hmul):
    bsz, n, c = z.shape
    plan = _dft_plan(n)
    x = jnp.transpose(z, (0, 2, 1))
    y = _dft_call(x, _hilo(plan['fwd']), tile=min(c, 256), post=hmul, g2=_hilo(plan['inv']),
                  name="hyena_conv_short")
    return jnp.transpose(y, (0, 2, 1))


def _plane_pitch(n2):
    return n2 if (n2 // 8) % 2 == 1 else n2 + 8


def _dot3c(g_hi, g_lo, x):
    x_hi, x_lo = _split2(x)
    d = functools.partial(jnp.dot, preferred_element_type=F32)
    return d(g_hi, x_hi) + (d(g_hi, x_lo) + d(g_lo, x_hi))


def _gdot(g_hi, g_lo, x):
    if g_lo is None:
        return jnp.dot(g_hi, x.astype(BF16), preferred_element_type=F32)
    return _dot3c(g_hi, g_lo, x)


def _stage_one(x_ref, g_hi_ref, g_lo_ref, asc_ref, *, n2, nt1, n1, pitch):
    def body(t2, c):
        xs = x_ref[pl.ds(t2, nt1, stride=n2), :]
        g_lo = None if g_lo_ref is None else g_lo_ref[t2]
        asc_ref[pl.ds(t2, 2 * n1, stride=pitch), :] = _gdot(g_hi_ref[t2], g_lo, xs)
        return c
    lax.fori_loop(0, n2, body, 0, unroll=4)


def _plane(asc_ref, k1, n1, n2, pitch):
    o_re = pl.multiple_of(k1 * pitch, 8)
    o_im = pl.multiple_of((n1 + k1) * pitch, 8)
    return o_re, o_im, jnp.concatenate([asc_ref[pl.ds(o_re, n2), :], asc_ref[pl.ds(o_im, n2), :]], axis=0)


def _fused_filter_body(f_ref, g_hi_ref, g_lo_ref, fb_hi_ref, fb_lo_ref, h_ref, asc_ref, *, n1, n2, pitch):
    _stage_one(f_ref, g_hi_ref, g_lo_ref, asc_ref, n2=n2, nt1=n1, n1=n1, pitch=pitch)

    def body(k1, c):
        _, _, ain = _plane(asc_ref, k1, n1, n2, pitch)
        h_ref[k1] = _dot3c(fb_hi_ref[...], fb_lo_ref[...], ain)
        return c
    lax.fori_loop(0, n1, body, 0, unroll=8)


def _fused_filter_call(f, plan):
    order, big, c = f.shape
    n1, n2 = plan['n1'], plan['n2']
    pitch = _plane_pitch(n2)
    g_hi, g_lo = _hilo(plan['ga'])
    fb_hi, fb_lo = _hilo(plan['fb'])
    const = lambda a: pl.BlockSpec(a.shape, lambda o, j: (0,) * a.ndim)
    return pl.pallas_call(
        functools.partial(_fused_filter_body, n1=n1, n2=n2, pitch=pitch),
        out_shape=jax.ShapeDtypeStruct((order, c // LANES, n1, 2 * n2, LANES), F32),
        grid=(order, c // LANES),
        in_specs=[pl.BlockSpec((None, big, LANES), lambda o, j: (o, 0, j)),
                  const(g_hi), const(g_lo), const(fb_hi), const(fb_lo)],
        out_specs=pl.BlockSpec((None, None, n1, 2 * n2, LANES), lambda o, j: (o, j, 0, 0, 0)),
        scratch_shapes=[pltpu.VMEM((2 * n1 * pitch, LANES), F32)],
        compiler_params=_cparams(("parallel", "parallel"), VMEM_LIMIT),
        name="hyena_filter_spectrum",
    )(f, g_hi, g_lo, fb_hi, fb_lo)


def _fused_conv_body(x_ref, xg_ref, skip_ref, h_ref, g_ref, fb_ref, fbi_ref, o_ref, asc_ref, y_ref,
                     *, n1, n2, pitch):
    h1 = n1 // 2
    _stage_one(x_ref, g_ref, None, asc_ref, n2=n2, nt1=h1, n1=n1, pitch=pitch)

    def mid(k1, c):
        o_re, o_im, ain = _plane(asc_ref, k1, n1, n2, pitch)
        b = _gdot(fb_ref[...], None, ain)
        h = h_ref[k1]
        br, bi, hr, hi = b[:n2], b[n2:], h[:n2], h[n2:]
        z = jnp.concatenate([br * hr - bi * hi, br * hi + bi * hr], axis=0)
        cc = _gdot(fbi_ref[...], None, z)
        asc_ref[pl.ds(o_re, n2), :] = cc[:n2]
        asc_ref[pl.ds(o_im, n2), :] = cc[n2:]
        return c
    lax.fori_loop(0, n1, mid, 0, unroll=8)

    def last(t2, c):
        zin = asc_ref[pl.ds(t2, 2 * n1, stride=pitch), :]
        y_ref[pl.ds(t2, h1, stride=n2), :] = lax.dot_general(
            g_ref[t2], zin.astype(BF16), (((0,), (0,)), ((), ())), preferred_element_type=F32)
        return c
    lax.fori_loop(0, n2, last, 0, unroll=4)
    x = x_ref[...]
    o_ref[...] = xg_ref[...] * (y_ref[...] + x * skip_ref[...])


def _fused_conv_call(x, xg, skip, hspec, plan, row0_x, row0_g, bsz, n):
    assert row0_x % n == 0 and row0_g % n == 0
    c = x.shape[1]
    n1, n2 = plan['n1'], plan['n2']
    h1 = n1 // 2
    pitch = _plane_pitch(n2)
    g_hi = _hilo(plan['ga'][:, :, :h1])[0]
    fb_hi = _hilo(plan['fb'])[0]
    fbi_hi = _hilo(plan['fb_inv'])[0]
    const = lambda a: pl.BlockSpec(a.shape, lambda j, b: (0,) * a.ndim)
    seq = lambda row0: pl.BlockSpec((n, LANES), lambda j, b: (row0 // n + b, j))
    return pl.pallas_call(
        functools.partial(_fused_conv_body, n1=n1, n2=n2, pitch=pitch),
        out_shape=jax.ShapeDtypeStruct((bsz * n, c), F32),
        grid=(c // LANES, bsz),
        in_specs=[seq(row0_x), seq(row0_g), pl.BlockSpec((1, LANES), lambda j, b: (0, j)),
                  pl.BlockSpec((None, n1, 2 * n2, LANES), lambda j, b: (j, 0, 0, 0)),
                  const(g_hi), const(fb_hi), const(fbi_hi)],
        out_specs=pl.BlockSpec((n, LANES), lambda j, b: (b, j)),
        scratch_shapes=[pltpu.VMEM((2 * n1 * pitch, LANES), F32), pltpu.VMEM((n, LANES), F32)],
        compiler_params=_cparams(("parallel", "parallel"), VMEM_LIMIT),
        name="hyena_conv_long",
    )(x, xg, skip.reshape(1, c), hspec, g_hi, fb_hi, fbi_hi)


def _hy_gate_body(x_ref, y_ref, z_ref, s_ref, o_ref):
    o_ref[...] = x_ref[...] * (y_ref[...] + z_ref[...] * s_ref[...])


def _hy_gate_call(x, y, z, skip):
    t, c = x.shape
    tok = pl.BlockSpec((512, c), lambda i: (i, 0))
    return pl.pallas_call(
        _hy_gate_body,
        out_shape=jax.ShapeDtypeStruct((t, c), F32),
        grid=(t // 512,),
        in_specs=[tok, tok, tok, pl.BlockSpec((1, c), lambda i: (0, 0))],
        out_specs=tok,
        compiler_params=_cparams(("parallel",)),
        name="hyena_gate",
    )(x, y, z, skip.reshape(1, c))


def _hyena_branch(hy, p, groups, n_ctx_blocks):
    u, x1, x2 = _hy_pre_call(hy, p['hy_conv_w'], p['hy_conv_b'], n_ctx_blocks)
    outs = []
    for start, bsz, n in groups:
        rows = bsz * n
        plan = _dft_plan(n)
        hmul = _long_conv_setup(n, _hy_filter_call(n, p))
        if plan['stages'] == 1:
            ug, x1g, x2g = (a[start:start + rows] for a in (u, x1, x2))
            y = _long_conv(ug.reshape(bsz, n, HY_WIDTH), hmul[0]).reshape(rows, HY_WIDTH)
            z = _hy_gate_call(x1g, y, ug, p['hy_skip'][0])
            y = _long_conv(z.reshape(bsz, n, HY_WIDTH), hmul[1]).reshape(rows, HY_WIDTH)
            outs.append(_hy_gate_call(x2g, y, z, p['hy_skip'][1]))
        else:
            z = _fused_conv_call(u, x1, p['hy_skip'][0], hmul[0], plan, start, start, bsz, n)
            outs.append(_fused_conv_call(z, x2, p['hy_skip'][1], hmul[1], plan, 0, start, bsz, n))
    return jnp.concatenate(outs, axis=0)


def _merge_body(x_ref, yfw_ref, ybw_ref, bonus_ref, g_ref, yb_ref, gt_ref, g1_ref, sh2_ref, sc2_ref, seg_ref,
                gng_ref, gnb_ref, wpa_ref, wpb_ref, wout_ref, lng_ref, lnb_ref, rwh_ref, rwl_ref, rb_ref,
                x1_ref, h2_ref, ti_ref, tg_ref, *, alpha):
    seg = seg_ref[...]
    y = yfw_ref[...] + ybw_ref[...]
    inv = 1.0 / RW_HEAD
    mu = _dot_x2(y, seg) * inv
    yc = y - mu
    var = _dot_x2(yc * yc, seg) * inv
    yn = yc * lax.rsqrt(var + GN_EPS) * gng_ref[...] + gnb_ref[...]
    y_a = (yn + bonus_ref[...]) * g_ref[...]
    gt = gt_ref[...]
    merged = gt[:, :D_MODEL] * _bdot(y_a, wpa_ref[...]) + gt[:, D_MODEL:] * _bdot(yb_ref[...], wpb_ref[...])
    mix = _bdot(merged, wout_ref[...])
    x1 = _layer_norm(alpha * x_ref[...] + g1_ref[...] * mix, lng_ref[...], lnb_ref[...])
    x1_ref[...] = x1
    h2 = x1 * (1.0 + sc2_ref[...]) + sh2_ref[...]
    _store_token_tiles(h2_ref, 0, h2)
    h_hi, h_lo = _split2(h2)
    d = functools.partial(jnp.dot, preferred_element_type=F32)
    logits = d(h_hi, rwh_ref[...]) + (d(h_hi, rwl_ref[...]) + d(h_lo, rwh_ref[...])) + rb_ref[...]
    lane = lax.broadcasted_iota(jnp.int32, logits.shape, 1)
    neg = jnp.float32(-jnp.inf)
    cur = jnp.where(lane < N_EXPERTS, logits, neg)
    top_i = jnp.zeros(logits.shape, jnp.int32)
    top_e = jnp.zeros(logits.shape, F32)
    den = jnp.zeros((logits.shape[0], 1), F32)
    v0 = None
    for j in range(TOP_K):
        mx = jnp.max(cur, axis=-1, keepdims=True)
        idx = jnp.min(jnp.where(cur == mx, lane, LANES), axis=-1, keepdims=True)
        if j == 0:
            v0 = mx
        e = jnp.exp(mx - v0)
        den = den + e
        top_i = jnp.where(lane == j, idx, top_i)
        top_e = jnp.where(lane == j, e, top_e)
        cur = jnp.where(lane == idx, neg, cur)
    ti_ref[...] = top_i
    tg_ref[...] = top_e / den


def _merge_call(x, yfw, ybw, bonus, g, yb, gates, mod_l, p, seg, cond_idx, alpha):
    t = x.shape[0]
    w = RW_WIDTH
    tok = lambda n: pl.BlockSpec((TOKEN_BLOCK, n), lambda i: (i, 0))
    modspec = lambda j: pl.BlockSpec((None, 1, D_MODEL), lambda i: (cond_idx(i), 0, j))
    full = lambda *s: pl.BlockSpec(s, lambda i: (0,) * len(s))
    rw_pad = jnp.zeros((D_MODEL, LANES), F32).at[:, :N_EXPERTS].set(p['router_w'])
    rw_hi = rw_pad.astype(BF16)
    rw_lo = (rw_pad - rw_hi.astype(F32)).astype(BF16)
    rb = jnp.zeros((1, LANES), F32).at[0, :N_EXPERTS].set(p['router_b'])
    o = jax.ShapeDtypeStruct((t, D_MODEL), F32)
    return pl.pallas_call(
        functools.partial(_merge_body, alpha=alpha),
        out_shape=(o, jax.ShapeDtypeStruct((t * TOKEN_TILE, LANES), F32),
                   jax.ShapeDtypeStruct((t, LANES), jnp.int32), jax.ShapeDtypeStruct((t, LANES), F32)),
        grid=(t // TOKEN_BLOCK,),
        in_specs=[tok(D_MODEL), tok(w), tok(w), tok(w), tok(w), tok(w),
                  tok(2 * D_MODEL), modspec(2), modspec(3), modspec(4), full(w, w), full(1, w), full(1, w),
                  full(w, D_MODEL), full(w, D_MODEL), full(D_MODEL, D_MODEL), full(1, D_MODEL), full(1, D_MODEL),
                  full(D_MODEL, LANES), full(D_MODEL, LANES), full(1, LANES)],
        out_specs=(tok(D_MODEL), pl.BlockSpec((TOKEN_BLOCK * TOKEN_TILE, LANES), lambda i: (i, 0)),
                   tok(LANES), tok(LANES)),
        compiler_params=_cparams(("parallel",), VMEM_LIMIT),
        name="merge_ln_router",
    )(x, yfw, ybw, bonus, g, yb, gates, mod_l, mod_l, mod_l, seg, p['gn_g'].reshape(1, w), p['gn_b'].reshape(1, w),
      p['w_pa'].astype(BF16), p['w_pb'].astype(BF16), p['w_out'].astype(BF16),
      p['ln1_g'].reshape(1, D_MODEL), p['ln1_b'].reshape(1, D_MODEL), rw_hi, rw_lo, rb)


DEINT_COLS = 256
DEINT_GROUPS = 4


def _deint_body(w_ref, p_ref, g_ref, l_ref):
    half = DEINT_COLS // 2
    for q in range(DEINT_GROUPS):
        y = jnp.dot(w_ref[:, q * DEINT_COLS:(q + 1) * DEINT_COLS].astype(BF16), p_ref[...],
                    preferred_element_type=F32)
        g_ref[:, q * half:(q + 1) * half] = y[:, :half].astype(BF16)
        l_ref[:, q * half:(q + 1) * half] = y[:, half:].astype(BF16)


def _deint_call(w, layer):
    _, e, k, n2 = w.shape
    half = DEINT_COLS // 2
    step_cols = DEINT_COLS * DEINT_GROUPS
    sel = np.zeros((DEINT_COLS, DEINT_COLS), np.float32)
    sel[2 * np.arange(half), np.arange(half)] = 1.0
    sel[2 * np.arange(half) + 1, half + np.arange(half)] = 1.0
    o = jax.ShapeDtypeStruct((e, k, n2 // 2), BF16)
    return pl.pallas_call(
        _deint_body,
        out_shape=(o, o),
        grid=(e, n2 // step_cols),
        in_specs=[pl.BlockSpec((None, None, k, step_cols), lambda i, j: (layer, i, 0, j)),
                  pl.BlockSpec((DEINT_COLS, DEINT_COLS), lambda i, j: (0, 0))],
        out_specs=(pl.BlockSpec((None, k, step_cols // 2), lambda i, j: (i, 0, j)),) * 2,
        compiler_params=_cparams(("parallel", "parallel")),
        name="expert_w_split",
    )(w, jnp.asarray(sel, BF16))


def _moe_body(blk_e_ref, n_on_ref, tok_ref, tokn_ref, dst_ref, h_hbm, wg_ref, wl_ref, bg_ref, bl_ref, wdn_ref,
              bdn_ref, y_hbm, xbuf, ybuf, sem_in, sem_out, *, n_real):
    i = pl.program_id(0)
    n_on = n_on_ref[0]
    slot = i % 2
    tt = TOKEN_TILE
    slot_rows = MOE_ROWS * tt

    def tile(ref, idx):
        return ref.at[pl.ds(pl.multiple_of(idx * tt, tt), tt), :]

    def gather_start(tref, s):
        def body(r, c):
            pltpu.make_async_copy(tile(h_hbm, tref[0, r]), tile(xbuf, s * MOE_ROWS + r), sem_in.at[s]).start()
            return c
        lax.fori_loop(0, MOE_ROWS, body, 0, unroll=8)

    def slot_buf(buf, s):
        return buf.at[pl.ds(pl.multiple_of(s * slot_rows, slot_rows), slot_rows), :]

    def gather_wait(s):
        pltpu.make_async_copy(h_hbm.at[pl.ds(0, slot_rows), :], slot_buf(xbuf, s), sem_in.at[s]).wait()

    def scatter_start(s):
        def body(r, c):
            pltpu.make_async_copy(tile(ybuf, s * MOE_ROWS + r), tile(y_hbm, dst_ref[0, r]), sem_out.at[s]).start()
            return c
        lax.fori_loop(0, MOE_ROWS, body, 0, unroll=8)

    def scatter_wait(s):
        pltpu.make_async_copy(slot_buf(ybuf, s), y_hbm.at[pl.ds(0, slot_rows), :], sem_out.at[s]).wait()

    @pl.when(i == 0)
    def _():
        ybuf[...] = jnp.zeros_like(ybuf)
        fills = [pltpu.make_async_copy(slot_buf(ybuf, s),
                                       y_hbm.at[pl.ds((n_real + s * MOE_ROWS) * tt, slot_rows), :],
                                       sem_out.at[s]) for s in range(2)]
        for cp in fills:
            cp.start()
        for cp in fills:
            cp.wait()

    @pl.when(jnp.logical_and(i == 0, n_on > 0))
    def _():
        gather_start(tok_ref, 0)

    @pl.when(i < n_on)
    def _():
        @pl.when(i + 1 < n_on)
        def _():
            gather_start(tokn_ref, 1 - slot)

        gather_wait(slot)
        row0 = slot * slot_rows
        x = _load_token_tiles(xbuf, row0, MOE_ROWS).astype(BF16)
        d = functools.partial(jnp.dot, preferred_element_type=F32)
        glu = jnp.minimum(d(x, wg_ref[...]) + bg_ref[...], SWIGLU_LIMIT)
        lin = jnp.clip(d(x, wl_ref[...]) + bl_ref[...], -SWIGLU_LIMIT, SWIGLU_LIMIT)
        act = glu * _sigmoid(SWIGLU_ALPHA * glu) * (lin + 1.0)
        _store_token_tiles(ybuf, row0, d(act.astype(BF16), wdn_ref[...]) + bdn_ref[...])
        scatter_start(slot)

        @pl.when(i >= 1)
        def _():
            scatter_wait(1 - slot)

        @pl.when(i == n_on - 1)
        def _():
            scatter_wait(slot)


def _moe_call(h2, top_i, wg, wl, bg, bl, wdn, bdn):
    t = h2.shape[0] // TOKEN_TILE
    m = t * TOP_K
    e = N_EXPERTS
    blk = MOE_ROWS
    flat_e = top_i.reshape(-1)
    order = jnp.argsort(flat_e, stable=True).astype(jnp.int32)
    sizes = jnp.bincount(flat_e, length=e).astype(jnp.int32)
    padded = (sizes + blk - 1) // blk * blk
    pad_end = jnp.cumsum(padded)
    pad_start = pad_end - padded
    grp_start = jnp.cumsum(sizes) - sizes
    n_blocks = -(-(m + e * (blk - 1)) // blk)
    blk_first = jnp.arange(n_blocks, dtype=jnp.int32) * blk
    blk_e = jnp.minimum(jnp.sum((pad_end[None, :] <= blk_first[:, None]).astype(jnp.int32), axis=1), e - 1)
    pidx = jnp.arange(n_blocks * blk, dtype=jnp.int32)
    e_p = jnp.repeat(blk_e, blk)
    idx = pidx - pad_start[e_p]
    valid = idx < sizes[e_p]
    assign = order[jnp.clip(grp_start[e_p] + idx, 0, m - 1)]
    tok_row = jnp.where(valid, assign // TOP_K, 0).astype(jnp.int32)
    spare = m + ((pidx // blk) % 2) * blk + pidx % blk
    dst_row = jnp.where(valid, (assign % TOP_K) * t + assign // TOP_K, spare).astype(jnp.int32)
    n_on = (pad_end[-1] // blk).astype(jnp.int32).reshape(1)
    tok3 = tok_row.reshape(n_blocks, 1, blk)

    smem = lambda f: pl.BlockSpec((None, 1, blk), f, memory_space=pltpu.SMEM)
    wspec = lambda a, b: pl.BlockSpec((None, a, b), lambda i, be, no: (be[i], 0, 0))
    gs = pltpu.PrefetchScalarGridSpec(
        num_scalar_prefetch=2,
        grid=(n_blocks,),
        in_specs=[smem(lambda i, be, no: (i, 0, 0)),
                  smem(lambda i, be, no: (jnp.minimum(i + 1, n_blocks - 1), 0, 0)),
                  smem(lambda i, be, no: (i, 0, 0)),
                  pl.BlockSpec(memory_space=pl.ANY),
                  wspec(D_MODEL, D_FF), wspec(D_MODEL, D_FF), wspec(1, D_FF), wspec(1, D_FF),
                  wspec(D_FF, D_MODEL), wspec(1, D_MODEL)],
        out_specs=pl.BlockSpec(memory_space=pl.ANY),
        scratch_shapes=[pltpu.VMEM((2 * blk * TOKEN_TILE, LANES), F32), pltpu.VMEM((2 * blk * TOKEN_TILE, LANES), F32),
                        pltpu.SemaphoreType.DMA((2,)), pltpu.SemaphoreType.DMA((2,))],
    )
    return pl.pallas_call(
        functools.partial(_moe_body, n_real=m),
        out_shape=jax.ShapeDtypeStruct(((m + 2 * blk) * TOKEN_TILE, LANES), F32),
        grid_spec=gs,
        compiler_params=_cparams(("arbitrary",), VMEM_LIMIT),
        name="moe_experts",
    )(blk_e, n_on, tok3, tok3, dst_row.reshape(n_blocks, 1, blk), h2, wg, wl, bg.reshape(e, 1, D_FF),
      bl.reshape(e, 1, D_FF), wdn, bdn.reshape(e, 1, D_MODEL))


def _combine_body(x1_ref, y0_ref, y1_ref, y2_ref, y3_ref, tg_ref, g2_ref, lng_ref, lnb_ref, o_ref, *, alpha):
    tg = tg_ref[...]
    rows = x1_ref.shape[0]
    moe = tg[:, 0:1] * _load_token_tiles(y0_ref, 0, rows)
    for j, y_ref in enumerate((y1_ref, y2_ref, y3_ref), start=1):
        moe = moe + tg[:, j:j + 1] * _load_token_tiles(y_ref, 0, rows)
    o_ref[...] = _layer_norm(alpha * x1_ref[...] + g2_ref[...] * moe, lng_ref[...], lnb_ref[...])


def _combine_call(x1, yexp, tg, mod_l, p, cond_idx, alpha):
    t = x1.shape[0]
    nb = t // TOKEN_BLOCK
    tok = lambda n: pl.BlockSpec((TOKEN_BLOCK, n), lambda i: (i, 0))
    full = lambda *s: pl.BlockSpec(s, lambda i: (0,) * len(s))
    yspec = [pl.BlockSpec((TOKEN_BLOCK * TOKEN_TILE, LANES), lambda i, j=j: (j * nb + i, 0)) for j in range(TOP_K)]
    return pl.pallas_call(
        functools.partial(_combine_body, alpha=alpha),
        out_shape=jax.ShapeDtypeStruct((t, D_MODEL), F32),
        grid=(nb,),
        in_specs=[tok(D_MODEL)] + yspec + [tok(LANES),
                  pl.BlockSpec((None, 1, D_MODEL), lambda i: (cond_idx(i), 0, 5)),
                  full(1, D_MODEL), full(1, D_MODEL)],
        out_specs=tok(D_MODEL),
        compiler_params=_cparams(("parallel",), VMEM_LIMIT),
        name="moe_combine_ln",
    )(x1, yexp, yexp, yexp, yexp, tg, mod_l, p['ln2_g'].reshape(1, D_MODEL), p['ln2_b'].reshape(1, D_MODEL))


def kernel(x_prompt, x_sample, c, state_rwkv, c_ctx, w_mod, b_mod, w_in, mu_shift, w0, w_lora_up, a0, a_lora_up, g_up, k_k, k_a, r_k, gn_g, gn_b, hy_conv_w, hy_conv_b, hy_f_w1, hy_f_b1, hy_f_w2, hy_f_b2, hy_f_freq, hy_f_w3, hy_skip, w_pa, w_pb, w_out, ln1_g, ln1_b, ln2_g, ln2_b, router_w, router_b, ex_w_up, ex_b_up, ex_w_down, ex_b_down):
    bsz, seq, dm = x_prompt.shape
    dbsz, dseq, _ = x_sample.shape
    depth = w_mod.shape[0]
    assert dm == D_MODEL and seq == TOKEN_BLOCK and dseq % TOKEN_BLOCK == 0 and TOKEN_BLOCK % GRID_W == 0
    assert 1 + dbsz <= 8
    alpha = (2 * depth) ** 0.25
    t_ctx = bsz * seq
    n_ctx_blocks = t_ctx // TOKEN_BLOCK
    lat_blocks = dseq // TOKEN_BLOCK

    def cond_idx(i):
        return jnp.where(i < n_ctx_blocks, 0, 1 + (i - n_ctx_blocks) // lat_blocks)

    x = jnp.concatenate([x_prompt.reshape(t_ctx, dm), x_sample.reshape(dbsz * dseq, dm)], axis=0)
    cond8 = jnp.zeros((8, dm), F32).at[0].set(c_ctx).at[1:1 + dbsz].set(c)
    mod = _mod_call(cond8, w_mod, b_mod)

    hd = RW_HEAD
    seg = (np.arange(RW_WIDTH)[:, None] // hd == np.arange(RW_WIDTH)[None, :] // hd)
    seg = jnp.asarray(seg, BF16)
    seq_lens = [seq] * bsz + [dseq] * dbsz
    tbl_np, nsteps = _scan_table(seq_lens)
    tbl = jnp.asarray(tbl_np)
    groups = [(0, bsz, seq), (t_ctx, dbsz, dseq)]

    new_states = []
    for l in range(depth):
        p = dict(mu_shift=mu_shift[l], w0=w0[l], w_lora_up=w_lora_up[l], a0=a0[l], a_lora_up=a_lora_up[l],
                 g_up=g_up[l], k_k=k_k[l], k_a=k_a[l], r_k=r_k[l], gn_g=gn_g[l], gn_b=gn_b[l],
                 hy_conv_w=hy_conv_w[l], hy_conv_b=hy_conv_b[l], hy_f_w1=hy_f_w1[l], hy_f_b1=hy_f_b1[l],
                 hy_f_w2=hy_f_w2[l], hy_f_b2=hy_f_b2[l], hy_f_freq=hy_f_freq[l], hy_f_w3=hy_f_w3[l],
                 hy_skip=hy_skip[l], w_pa=w_pa[l], w_pb=w_pb[l], w_out=w_out[l], ln1_g=ln1_g[l], ln1_b=ln1_b[l],
                 ln2_g=ln2_g[l], ln2_b=ln2_b[l], router_w=router_w[l], router_b=router_b[l])
        mod_l = mod[l].reshape(8, 1, 6 * dm)
        rw, hy, gates = _inproj_call(x, mod_l, w_in[l].astype(BF16), cond_idx)
        r, kk, v, lw, b, kd, g, bonus = _rwkv_pre_call(rw, p, seg, n_ctx_blocks)
        s0 = jnp.concatenate([jnp.zeros((bsz, 2, RW_HEADS, hd, hd), F32), state_rwkv[:, l].astype(F32)], axis=0)
        yfw, ybw, sfin = _scan_call(tbl, nsteps, bsz + dbsz, r, kk, v, lw, b, kd, s0)
        new_states.append(sfin[:bsz].astype(x_prompt.dtype))
        yb = _hyena_branch(hy, p, groups, n_ctx_blocks)
        x1, h2, top_i, top_g = _merge_call(x, yfw, ybw, bonus, g, yb, gates, mod_l, p, seg, cond_idx, alpha)
        wg, wl = _deint_call(ex_w_up, l)
        yexp = _moe_call(h2, top_i[:, :TOP_K], wg, wl, ex_b_up[l][:, 0::2], ex_b_up[l][:, 1::2],
                         ex_w_down[l].astype(BF16), ex_b_down[l])
        x = _combine_call(x1, yexp, top_g, mod_l, p, cond_idx, alpha)

    y_p = x[:t_ctx].reshape(bsz, seq, dm)
    y_s = x[t_ctx:].reshape(dbsz, dseq, dm)
    return (y_p, y_s, jnp.stack(new_states, axis=1))
```

```python
import functools
import math

import numpy as np
import jax
import jax.numpy as jnp
from jax import lax
from jax.experimental import pallas as pl
from jax.experimental.pallas import tpu as pltpu

F32 = jnp.float32
BF16 = jnp.bfloat16

D_MODEL = 1024
GRID_W = 64
RW_WIDTH = 512
RW_HEAD = 64
RW_HEADS = RW_WIDTH // RW_HEAD
LORA_W = 64
LORA_A = 64
LORA_G = 128
GN_EPS = 64e-5
HY_WIDTH = 512
HY_ORDER = 2
HY_BANDS = 16
HY_EMB = 2 * HY_BANDS + 1
HY_FFN = 64
HY_DECAY_TARGET = 1e-2
HY_DECAY_SHORT_PCT = 0.3
HY_DECAY_LONG_PCT = 1.5
RW_COLS = 3 * RW_WIDTH + LORA_W + LORA_A + LORA_G
HY_COLS = (HY_ORDER + 1) * HY_WIDTH
IN_COLS = RW_COLS + HY_COLS + 2 * D_MODEL
N_EXPERTS = 32
TOP_K = 4
D_FF = D_MODEL
SWIGLU_ALPHA = 1.702
SWIGLU_LIMIT = 7.0
LN_EPS = 1e-5

TOKEN_BLOCK = 256
SCAN_CHUNK = 64
INV_BLOCK = 16
DFT_N1 = 128
MOE_ROWS = 256
DMA_GROUP = 8
LANES = 128
VMEM_LIMIT = 56 * 1024 * 1024


def _cparams(sem, vmem=None):
    return pltpu.CompilerParams(dimension_semantics=sem, vmem_limit_bytes=vmem)


def _bdot(a, b):
    return jnp.dot(a.astype(BF16), b.astype(BF16), preferred_element_type=F32)


def _bdg(a, b, ca, cb):
    return lax.dot_general(a.astype(BF16), b.astype(BF16), (((ca,), (cb,)), ((0,), (0,))), preferred_element_type=F32)


def _bmm(a, b):
    return _bdg(a, b, 2, 1)


def _bmm_nt(a, b):
    return _bdg(a, b, 2, 2)


def _bmm_tn(a, b):
    return _bdg(a, b, 1, 1)


def _split2(x):
    hi = x.astype(BF16)
    lo = (x - hi.astype(F32)).astype(BF16)
    return hi, lo


def _split3(x):
    hi = x.astype(BF16)
    r1 = x - hi.astype(F32)
    mid = r1.astype(BF16)
    lo = (r1 - mid.astype(F32)).astype(BF16)
    return hi, mid, lo


def _dot_x2(x, g_bf16):
    hi, lo = _split2(x)
    d = functools.partial(jnp.dot, preferred_element_type=F32)
    return d(hi, g_bf16) + d(lo, g_bf16)


def _dot3(x, g_hi, g_lo):
    x_hi, x_lo = _split2(x)
    d = functools.partial(jnp.dot, preferred_element_type=F32)
    return d(x_hi, g_hi) + (d(x_hi, g_lo) + d(x_lo, g_hi))


def _sigmoid(x):
    return 1.0 / (1.0 + jnp.exp(-x))


TOKEN_TILE = D_MODEL // LANES


def _store_token_tiles(ref, row0, x):
    rows = x.shape[0]
    for s in range(TOKEN_TILE):
        ref[pl.ds(row0 + s, rows, stride=TOKEN_TILE), :] = x[:, s * LANES:(s + 1) * LANES]


def _load_token_tiles(ref, row0, rows):
    return jnp.concatenate([ref[pl.ds(row0 + s, rows, stride=TOKEN_TILE), :] for s in range(TOKEN_TILE)], axis=1)


def _layer_norm(x, g, b):
    mu = jnp.mean(x, axis=-1, keepdims=True)
    xc = x - mu
    var = jnp.mean(xc * xc, axis=-1, keepdims=True)
    return xc * lax.rsqrt(var + LN_EPS) * g + b


def _mod_body(c_ref, w_ref, b_ref, o_ref):
    c = c_ref[...]
    o_ref[...] = _bdot(c * _sigmoid(c), w_ref[...]) + b_ref[...]


def _mod_call(cond8, w_mod, b_mod):
    depth = w_mod.shape[0]
    tn = 1536
    return pl.pallas_call(
        _mod_body,
        out_shape=jax.ShapeDtypeStruct((depth, 8, 6 * D_MODEL), F32),
        grid=(depth, 6 * D_MODEL // tn),
        in_specs=[pl.BlockSpec((8, D_MODEL), lambda l, j: (0, 0)),
                  pl.BlockSpec((None, D_MODEL, tn), lambda l, j: (l, 0, j)),
                  pl.BlockSpec((None, 1, tn), lambda l, j: (l, 0, j))],
        out_specs=pl.BlockSpec((None, 8, tn), lambda l, j: (l, 0, j)),
        compiler_params=_cparams(("parallel", "parallel"), VMEM_LIMIT),
        name="adaln_mod",
    )(cond8, w_mod, b_mod.reshape(depth, 1, 6 * D_MODEL))


def _inproj_body(x_ref, sh_ref, sc_ref, w_ref, rw_ref, hy_ref, gt_ref):
    h = (x_ref[...] * (1.0 + sc_ref[...]) + sh_ref[...]).astype(BF16)
    d = functools.partial(jnp.dot, preferred_element_type=F32)
    rw_ref[...] = d(h, w_ref[:, :RW_COLS])
    hy_ref[...] = d(h, w_ref[:, RW_COLS:RW_COLS + HY_COLS])
    gt_ref[...] = _sigmoid(d(h, w_ref[:, RW_COLS + HY_COLS:]))


def _inproj_call(x, mod_l, w_in_bf16, cond_idx):
    t = x.shape[0]
    nb = t // TOKEN_BLOCK
    tok = lambda n: pl.BlockSpec((TOKEN_BLOCK, n), lambda i: (i, 0))
    modspec = lambda j: pl.BlockSpec((None, 1, D_MODEL), lambda i: (cond_idx(i), 0, j))
    return pl.pallas_call(
        _inproj_body,
        out_shape=(jax.ShapeDtypeStruct((t, RW_COLS), F32), jax.ShapeDtypeStruct((t, HY_COLS), F32),
                   jax.ShapeDtypeStruct((t, 2 * D_MODEL), F32)),
        grid=(nb,),
        in_specs=[tok(D_MODEL), modspec(0), modspec(1),
                  pl.BlockSpec((D_MODEL, IN_COLS), lambda i: (0, 0))],
        out_specs=(tok(RW_COLS), tok(HY_COLS), tok(2 * D_MODEL)),
        compiler_params=_cparams(("parallel",), VMEM_LIMIT),
        name="in_proj",
    )(x, mod_l, mod_l, w_in_bf16)


def _neighbours(x, n_ctx_blocks):
    rows = x.shape[0]
    row = lax.broadcasted_iota(jnp.int32, (rows, 1), 0)
    seg_mask = jnp.where(pl.program_id(0) < n_ctx_blocks, rows - 1, GRID_W - 1)
    pos = row & seg_mask
    prev = jnp.where(pos == 0, 0.0, pltpu.roll(x, 1, 0))
    nxt = jnp.where(pos == seg_mask, 0.0, pltpu.roll(x, rows - 1, 0))
    return prev, nxt


def _rwkv_pre_body(rw_ref, mu_ref, kk_s_ref, ka_ref, w0_ref, a0_ref, wl_ref, al_ref, gup_ref, rk_ref, seg_ref,
                   r_ref, kk_ref, v_ref, lw_ref, b_ref, kd_ref, g_ref, bonus_ref, *, n_ctx_blocks):
    x = rw_ref[...]
    prev, nxt = _neighbours(x, n_ctx_blocks)
    cols = x + mu_ref[...] * (0.5 * (prev + nxt) - x)
    w = RW_WIDTH
    r = cols[:, :w]
    k = cols[:, w:2 * w]
    v = cols[:, 2 * w:3 * w]
    wd = cols[:, 3 * w:3 * w + LORA_W]
    ad = cols[:, 3 * w + LORA_W:3 * w + LORA_W + LORA_A]
    gd = cols[:, 3 * w + LORA_W + LORA_A:]
    seg = seg_ref[...]
    kkr = k * kk_s_ref[...]
    ss = _dot_x2(kkr * kkr, seg)
    kk = kkr / jnp.maximum(jnp.sqrt(ss), 1e-12)
    r_ref[...] = r
    kk_ref[...] = kk
    v_ref[...] = v
    g_ref[...] = _bdot(_sigmoid(gd), gup_ref[...])
    tw = jnp.tanh(wd)
    bonus = jnp.zeros_like(r)
    for d in range(2):
        w_logit = w0_ref[d:d + 1, :] + _bdot(tw, wl_ref[d])
        lw_ref[d] = -math.exp(-0.5) * _sigmoid(w_logit)
        a = _sigmoid(a0_ref[d:d + 1, :] + _bdot(ad, al_ref[d]))
        kd = k * (1.0 + (a - 1.0) * ka_ref[...])
        kd_ref[d] = kd
        b_ref[d] = kk * a
        bonus = bonus + _dot_x2(r * kd * rk_ref[d:d + 1, :], seg) * v
    bonus_ref[...] = bonus


def _rwkv_pre_call(rw, p, seg, n_ctx_blocks):
    t = rw.shape[0]
    nb = t // TOKEN_BLOCK
    w = RW_WIDTH
    tok = lambda n: pl.BlockSpec((TOKEN_BLOCK, n), lambda i: (i, 0))
    tok2 = pl.BlockSpec((2, TOKEN_BLOCK, w), lambda i: (0, i, 0))
    full = lambda *s: pl.BlockSpec(s, lambda i: (0,) * len(s))
    o1 = jax.ShapeDtypeStruct((t, w), F32)
    o2 = jax.ShapeDtypeStruct((2, t, w), F32)
    return pl.pallas_call(
        functools.partial(_rwkv_pre_body, n_ctx_blocks=n_ctx_blocks),
        out_shape=(o1, o1, o1, o2, o2, o2, o1, o1),
        grid=(nb,),
        in_specs=[tok(RW_COLS), full(1, RW_COLS), full(1, w), full(1, w), full(2, w), full(2, w),
                  full(2, LORA_W, w), full(2, LORA_A, w), full(LORA_G, w), full(2, w), full(w, w)],
        out_specs=(tok(w), tok(w), tok(w), tok2, tok2, tok2, tok(w), tok(w)),
        compiler_params=_cparams(("parallel",), VMEM_LIMIT),
        name="rwkv_pre",
    )(rw, p['mu_shift'].reshape(1, RW_COLS), p['k_k'].reshape(1, w), p['k_a'].reshape(1, w), p['w0'], p['a0'],
      p['w_lora_up'], p['a_lora_up'], p['g_up'], p['r_k'].reshape(2, w), seg)


def _scan_body(tbl_ref, rf_ref, kkf_ref, vf_ref, rb_ref, kkb_ref, vb_ref, lwf_ref, bf_ref, kdf_ref, lwb_ref,
               bb_ref, kdb_ref, s0_ref, yf_ref, yb_ref, sfin_ref, s_ref):
    base = pl.program_id(0) * 5
    c = SCAN_CHUNK
    hd = RW_HEAD
    nh = 2 * RW_HEADS

    @pl.when(tbl_ref[base + 3] == 1)
    def _():
        s_ref[...] = s0_ref[...].reshape(nh, hd, hd)

    rowi = lax.broadcasted_iota(jnp.int32, (c, c), 0)
    coli = lax.broadcasted_iota(jnp.int32, (c, c), 1)
    same_blk = (rowi // INV_BLOCK) == (coli // INV_BLOCK)
    eye = (rowi == coli).astype(F32)

    def both(fwd, bwd):
        return jnp.concatenate([jnp.broadcast_to(fwd, (RW_HEADS, c, c)), jnp.broadcast_to(bwd, (RW_HEADS, c, c))],
                               axis=0)

    diff = both(rowi - coli, coli - rowi)
    strict = diff > 0
    incl = diff >= 0

    def heads(a):
        return jnp.stack([a[:, h * hd:(h + 1) * hd] for h in range(RW_HEADS)], axis=0)

    def prep(r_ref, kk_ref, v_ref, lw_ref, b_ref, kd_ref, tri):
        lw = lw_ref[...]
        hi, mid, lo = _split3(lw)
        dd = functools.partial(jnp.dot, preferred_element_type=F32)
        cum = dd(tri, hi) + (dd(tri, mid) + dd(tri, lo))
        tot = jnp.sum(lw, axis=0, keepdims=True)
        e_neg = jnp.exp(-cum)
        e_rem = jnp.exp(tot - cum)
        kk = kk_ref[...]
        bb = b_ref[...]
        kd = kd_ref[...]
        return [heads(a) for a in (kk * jnp.exp(cum - lw), bb * e_neg, kd * e_neg, r_ref[...] * jnp.exp(cum),
                                   v_ref[...], kd * e_rem, bb * e_rem, jnp.exp(tot))]

    fw = prep(rf_ref, kkf_ref, vf_ref, lwf_ref, bf_ref, kdf_ref, (rowi >= coli).astype(BF16))
    bw = prep(rb_ref, kkb_ref, vb_ref, lwb_ref, bb_ref, kdb_ref, (rowi <= coli).astype(BF16))
    a_h, b_h, k_h, r_h, v_h, kapg_h, betg_h, gtot_h = [jnp.concatenate([f, b], axis=0) for f, b in zip(fw, bw)]
    ar = jnp.concatenate([a_h, r_h], axis=1)
    gb = _bmm_nt(ar, b_h)
    gk = _bmm_nt(ar, k_h)
    low = jnp.where(strict, gb[:, :c], 0.0)
    a_ka = jnp.where(strict, gk[:, :c], 0.0)
    a_br = jnp.where(incl, gb[:, c:], 0.0)
    a_kr = jnp.where(incl, gk[:, c:], 0.0)
    nd = jnp.where(same_blk, -low, 0.0)
    loff = jnp.where(same_blk, 0.0, low)
    x = eye + nd
    n2 = _bmm(nd, nd)
    x = x + _bmm(x, n2)
    n4 = _bmm(n2, n2)
    x = x + _bmm(x, n4)
    n8 = _bmm(n4, n4)
    x = x + _bmm(x, n8)
    m = _bmm(x, loff)
    m2 = _bmm(m, m)
    y1 = x + _bmm(m2, x)
    tinv = y1 - _bmm(m, y1)
    w_h = _bmm(a_ka, v_h)
    rhs = jnp.concatenate([a_h, w_h], axis=2)
    x0 = _bmm(tinv, rhs)
    res = rhs - x0 - _bmm(low, x0)
    xs = x0 + _bmm(tinv, res)
    p_h = xs[:, :, :hd]
    q_h = xs[:, :, hd:]
    s_old = s_ref[...]
    uy = _bmm_nt(jnp.concatenate([p_h, r_h], axis=1), s_old)
    u_h = uy[:, :c] + q_h
    y_h = uy[:, c:] + _bmm(a_kr, v_h) - _bmm(a_br, u_h)
    zv = jnp.concatenate([v_h, u_h], axis=1)
    zk = jnp.concatenate([kapg_h, -betg_h], axis=1)
    s_ref[...] = s_old * gtot_h + _bmm_tn(zv, zk)
    for h in range(RW_HEADS):
        yf_ref[:, h * hd:(h + 1) * hd] = y_h[h]
        yb_ref[:, h * hd:(h + 1) * hd] = y_h[RW_HEADS + h]

    @pl.when(tbl_ref[base + 4] == 1)
    def _():
        sfin_ref[...] = s_ref[...].reshape(2, RW_HEADS, hd, hd)


def _scan_call(tbl, nsteps, nseq, r, kk, v, lw, b, kd, s0):
    t = r.shape[0]
    w = RW_WIDTH
    c = SCAN_CHUNK
    blk = lambda d: pl.BlockSpec((c, w), lambda i, tb: (tb[i * 5 + d], 0))
    blk_dir = lambda d: pl.BlockSpec((None, c, w), lambda i, tb: (d, tb[i * 5 + d], 0))
    st = pl.BlockSpec((None, 2, RW_HEADS, RW_HEAD, RW_HEAD), lambda i, tb: (tb[i * 5 + 2], 0, 0, 0, 0))
    gs = pltpu.PrefetchScalarGridSpec(
        num_scalar_prefetch=1,
        grid=(nsteps,),
        in_specs=[blk(0), blk(0), blk(0), blk(1), blk(1), blk(1), blk_dir(0), blk_dir(0), blk_dir(0),
                  blk_dir(1), blk_dir(1), blk_dir(1), st],
        out_specs=(blk(0), blk(1), st),
        scratch_shapes=[pltpu.VMEM((2 * RW_HEADS, RW_HEAD, RW_HEAD), F32)],
    )
    o = jax.ShapeDtypeStruct((t, w), F32)
    return pl.pallas_call(
        _scan_body,
        out_shape=(o, o, jax.ShapeDtypeStruct((nseq, 2, RW_HEADS, RW_HEAD, RW_HEAD), F32)),
        grid_spec=gs,
        compiler_params=_cparams(("arbitrary",), VMEM_LIMIT),
        name="rwkv_scan",
    )(tbl, r, kk, v, r, kk, v, lw, b, kd, lw, b, kd, s0)


def _scan_table(seq_lens):
    c = SCAN_CHUNK
    rows = []
    start = 0
    for s, n in enumerate(seq_lens):
        nc = n // c
        for j in range(nc):
            rows.append((start + j, start + nc - 1 - j, s, int(j == 0), int(j == nc - 1)))
        start += nc
    return np.asarray(rows, np.int32).reshape(-1), len(rows)


def _hy_pre_body(hy_ref, w_ref, b_ref, u_ref, x1_ref, x2_ref, *, n_ctx_blocks):
    x = hy_ref[...]
    prev, nxt = _neighbours(x, n_ctx_blocks)
    cols = prev * w_ref[0:1, :] + x * w_ref[1:2, :] + nxt * w_ref[2:3, :] + b_ref[...]
    w = HY_WIDTH
    u_ref[...] = cols[:, :w]
    x1_ref[...] = cols[:, w:2 * w]
    x2_ref[...] = cols[:, 2 * w:]


def _hy_pre_call(hy, conv_w, conv_b, n_ctx_blocks):
    t = hy.shape[0]
    tok = lambda n: pl.BlockSpec((TOKEN_BLOCK, n), lambda i: (i, 0))
    o = jax.ShapeDtypeStruct((t, HY_WIDTH), F32)
    return pl.pallas_call(
        functools.partial(_hy_pre_body, n_ctx_blocks=n_ctx_blocks),
        out_shape=(o, o, o),
        grid=(t // TOKEN_BLOCK,),
        in_specs=[tok(HY_COLS), pl.BlockSpec((3, HY_COLS), lambda i: (0, 0)),
                  pl.BlockSpec((1, HY_COLS), lambda i: (0, 0))],
        out_specs=(tok(HY_WIDTH),) * 3,
        compiler_params=_cparams(("parallel",)),
        name="hyena_pre",
    )(hy, conv_w, conv_b.reshape(1, HY_COLS))


def _dot_f32(a, b):
    a_hi, a_lo = _split2(a)
    b_hi, b_lo = _split2(b)
    d = functools.partial(jnp.dot, preferred_element_type=F32)
    return d(a_hi, b_hi) + (d(a_hi, b_lo) + d(a_lo, b_hi))


def _hy_filter_body(bands_ref, w1t_ref, w1c_ref, w1s_ref, b1_ref, w2_ref, b2_ref, fr_ref, w3_ref, dl_ref, f_ref,
                    *, n, tile):
    row = lax.broadcasted_iota(jnp.int32, (tile, LANES), 0) + pl.program_id(0) * tile
    fwd = row < n
    pos = jnp.where(fwd, row, 2 * n - row).astype(F32)
    tcol = pos / n
    ang = (2.0 * math.pi / n) * pos * bands_ref[...]
    pre1 = (tcol[:, :HY_FFN] * w1t_ref[...] + _dot_f32(jnp.cos(ang), w1c_ref[...])
            + _dot_f32(jnp.sin(ang), w1s_ref[...]) + b1_ref[...])
    fr = fr_ref[...]
    h1 = jnp.sin(fr * pre1)
    h2 = jnp.sin(fr * (_dot_f32(h1, w2_ref[...]) + b2_ref[...]))
    h = _dot_f32(h2, w3_ref[...])
    keep = row != n
    for q in range(HY_WIDTH // LANES):
        win = jnp.where(keep, jnp.exp(-tcol * dl_ref[:, q * LANES:(q + 1) * LANES]), 0.0)
        for o in range(HY_ORDER):
            lo = o * 2 * HY_WIDTH + q * LANES
            f_ref[o, :, q * LANES:(q + 1) * LANES] = jnp.where(
                fwd, h[:, lo:lo + LANES], h[:, lo + HY_WIDTH:lo + HY_WIDTH + LANES]) * win


def _hy_filter_call(n, p):
    bands = jnp.zeros((1, LANES), F32).at[0, :HY_BANDS].set(
        jnp.linspace(1e-4, HY_BANDS - 1, HY_BANDS, dtype=F32))
    max_decay = math.log(HY_DECAY_TARGET) / HY_DECAY_SHORT_PCT
    min_decay = math.log(HY_DECAY_TARGET) / HY_DECAY_LONG_PCT
    deltas = jnp.abs(jnp.linspace(min_decay, max_decay, HY_WIDTH, dtype=F32)).reshape(1, HY_WIDTH)
    w1 = p['hy_f_w1']
    w1t = w1[0:1]
    w1c = jnp.zeros((LANES, HY_FFN), F32).at[:HY_BANDS].set(w1[1:1 + HY_BANDS])
    w1s = jnp.zeros((LANES, HY_FFN), F32).at[:HY_BANDS].set(w1[1 + HY_BANDS:HY_EMB])
    nout = HY_ORDER * 2 * HY_WIDTH
    tile = min(2 * n, 512)
    full = lambda *s: pl.BlockSpec(s, lambda i: (0,) * len(s))
    return pl.pallas_call(
        functools.partial(_hy_filter_body, n=n, tile=tile),
        out_shape=jax.ShapeDtypeStruct((HY_ORDER, 2 * n, HY_WIDTH), F32),
        grid=(2 * n // tile,),
        in_specs=[full(1, LANES), full(1, HY_FFN), full(LANES, HY_FFN), full(LANES, HY_FFN), full(1, HY_FFN),
                  full(HY_FFN, HY_FFN), full(1, HY_FFN), full(1, HY_FFN), full(HY_FFN, nout), full(1, HY_WIDTH)],
        out_specs=pl.BlockSpec((HY_ORDER, tile, HY_WIDTH), lambda i: (0, i, 0)),
        compiler_params=_cparams(("parallel",), VMEM_LIMIT),
        name="hyena_filter",
    )(bands, w1t, w1c, w1s, p['hy_f_b1'].reshape(1, HY_FFN), p['hy_f_w2'], p['hy_f_b2'].reshape(1, HY_FFN),
      p['hy_f_freq'].reshape(1, HY_FFN), p['hy_f_w3'], deltas)


def _dft_body(*refs, pre, post, two, half_in, half_mid):
    it = iter(refs)
    x = next(it)[...]
    if pre:
        m1 = next(it)[...]
        m2 = next(it)[...]
        x = x * m1 + pltpu.roll(x, half_in, 1) * m2
    g1h = next(it)[...]
    g1l = next(it)[...]
    y = _dot3(x, g1h, g1l)
    if post:
        m1 = next(it)[...]
        m2 = next(it)[...]
        y = y * m1 + pltpu.roll(y, half_mid, 1) * m2
    if two:
        g2h = next(it)[...]
        g2l = next(it)[...]
        y = _dot3(y, g2h, g2l)
    o_ref = next(it)
    o_ref[...] = y


def _dft_call(x, g1, tile, pre=None, post=None, g2=None, name="hyena_dft"):
    bsz, rows, k = x.shape
    args = [x]
    specs = [pl.BlockSpec((None, tile, k), lambda b, j: (b, j, 0))]

    def add_mul(mm):
        for a in mm:
            nblk = a.shape[0] // tile
            specs.append(pl.BlockSpec((tile, a.shape[1]), lambda b, j, nblk=nblk: (j % nblk, 0)))
            args.append(a)

    def add_mat(g):
        for a in g:
            specs.append(pl.BlockSpec(a.shape, lambda b, j: (0, 0)))
            args.append(a)

    if pre is not None:
        add_mul(pre)
    add_mat(g1)
    if post is not None:
        add_mul(post)
    if g2 is not None:
        add_mat(g2)
    mid = g1[0].shape[1]
    nout = g2[0].shape[1] if g2 is not None else mid
    body = functools.partial(_dft_body, pre=pre is not None, post=post is not None, two=g2 is not None,
                             half_in=k // 2, half_mid=mid // 2)
    return pl.pallas_call(
        body,
        out_shape=jax.ShapeDtypeStruct((bsz, rows, nout), F32),
        grid=(bsz, rows // tile),
        in_specs=specs,
        out_specs=pl.BlockSpec((None, tile, nout), lambda b, j: (b, j, 0)),
        compiler_params=_cparams(("parallel", "parallel"), VMEM_LIMIT),
        name=name,
    )(*args)


def _hilo(a):
    a = np.asarray(a, np.float64)
    hi = jnp.asarray(a, F32).astype(BF16)
    lo = (jnp.asarray(a, F32) - hi.astype(F32)).astype(BF16)
    return hi, lo


@functools.lru_cache(maxsize=None)
def _dft_plan(n):
    big = 2 * n
    if big <= 512:
        t = np.arange(n)[:, None]
        k = np.arange(big)[None, :]
        fwd = np.exp(-2j * np.pi * t * k / big)
        fwd_full = np.exp(-2j * np.pi * np.arange(big)[:, None] * k / big)
        inv = np.exp(2j * np.pi * np.arange(big)[:, None] * np.arange(n)[None, :] / big) / big
        return dict(stages=1,
                    fwd=np.concatenate([fwd.real, fwd.imag], axis=1),
                    fwd_full=np.concatenate([fwd_full.real, fwd_full.imag], axis=1),
                    inv=np.concatenate([inv.real, -inv.imag], axis=0))
    n1, n2 = DFT_N1, big // DFT_N1
    k1 = np.arange(n1)[None, :, None]
    t = n2 * np.arange(n1)[None, None, :] + np.arange(n2)[:, None, None]
    ga = np.exp(-2j * np.pi * k1 * t / big)
    ga = np.concatenate([ga.real, ga.imag], axis=1)
    fb = np.exp(-2j * np.pi * np.arange(n2)[:, None] * np.arange(n2)[None, :] / n2)
    fbc = np.block([[fb.real, -fb.imag], [fb.imag, fb.real]])
    fbi = np.conj(fb) / big
    fbic = np.block([[fbi.real, -fbi.imag], [fbi.imag, fbi.real]])
    return dict(stages=2, n1=n1, n2=n2, ga=ga, fb=fbc, fb_inv=fbic)


def _long_conv_setup(n, f):
    c = HY_WIDTH
    plan = _dft_plan(n)
    if plan['stages'] == 1:
        xf = jnp.transpose(f, (0, 2, 1))
        spec = _dft_call(xf, _hilo(plan['fwd_full']), tile=min(c, 256), name="hyena_filter_dft")
        fr, fi = spec[..., :2 * n], spec[..., 2 * n:]
        return [(jnp.concatenate([fr[o], fr[o]], axis=1), jnp.concatenate([-fi[o], fi[o]], axis=1))
                for o in range(HY_ORDER)]
    return _fused_filter_call(f, plan)


def _long_conv(z, hmul):
    bsz, n, c = z.shape
    plan = _dft_plan(n)
    x = jnp.transpose(z, (0, 2, 1))
    y = _dft_call(x, _hilo(plan['fwd']), tile=min(c, 256), post=hmul, g2=_hilo(plan['inv']),
                  name="hyena_conv_short")
    return jnp.transpose(y, (0, 2, 1))


def _plane_pitch(n2):
    return n2 if (n2 // 8) % 2 == 1 else n2 + 8


def _dot3c(g_hi, g_lo, x):
    x_hi, x_lo = _split2(x)
    d = functools.partial(jnp.dot, preferred_element_type=F32)
    return d(g_hi, x_hi) + (d(g_hi, x_lo) + d(g_lo, x_hi))


def _gdot(g_hi, g_lo, x):
    if g_lo is None:
        return jnp.dot(g_hi, x.astype(BF16), preferred_element_type=F32)
    return _dot3c(g_hi, g_lo, x)


def _stage_one(x_ref, g_hi_ref, g_lo_ref, asc_ref, *, n2, nt1, n1, pitch):
    def body(t2, c):
        xs = x_ref[pl.ds(t2, nt1, stride=n2), :]
        g_lo = None if g_lo_ref is None else g_lo_ref[t2]
        asc_ref[pl.ds(t2, 2 * n1, stride=pitch), :] = _gdot(g_hi_ref[t2], g_lo, xs)
        return c
    lax.fori_loop(0, n2, body, 0, unroll=4)


def _plane(asc_ref, k1, n1, n2, pitch):
    o_re = pl.multiple_of(k1 * pitch, 8)
    o_im = pl.multiple_of((n1 + k1) * pitch, 8)
    return o_re, o_im, jnp.concatenate([asc_ref[pl.ds(o_re, n2), :], asc_ref[pl.ds(o_im, n2), :]], axis=0)


def _fused_filter_body(f_ref, g_hi_ref, g_lo_ref, fb_hi_ref, fb_lo_ref, h_ref, asc_ref, *, n1, n2, pitch):
    _stage_one(f_ref, g_hi_ref, g_lo_ref, asc_ref, n2=n2, nt1=n1, n1=n1, pitch=pitch)

    def body(k1, c):
        _, _, ain = _plane(asc_ref, k1, n1, n2, pitch)
        h_ref[k1] = _dot3c(fb_hi_ref[...], fb_lo_ref[...], ain)
        return c
    lax.fori_loop(0, n1, body, 0, unroll=8)


def _fused_filter_call(f, plan):
    order, big, c = f.shape
    n1, n2 = plan['n1'], plan['n2']
    pitch = _plane_pitch(n2)
    g_hi, g_lo = _hilo(plan['ga'])
    fb_hi, fb_lo = _hilo(plan['fb'])
    const = lambda a: pl.BlockSpec(a.shape, lambda o, j: (0,) * a.ndim)
    return pl.pallas_call(
        functools.partial(_fused_filter_body, n1=n1, n2=n2, pitch=pitch),
        out_shape=jax.ShapeDtypeStruct((order, c // LANES, n1, 2 * n2, LANES), F32),
        grid=(order, c // LANES),
        in_specs=[pl.BlockSpec((None, big, LANES), lambda o, j: (o, 0, j)),
                  const(g_hi), const(g_lo), const(fb_hi), const(fb_lo)],
        out_specs=pl.BlockSpec((None, None, n1, 2 * n2, LANES), lambda o, j: (o, j, 0, 0, 0)),
        scratch_shapes=[pltpu.VMEM((2 * n1 * pitch, LANES), F32)],
        compiler_params=_cparams(("parallel", "parallel"), VMEM_LIMIT),
        name="hyena_filter_spectrum",
    )(f, g_hi, g_lo, fb_hi, fb_lo)


def _fused_conv_body(x_ref, xg_ref, skip_ref, h_ref, g_ref, fb_ref, fbi_ref, o_ref, asc_ref, y_ref,
                     *, n1, n2, pitch):
    h1 = n1 // 2
    _stage_one(x_ref, g_ref, None, asc_ref, n2=n2, nt1=h1, n1=n1, pitch=pitch)

    def mid(k1, c):
        o_re, o_im, ain = _plane(asc_ref, k1, n1, n2, pitch)
        b = _gdot(fb_ref[...], None, ain)
        h = h_ref[k1]
        br, bi, hr, hi = b[:n2], b[n2:], h[:n2], h[n2:]
        z = jnp.concatenate([br * hr - bi * hi, br * hi + bi * hr], axis=0)
        cc = _gdot(fbi_ref[...], None, z)
        asc_ref[pl.ds(o_re, n2), :] = cc[:n2]
        asc_ref[pl.ds(o_im, n2), :] = cc[n2:]
        return c
    lax.fori_loop(0, n1, mid, 0, unroll=8)

    def last(t2, c):
        zin = asc_ref[pl.ds(t2, 2 * n1, stride=pitch), :]
        y_ref[pl.ds(t2, h1, stride=n2), :] = lax.dot_general(
            g_ref[t2], zin.astype(BF16), (((0,), (0,)), ((), ())), preferred_element_type=F32)
        return c
    lax.fori_loop(0, n2, last, 0, unroll=4)
    x = x_ref[...]
    o_ref[...] = xg_ref[...] * (y_ref[...] + x * skip_ref[...])


def _fused_conv_call(x, xg, skip, hspec, plan, row0_x, row0_g, bsz, n):
    assert row0_x % n == 0 and row0_g % n == 0
    c = x.shape[1]
    n1, n2 = plan['n1'], plan['n2']
    h1 = n1 // 2
    pitch = _plane_pitch(n2)
    g_hi = _hilo(plan['ga'][:, :, :h1])[0]
    fb_hi = _hilo(plan['fb'])[0]
    fbi_hi = _hilo(plan['fb_inv'])[0]
    const = lambda a: pl.BlockSpec(a.shape, lambda j, b: (0,) * a.ndim)
    seq = lambda row0: pl.BlockSpec((n, LANES), lambda j, b: (row0 // n + b, j))
    return pl.pallas_call(
        functools.partial(_fused_conv_body, n1=n1, n2=n2, pitch=pitch),
        out_shape=jax.ShapeDtypeStruct((bsz * n, c), F32),
        grid=(c // LANES, bsz),
        in_specs=[seq(row0_x), seq(row0_g), pl.BlockSpec((1, LANES), lambda j, b: (0, j)),
                  pl.BlockSpec((None, n1, 2 * n2, LANES), lambda j, b: (j, 0, 0, 0)),
                  const(g_hi), const(fb_hi), const(fbi_hi)],
        out_specs=pl.BlockSpec((n, LANES), lambda j, b: (b, j)),
        scratch_shapes=[pltpu.VMEM((2 * n1 * pitch, LANES), F32), pltpu.VMEM((n, LANES), F32)],
        compiler_params=_cparams(("parallel", "parallel"), VMEM_LIMIT),
        name="hyena_conv_long",
    )(x, xg, skip.reshape(1, c), hspec, g_hi, fb_hi, fbi_hi)


def _hy_gate_body(x_ref, y_ref, z_ref, s_ref, o_ref):
    o_ref[...] = x_ref[...] * (y_ref[...] + z_ref[...] * s_ref[...])


def _hy_gate_call(x, y, z, skip):
    t, c = x.shape
    tok = pl.BlockSpec((512, c), lambda i: (i, 0))
    return pl.pallas_call(
        _hy_gate_body,
        out_shape=jax.ShapeDtypeStruct((t, c), F32),
        grid=(t // 512,),
        in_specs=[tok, tok, tok, pl.BlockSpec((1, c), lambda i: (0, 0))],
        out_specs=tok,
        compiler_params=_cparams(("parallel",)),
        name="hyena_gate",
    )(x, y, z, skip.reshape(1, c))


def _hyena_branch(hy, p, groups, n_ctx_blocks):
    u, x1, x2 = _hy_pre_call(hy, p['hy_conv_w'], p['hy_conv_b'], n_ctx_blocks)
    outs = []
    for start, bsz, n in groups:
        rows = bsz * n
        plan = _dft_plan(n)
        hmul = _long_conv_setup(n, _hy_filter_call(n, p))
        if plan['stages'] == 1:
            ug, x1g, x2g = (a[start:start + rows] for a in (u, x1, x2))
            y = _long_conv(ug.reshape(bsz, n, HY_WIDTH), hmul[0]).reshape(rows, HY_WIDTH)
            z = _hy_gate_call(x1g, y, ug, p['hy_skip'][0])
            y = _long_conv(z.reshape(bsz, n, HY_WIDTH), hmul[1]).reshape(rows, HY_WIDTH)
            outs.append(_hy_gate_call(x2g, y, z, p['hy_skip'][1]))
        else:
            z = _fused_conv_call(u, x1, p['hy_skip'][0], hmul[0], plan, start, start, bsz, n)
            outs.append(_fused_conv_call(z, x2, p['hy_skip'][1], hmul[1], plan, 0, start, bsz, n))
    return jnp.concatenate(outs, axis=0)


def _merge_body(x_ref, yfw_ref, ybw_ref, bonus_ref, g_ref, yb_ref, gt_ref, g1_ref, sh2_ref, sc2_ref, seg_ref,
                gng_ref, gnb_ref, wpa_ref, wpb_ref, wout_ref, lng_ref, lnb_ref, rwh_ref, rwl_ref, rb_ref,
                x1_ref, h2_ref, ti_ref, tg_ref, *, alpha):
    seg = seg_ref[...]
    y = yfw_ref[...] + ybw_ref[...]
    inv = 1.0 / RW_HEAD
    mu = _dot_x2(y, seg) * inv
    yc = y - mu
    var = _dot_x2(yc * yc, seg) * inv
    yn = yc * lax.rsqrt(var + GN_EPS) * gng_ref[...] + gnb_ref[...]
    y_a = (yn + bonus_ref[...]) * g_ref[...]
    gt = gt_ref[...]
    merged = gt[:, :D_MODEL] * _bdot(y_a, wpa_ref[...]) + gt[:, D_MODEL:] * _bdot(yb_ref[...], wpb_ref[...])
    mix = _bdot(merged, wout_ref[...])
    x1 = _layer_norm(alpha * x_ref[...] + g1_ref[...] * mix, lng_ref[...], lnb_ref[...])
    x1_ref[...] = x1
    h2 = x1 * (1.0 + sc2_ref[...]) + sh2_ref[...]
    _store_token_tiles(h2_ref, 0, h2)
    h_hi, h_lo = _split2(h2)
    d = functools.partial(jnp.dot, preferred_element_type=F32)
    logits = d(h_hi, rwh_ref[...]) + (d(h_hi, rwl_ref[...]) + d(h_lo, rwh_ref[...])) + rb_ref[...]
    lane = lax.broadcasted_iota(jnp.int32, logits.shape, 1)
    neg = jnp.float32(-jnp.inf)
    cur = jnp.where(lane < N_EXPERTS, logits, neg)
    top_i = jnp.zeros(logits.shape, jnp.int32)
    top_e = jnp.zeros(logits.shape, F32)
    den = jnp.zeros((logits.shape[0], 1), F32)
    v0 = None
    for j in range(TOP_K):
        mx = jnp.max(cur, axis=-1, keepdims=True)
        idx = jnp.min(jnp.where(cur == mx, lane, LANES), axis=-1, keepdims=True)
        if j == 0:
            v0 = mx
        e = jnp.exp(mx - v0)
        den = den + e
        top_i = jnp.where(lane == j, idx, top_i)
        top_e = jnp.where(lane == j, e, top_e)
        cur = jnp.where(lane == idx, neg, cur)
    ti_ref[...] = top_i
    tg_ref[...] = top_e / den


def _merge_call(x, yfw, ybw, bonus, g, yb, gates, mod_l, p, seg, cond_idx, alpha):
    t = x.shape[0]
    w = RW_WIDTH
    tok = lambda n: pl.BlockSpec((TOKEN_BLOCK, n), lambda i: (i, 0))
    modspec = lambda j: pl.BlockSpec((None, 1, D_MODEL), lambda i: (cond_idx(i), 0, j))
    full = lambda *s: pl.BlockSpec(s, lambda i: (0,) * len(s))
    rw_pad = jnp.zeros((D_MODEL, LANES), F32).at[:, :N_EXPERTS].set(p['router_w'])
    rw_hi = rw_pad.astype(BF16)
    rw_lo = (rw_pad - rw_hi.astype(F32)).astype(BF16)
    rb = jnp.zeros((1, LANES), F32).at[0, :N_EXPERTS].set(p['router_b'])
    o = jax.ShapeDtypeStruct((t, D_MODEL), F32)
    return pl.pallas_call(
        functools.partial(_merge_body, alpha=alpha),
        out_shape=(o, jax.ShapeDtypeStruct((t * TOKEN_TILE, LANES), F32),
                   jax.ShapeDtypeStruct((t, LANES), jnp.int32), jax.ShapeDtypeStruct((t, LANES), F32)),
        grid=(t // TOKEN_BLOCK,),
        in_specs=[tok(D_MODEL), tok(w), tok(w), tok(w), tok(w), tok(w),
                  tok(2 * D_MODEL), modspec(2), modspec(3), modspec(4), full(w, w), full(1, w), full(1, w),
                  full(w, D_MODEL), full(w, D_MODEL), full(D_MODEL, D_MODEL), full(1, D_MODEL), full(1, D_MODEL),
                  full(D_MODEL, LANES), full(D_MODEL, LANES), full(1, LANES)],
        out_specs=(tok(D_MODEL), pl.BlockSpec((TOKEN_BLOCK * TOKEN_TILE, LANES), lambda i: (i, 0)),
                   tok(LANES), tok(LANES)),
        compiler_params=_cparams(("parallel",), VMEM_LIMIT),
        name="merge_ln_router",
    )(x, yfw, ybw, bonus, g, yb, gates, mod_l, mod_l, mod_l, seg, p['gn_g'].reshape(1, w), p['gn_b'].reshape(1, w),
      p['w_pa'].astype(BF16), p['w_pb'].astype(BF16), p['w_out'].astype(BF16),
      p['ln1_g'].reshape(1, D_MODEL), p['ln1_b'].reshape(1, D_MODEL), rw_hi, rw_lo, rb)


DEINT_COLS = 256
DEINT_GROUPS = 4


def _deint_body(w_ref, p_ref, g_ref, l_ref):
    half = DEINT_COLS // 2
    for q in range(DEINT_GROUPS):
        y = jnp.dot(w_ref[:, q * DEINT_COLS:(q + 1) * DEINT_COLS].astype(BF16), p_ref[...],
                    preferred_element_type=F32)
        g_ref[:, q * half:(q + 1) * half] = y[:, :half].astype(BF16)
        l_ref[:, q * half:(q + 1) * half] = y[:, half:].astype(BF16)


def _deint_call(w, layer):
    _, e, k, n2 = w.shape
    half = DEINT_COLS // 2
    step_cols = DEINT_COLS * DEINT_GROUPS
    sel = np.zeros((DEINT_COLS, DEINT_COLS), np.float32)
    sel[2 * np.arange(half), np.arange(half)] = 1.0
    sel[2 * np.arange(half) + 1, half + np.arange(half)] = 1.0
    o = jax.ShapeDtypeStruct((e, k, n2 // 2), BF16)
    return pl.pallas_call(
        _deint_body,
        out_shape=(o, o),
        grid=(e, n2 // step_cols),
        in_specs=[pl.BlockSpec((None, None, k, step_cols), lambda i, j: (layer, i, 0, j)),
                  pl.BlockSpec((DEINT_COLS, DEINT_COLS), lambda i, j: (0, 0))],
        out_specs=(pl.BlockSpec((None, k, step_cols // 2), lambda i, j: (i, 0, j)),) * 2,
        compiler_params=_cparams(("parallel", "parallel")),
        name="expert_w_split",
    )(w, jnp.asarray(sel, BF16))


def _moe_body(blk_e_ref, n_on_ref, tok_ref, tokn_ref, dst_ref, h_hbm, wg_ref, wl_ref, bg_ref, bl_ref, wdn_ref,
              bdn_ref, y_hbm, xbuf, ybuf, sem_in, sem_out, *, n_real):
    i = pl.program_id(0)
    n_on = n_on_ref[0]
    slot = i % 2
    tt = TOKEN_TILE
    slot_rows = MOE_ROWS * tt

    def tile(ref, idx):
        return ref.at[pl.ds(pl.multiple_of(idx * tt, tt), tt), :]

    def gather_start(tref, s):
        def body(g, c):
            for u in range(DMA_GROUP):
                r = g * DMA_GROUP + u
                pltpu.make_async_copy(tile(h_hbm, tref[0, r]), tile(xbuf, s * MOE_ROWS + r),
                                      sem_in.at[s]).start(priority=u % 2)
            return c
        lax.fori_loop(0, MOE_ROWS // DMA_GROUP, body, 0)

    def slot_buf(buf, s):
        return buf.at[pl.ds(pl.multiple_of(s * slot_rows, slot_rows), slot_rows), :]

    def gather_wait(s):
        pltpu.make_async_copy(h_hbm.at[pl.ds(0, slot_rows), :], slot_buf(xbuf, s), sem_in.at[s]).wait()

    def scatter_start(s):
        def body(g, c):
            for u in range(DMA_GROUP):
                r = g * DMA_GROUP + u
                pltpu.make_async_copy(tile(ybuf, s * MOE_ROWS + r), tile(y_hbm, dst_ref[0, r]),
                                      sem_out.at[s]).start(priority=u % 2)
            return c
        lax.fori_loop(0, MOE_ROWS // DMA_GROUP, body, 0)

    def scatter_wait(s):
        pltpu.make_async_copy(slot_buf(ybuf, s), y_hbm.at[pl.ds(0, slot_rows), :], sem_out.at[s]).wait()

    @pl.when(i == 0)
    def _():
        ybuf[...] = jnp.zeros_like(ybuf)
        fills = [pltpu.make_async_copy(slot_buf(ybuf, s),
                                       y_hbm.at[pl.ds((n_real + s * MOE_ROWS) * tt, slot_rows), :],
                                       sem_out.at[s]) for s in range(2)]
        for cp in fills:
            cp.start()
        for cp in fills:
            cp.wait()

    @pl.when(jnp.logical_and(i == 0, n_on > 0))
    def _():
        gather_start(tok_ref, 0)

    @pl.when(i < n_on)
    def _():
        @pl.when(i + 1 < n_on)
        def _():
            gather_start(tokn_ref, 1 - slot)

        gather_wait(slot)
        row0 = slot * slot_rows
        x = _load_token_tiles(xbuf, row0, MOE_ROWS).astype(BF16)
        d = functools.partial(jnp.dot, preferred_element_type=F32)
        glu = jnp.minimum(d(x, wg_ref[...]) + bg_ref[...], SWIGLU_LIMIT)
        lin = jnp.clip(d(x, wl_ref[...]) + bl_ref[...], -SWIGLU_LIMIT, SWIGLU_LIMIT)
        act = glu * _sigmoid(SWIGLU_ALPHA * glu) * (lin + 1.0)
        _store_token_tiles(ybuf, row0, d(act.astype(BF16), wdn_ref[...]) + bdn_ref[...])
        scatter_start(slot)

        @pl.when(i >= 1)
        def _():
            scatter_wait(1 - slot)

        @pl.when(i == n_on - 1)
        def _():
            scatter_wait(slot)


def _moe_call(h2, top_i, wg, wl, bg, bl, wdn, bdn):
    t = h2.shape[0] // TOKEN_TILE
    m = t * TOP_K
    e = N_EXPERTS
    blk = MOE_ROWS
    flat_e = top_i.reshape(-1)
    order = jnp.argsort(flat_e, stable=True).astype(jnp.int32)
    sizes = jnp.bincount(flat_e, length=e).astype(jnp.int32)
    padded = (sizes + blk - 1) // blk * blk
    pad_end = jnp.cumsum(padded)
    pad_start = pad_end - padded
    grp_start = jnp.cumsum(sizes) - sizes
    n_blocks = -(-(m + e * (blk - 1)) // blk)
    blk_first = jnp.arange(n_blocks, dtype=jnp.int32) * blk
    blk_e = jnp.minimum(jnp.sum((pad_end[None, :] <= blk_first[:, None]).astype(jnp.int32), axis=1), e - 1)
    pidx = jnp.arange(n_blocks * blk, dtype=jnp.int32)
    e_p = jnp.repeat(blk_e, blk)
    idx = pidx - pad_start[e_p]
    valid = idx < sizes[e_p]
    assign = order[jnp.clip(grp_start[e_p] + idx, 0, m - 1)]
    tok_row = jnp.where(valid, assign // TOP_K, 0).astype(jnp.int32)
    spare = m + ((pidx // blk) % 2) * blk + pidx % blk
    dst_row = jnp.where(valid, (assign % TOP_K) * t + assign // TOP_K, spare).astype(jnp.int32)
    n_on = (pad_end[-1] // blk).astype(jnp.int32).reshape(1)
    tok3 = tok_row.reshape(n_blocks, 1, blk)

    smem = lambda f: pl.BlockSpec((None, 1, blk), f, memory_space=pltpu.SMEM)
    wspec = lambda a, b: pl.BlockSpec((None, a, b), lambda i, be, no: (be[i], 0, 0))
    gs = pltpu.PrefetchScalarGridSpec(
        num_scalar_prefetch=2,
        grid=(n_blocks,),
        in_specs=[smem(lambda i, be, no: (i, 0, 0)),
                  smem(lambda i, be, no: (jnp.minimum(i + 1, n_blocks - 1), 0, 0)),
                  smem(lambda i, be, no: (i, 0, 0)),
                  pl.BlockSpec(memory_space=pl.ANY),
                  wspec(D_MODEL, D_FF), wspec(D_MODEL, D_FF), wspec(1, D_FF), wspec(1, D_FF),
                  wspec(D_FF, D_MODEL), wspec(1, D_MODEL)],
        out_specs=pl.BlockSpec(memory_space=pl.ANY),
        scratch_shapes=[pltpu.VMEM((2 * blk * TOKEN_TILE, LANES), F32), pltpu.VMEM((2 * blk * TOKEN_TILE, LANES), F32),
                        pltpu.SemaphoreType.DMA((2,)), pltpu.SemaphoreType.DMA((2,))],
    )
    return pl.pallas_call(
        functools.partial(_moe_body, n_real=m),
        out_shape=jax.ShapeDtypeStruct(((m + 2 * blk) * TOKEN_TILE, LANES), F32),
        grid_spec=gs,
        compiler_params=_cparams(("arbitrary",), VMEM_LIMIT),
        name="moe_experts",
    )(blk_e, n_on, tok3, tok3, dst_row.reshape(n_blocks, 1, blk), h2, wg, wl, bg.reshape(e, 1, D_FF),
      bl.reshape(e, 1, D_FF), wdn, bdn.reshape(e, 1, D_MODEL))


def _combine_body(x1_ref, y0_ref, y1_ref, y2_ref, y3_ref, tg_ref, g2_ref, lng_ref, lnb_ref, o_ref, *, alpha):
    tg = tg_ref[...]
    rows = x1_ref.shape[0]
    moe = tg[:, 0:1] * _load_token_tiles(y0_ref, 0, rows)
    for j, y_ref in enumerate((y1_ref, y2_ref, y3_ref), start=1):
        moe = moe + tg[:, j:j + 1] * _load_token_tiles(y_ref, 0, rows)
    o_ref[...] = _layer_norm(alpha * x1_ref[...] + g2_ref[...] * moe, lng_ref[...], lnb_ref[...])


def _combine_call(x1, yexp, tg, mod_l, p, cond_idx, alpha):
    t = x1.shape[0]
    nb = t // TOKEN_BLOCK
    tok = lambda n: pl.BlockSpec((TOKEN_BLOCK, n), lambda i: (i, 0))
    full = lambda *s: pl.BlockSpec(s, lambda i: (0,) * len(s))
    yspec = [pl.BlockSpec((TOKEN_BLOCK * TOKEN_TILE, LANES), lambda i, j=j: (j * nb + i, 0)) for j in range(TOP_K)]
    return pl.pallas_call(
        functools.partial(_combine_body, alpha=alpha),
        out_shape=jax.ShapeDtypeStruct((t, D_MODEL), F32),
        grid=(nb,),
        in_specs=[tok(D_MODEL)] + yspec + [tok(LANES),
                  pl.BlockSpec((None, 1, D_MODEL), lambda i: (cond_idx(i), 0, 5)),
                  full(1, D_MODEL), full(1, D_MODEL)],
        out_specs=tok(D_MODEL),
        compiler_params=_cparams(("parallel",), VMEM_LIMIT),
        name="moe_combine_ln",
    )(x1, yexp, yexp, yexp, yexp, tg, mod_l, p['ln2_g'].reshape(1, D_MODEL), p['ln2_b'].reshape(1, D_MODEL))


def kernel(x_prompt, x_sample, c, state_rwkv, c_ctx, w_mod, b_mod, w_in, mu_shift, w0, w_lora_up, a0, a_lora_up, g_up, k_k, k_a, r_k, gn_g, gn_b, hy_conv_w, hy_conv_b, hy_f_w1, hy_f_b1, hy_f_w2, hy_f_b2, hy_f_freq, hy_f_w3, hy_skip, w_pa, w_pb, w_out, ln1_g, ln1_b, ln2_g, ln2_b, router_w, router_b, ex_w_up, ex_b_up, ex_w_down, ex_b_down):
    bsz, seq, dm = x_prompt.shape
    dbsz, dseq, _ = x_sample.shape
    depth = w_mod.shape[0]
    assert dm == D_MODEL and seq == TOKEN_BLOCK and dseq % TOKEN_BLOCK == 0 and TOKEN_BLOCK % GRID_W == 0
    assert 1 + dbsz <= 8
    alpha = (2 * depth) ** 0.25
    t_ctx = bsz * seq
    n_ctx_blocks = t_ctx // TOKEN_BLOCK
    lat_blocks = dseq // TOKEN_BLOCK

    def cond_idx(i):
        return jnp.where(i < n_ctx_blocks, 0, 1 + (i - n_ctx_blocks) // lat_blocks)

    x = jnp.concatenate([x_prompt.reshape(t_ctx, dm), x_sample.reshape(dbsz * dseq, dm)], axis=0)
    cond8 = jnp.zeros((8, dm), F32).at[0].set(c_ctx).at[1:1 + dbsz].set(c)
    mod = _mod_call(cond8, w_mod, b_mod)

    hd = RW_HEAD
    seg = (np.arange(RW_WIDTH)[:, None] // hd == np.arange(RW_WIDTH)[None, :] // hd)
    seg = jnp.asarray(seg, BF16)
    seq_lens = [seq] * bsz + [dseq] * dbsz
    tbl_np, nsteps = _scan_table(seq_lens)
    tbl = jnp.asarray(tbl_np)
    groups = [(0, bsz, seq), (t_ctx, dbsz, dseq)]

    new_states = []
    for l in range(depth):
        p = dict(mu_shift=mu_shift[l], w0=w0[l], w_lora_up=w_lora_up[l], a0=a0[l], a_lora_up=a_lora_up[l],
                 g_up=g_up[l], k_k=k_k[l], k_a=k_a[l], r_k=r_k[l], gn_g=gn_g[l], gn_b=gn_b[l],
                 hy_conv_w=hy_conv_w[l], hy_conv_b=hy_conv_b[l], hy_f_w1=hy_f_w1[l], hy_f_b1=hy_f_b1[l],
                 hy_f_w2=hy_f_w2[l], hy_f_b2=hy_f_b2[l], hy_f_freq=hy_f_freq[l], hy_f_w3=hy_f_w3[l],
                 hy_skip=hy_skip[l], w_pa=w_pa[l], w_pb=w_pb[l], w_out=w_out[l], ln1_g=ln1_g[l], ln1_b=ln1_b[l],
                 ln2_g=ln2_g[l], ln2_b=ln2_b[l], router_w=router_w[l], router_b=router_b[l])
        mod_l = mod[l].reshape(8, 1, 6 * dm)
        rw, hy, gates = _inproj_call(x, mod_l, w_in[l].astype(BF16), cond_idx)
        r, kk, v, lw, b, kd, g, bonus = _rwkv_pre_call(rw, p, seg, n_ctx_blocks)
        s0 = jnp.concatenate([jnp.zeros((bsz, 2, RW_HEADS, hd, hd), F32), state_rwkv[:, l].astype(F32)], axis=0)
        yfw, ybw, sfin = _scan_call(tbl, nsteps, bsz + dbsz, r, kk, v, lw, b, kd, s0)
        new_states.append(sfin[:bsz].astype(x_prompt.dtype))
        yb = _hyena_branch(hy, p, groups, n_ctx_blocks)
        x1, h2, top_i, top_g = _merge_call(x, yfw, ybw, bonus, g, yb, gates, mod_l, p, seg, cond_idx, alpha)
        wg, wl = _deint_call(ex_w_up, l)
        yexp = _moe_call(h2, top_i[:, :TOP_K], wg, wl, ex_b_up[l][:, 0::2], ex_b_up[l][:, 1::2],
                         ex_w_down[l].astype(BF16), ex_b_down[l])
        x = _combine_call(x1, yexp, top_g, mod_l, p, cond_idx, alpha)

    y_p = x[:t_ctx].reshape(bsz, seq, dm)
    y_s = x[t_ctx:].reshape(dbsz, dseq, dm)
    return (y_p, y_s, jnp.stack(new_states, axis=1))
```
